```python
import math
import jax, jax.numpy as jnp
from jax import lax
import numpy as np

D_MODEL = 2048
BATCH = 4
SEQ = 4096
DEPTH = 1

DN_HEADS = D_MODEL // 256
DN_HEAD_DIM = 128
DN_DIM = DN_HEADS * DN_HEAD_DIM
CONV_WIDTH = 4
CHUNK = 64
DF_HEADS = D_MODEL // 512
DF_HEAD_DIM = 128
DF_DIM = DF_HEADS * 2 * DF_HEAD_DIM
Q_BLOCK = 128
NUM_BUCKETS = 32
MAX_DISTANCE = 128
D_FF = 4 * D_MODEL
IN_COLS = 4 * DN_DIM + 2 * DN_HEADS + 3 * DF_DIM
MIX_WIDTH = DN_DIM + DF_DIM

kernel_name = "hybrid_gdn_diffattn_block"


def rms_norm(x, g, eps=1e-6):
    xf = x.astype(jnp.float32)
    y = xf * lax.rsqrt(jnp.mean(xf * xf, axis=-1, keepdims=True) + eps)
    return (y * g.astype(jnp.float32)).astype(x.dtype)


def l2_norm(x, eps=1e-6):
    xf = x.astype(jnp.float32)
    return xf * lax.rsqrt(jnp.sum(xf * xf, axis=-1, keepdims=True) + eps)


def causal_dwconv(x, w):
    K, C = w.shape
    return lax.conv_general_dilated(
        x, w[:, None, :], window_strides=(1,), padding=[(K - 1, 0)],
        dimension_numbers=("NWC", "WIO", "NWC"), feature_group_count=C)


def t5_causal_bucket(n):
    max_exact = NUM_BUCKETS // 2
    n = jnp.maximum(n, 0)
    nf = jnp.maximum(n, 1).astype(jnp.float32)
    large = max_exact + (jnp.log(nf / max_exact) / math.log(MAX_DISTANCE / max_exact)
                         * (NUM_BUCKETS - max_exact)).astype(jnp.int32)
    large = jnp.minimum(large, NUM_BUCKETS - 1)
    return jnp.where(n < max_exact, n, large)


def gated_delta_rule(q, k, v, g, beta):
    B, S, H, dk = q.shape
    dv = v.shape[-1]
    C = CHUNK
    N = S // C
    f32 = jnp.float32

    def chunks(t):
        t = t.astype(f32).reshape((B, N, C, H) + t.shape[3:])
        perm = (1, 0, 3, 2) + tuple(range(4, t.ndim))
        return t.transpose(perm)

    q = chunks(q) * (dk ** -0.5)
    k = chunks(k)
    v = chunks(v)
    g = chunks(g)
    beta = chunks(beta)

    gc = jnp.cumsum(g, axis=-1)
    tril = jnp.tril(jnp.ones((C, C), dtype=bool))
    strict = jnp.tril(jnp.ones((C, C), dtype=bool), -1)
    decay = jnp.exp(jnp.where(tril, gc[..., :, None] - gc[..., None, :], -jnp.inf))

    kb = k * beta[..., None]
    vb = v * beta[..., None]
    L = jnp.where(strict, jnp.einsum("nbhid,nbhjd->nbhij", kb, k) * decay, 0.0)
    eye = jnp.eye(C, dtype=f32)
    T = lax.linalg.triangular_solve(L + eye, jnp.broadcast_to(eye, L.shape),
                                    left_side=True, lower=True, unit_diagonal=True)
    u = jnp.einsum("nbhij,nbhjd->nbhid", T, vb)
    w = jnp.einsum("nbhij,nbhjd->nbhid", T, kb * jnp.exp(gc)[..., None])
    qk = jnp.where(tril, jnp.einsum("nbhid,nbhjd->nbhij", q, k) * decay, 0.0)

    def step(state, inp):
        qi, ki, ui, wi, gci, qki = inp
        v_new = ui - jnp.einsum("bhck,bhkv->bhcv", wi, state)
        o = (jnp.einsum("bhck,bhkv->bhcv", qi * jnp.exp(gci)[..., None], state)
             + jnp.einsum("bhij,bhjv->bhiv", qki, v_new))
        glast = gci[..., -1]
        kdec = ki * jnp.exp(glast[..., None] - gci)[..., None]
        state = state * jnp.exp(glast)[..., None, None] + jnp.einsum("bhck,bhcv->bhkv", kdec, v_new)
        return state, o

    s0 = jnp.zeros((B, H, dk, dv), f32)
    _, o = lax.scan(step, s0, (q, k, u, w, gc, qk))
    return o.transpose(1, 0, 3, 2, 4).reshape(B, S, H, dv)


def diff_attention(q, k, v, lam, bias_table):
    B, S, H, _, d = q.shape
    NB = S // Q_BLOCK
    qb_all = (q * (d ** -0.5)).reshape(B, NB, Q_BLOCK, H, 2, d).transpose(1, 0, 3, 4, 2, 5)
    kt = k.transpose(0, 2, 3, 1, 4)
    vt = v.transpose(0, 2, 1, 3)
    k_pos = jnp.arange(S)

    def block(args):
        qb, i = args
        q_pos = i * Q_BLOCK + jnp.arange(Q_BLOCK)
        rel = q_pos[:, None] - k_pos[None, :]
        bias = bias_table[t5_causal_bucket(rel)].astype(jnp.float32)
        bias = bias.transpose(2, 0, 1)
        logits = jnp.einsum("bhmqd,bhmkd->bhmqk", qb, kt).astype(jnp.float32)
        logits = logits + bias[None, :, None]
        logits = jnp.where((rel >= 0)[None, None, None], logits, -jnp.inf)
        p = jax.nn.softmax(logits, axis=-1)
        attn = p[:, :, 0] - lam * p[:, :, 1]
        return jnp.einsum("bhqk,bhkv->bhqv", attn.astype(vt.dtype), vt).astype(jnp.float32)

    o = lax.map(block, (qb_all, jnp.arange(NB)))
    return o.transpose(1, 0, 3, 2, 4).reshape(B, S, H, 2 * d)


def setup_inputs(seed: int = 0) -> dict:
    key = jax.random.key(seed)
    ks = jax.random.split(key, 20)
    f32 = jnp.float32
    nrm = lambda k, shape, s: jax.random.normal(k, shape, f32) * s
    x = jax.random.normal(ks[0], (BATCH, SEQ, D_MODEL), f32)
    attn_norm = 1.0 + nrm(ks[1], (DEPTH, D_MODEL), 0.02)
    w_in = nrm(ks[2], (DEPTH, D_MODEL, IN_COLS), D_MODEL ** -0.5)
    conv_w = nrm(ks[3], (DEPTH, CONV_WIDTH, 3 * DN_DIM), CONV_WIDTH ** -0.5)
    a_log = jnp.log(jax.random.uniform(ks[4], (DEPTH, DN_HEADS), f32, 1.0, 16.0))
    dt = jnp.exp(jax.random.uniform(ks[5], (DEPTH, DN_HEADS), f32, math.log(1e-3), math.log(1e-1)))
    dt_bias = dt + jnp.log(-jnp.expm1(-dt))
    dn_norm = 1.0 + nrm(ks[6], (DEPTH, DN_HEAD_DIM), 0.02)
    lambda_q1 = nrm(ks[7], (DEPTH, DF_HEAD_DIM), 0.1)
    lambda_k1 = nrm(ks[8], (DEPTH, DF_HEAD_DIM), 0.1)
    lambda_q2 = nrm(ks[9], (DEPTH, DF_HEAD_DIM), 0.1)
    lambda_k2 = nrm(ks[10], (DEPTH, DF_HEAD_DIM), 0.1)
    df_norm = 1.0 + nrm(ks[11], (DEPTH, 2 * DF_HEAD_DIM), 0.02)
    rel_bias = nrm(ks[12], (NUM_BUCKETS, DF_HEADS), 0.2)
    w_o = nrm(ks[13], (DEPTH, MIX_WIDTH, D_MODEL), MIX_WIDTH ** -0.5)
    mlp_norm = 1.0 + nrm(ks[14], (DEPTH, D_MODEL), 0.02)
    w_up = nrm(ks[15], (DEPTH, D_MODEL, D_FF), D_MODEL ** -0.5)
    w_down = nrm(ks[16], (DEPTH, D_FF, D_MODEL), D_FF ** -0.5)
    final_norm = 1.0 + nrm(ks[17], (D_MODEL,), 0.02)
    return {"x": x, "attn_norm": attn_norm, "w_in": w_in, "conv_w": conv_w,
            "a_log": a_log, "dt_bias": dt_bias, "dn_norm": dn_norm,
            "lambda_q1": lambda_q1, "lambda_k1": lambda_k1,
            "lambda_q2": lambda_q2, "lambda_k2": lambda_k2, "df_norm": df_norm,
            "rel_bias": rel_bias, "w_o": w_o, "mlp_norm": mlp_norm,
            "w_up": w_up, "w_down": w_down, "final_norm": final_norm}


def reference(x, attn_norm, w_in, conv_w, a_log, dt_bias, dn_norm,
              lambda_q1, lambda_k1, lambda_q2, lambda_k2, df_norm,
              rel_bias, w_o, mlp_norm, w_up, w_down, final_norm):
    B, S, _ = x.shape
    f32 = jnp.float32
    sizes = [3 * DN_DIM, DN_DIM, DN_HEADS, DN_HEADS, DF_DIM, DF_DIM, DF_DIM]
    cuts = list(np.cumsum(sizes)[:-1])
    h = x
    for l in range(DEPTH):
        u = rms_norm(h, attn_norm[l])
        proj = jnp.einsum("bsd,de->bse", u, w_in[l])
        dn_qkv, dn_z, dn_b, dn_a, df_q, df_k, df_v = jnp.split(proj, cuts, axis=-1)

        qkv = jax.nn.silu(causal_dwconv(dn_qkv, conv_w[l]))
        q, k, v = jnp.split(qkv, 3, axis=-1)
        q = l2_norm(q.reshape(B, S, DN_HEADS, DN_HEAD_DIM))
        k = l2_norm(k.reshape(B, S, DN_HEADS, DN_HEAD_DIM))
        v = v.reshape(B, S, DN_HEADS, DN_HEAD_DIM)
        beta = jax.nn.sigmoid(dn_b.astype(f32))
        g = -jnp.exp(a_log[l].astype(f32)) * jax.nn.softplus(dn_a.astype(f32) + dt_bias[l].astype(f32))
        o_dn = gated_delta_rule(q, k, v, g, beta)
        z = dn_z.reshape(B, S, DN_HEADS, DN_HEAD_DIM).astype(f32)
        o_dn = rms_norm(o_dn, dn_norm[l]) * jax.nn.silu(z)
        y_dn = o_dn.reshape(B, S, DN_DIM).astype(x.dtype)

        lam_init = 0.8 - 0.6 * math.exp(-0.3 * l)
        lam = (jnp.exp(jnp.sum(lambda_q1[l].astype(f32) * lambda_k1[l].astype(f32)))
               - jnp.exp(jnp.sum(lambda_q2[l].astype(f32) * lambda_k2[l].astype(f32)))
               + lam_init)
        dq = df_q.reshape(B, S, DF_HEADS, 2, DF_HEAD_DIM)
        dk = df_k.reshape(B, S, DF_HEADS, 2, DF_HEAD_DIM)
        dv = df_v.reshape(B, S, DF_HEADS, 2 * DF_HEAD_DIM)
        o_df = diff_attention(dq, dk, dv, lam, rel_bias)
        o_df = rms_norm(o_df, df_norm[l], eps=1e-5) * (1.0 - lam_init)
        y_df = o_df.reshape(B, S, DF_DIM).astype(x.dtype)

        mix = jnp.concatenate([y_dn, y_df], axis=-1)
        h = h + jnp.einsum("bse,ed->bsd", mix, w_o[l])

        u = rms_norm(h, mlp_norm[l])
        hid = jnp.square(jax.nn.relu(jnp.einsum("bsd,df->bsf", u, w_up[l])))
        h = h + jnp.einsum("bsf,fd->bsd", hid, w_down[l])
    return rms_norm(h, final_norm)
```

```python
import functools
import math

import numpy as np
import jax
import jax.numpy as jnp
from jax import lax
from jax.experimental import pallas as pl
from jax.experimental.pallas import tpu as pltpu

F32 = jnp.float32
BF16 = jnp.bfloat16

DN_HEAD_DIM = 128
DF_HEAD_DIM = 128
CONV_WIDTH = 4
CHUNK = 64
NUM_BUCKETS = 32
MAX_DISTANCE = 128
GATE_COLS = 128

V7X_VMEM_LIMIT = 56 * 1024 * 1024


def _dot(a, b):
    return jnp.dot(a, b, preferred_element_type=F32)


def _dot_nt(a, b):
    return lax.dot_general(a, b, (((1,), (1,)), ((), ())), preferred_element_type=F32)


def _sigmoid(x):
    return 1.0 / (1.0 + jnp.exp(-x))


def _in_proj_kernel(x_ref, g_ref, w_ref, wg_ref, proj_ref, gate_ref, u_ref):
    @pl.when(pl.program_id(1) == 0)
    def _():
        x = x_ref[...]
        ms = jnp.mean(x * x, axis=-1, keepdims=True)
        u = (x * lax.rsqrt(ms + 1e-6) * g_ref[...]).astype(BF16)
        u_ref[...] = u
        gate_ref[...] = _dot(u, wg_ref[...])

    proj_ref[...] = _dot(u_ref[...], w_ref[...]).astype(BF16)


def _in_proj(x2, gain, w_main, w_gate, *, tm, tn):
    T, D = x2.shape
    N = w_main.shape[1]
    return pl.pallas_call(
        _in_proj_kernel,
        grid=(T // tm, N // tn),
        in_specs=[
            pl.BlockSpec((tm, D), lambda i, j: (i, 0)),
            pl.BlockSpec((1, D), lambda i, j: (0, 0)),
            pl.BlockSpec((D, tn), lambda i, j: (0, j)),
            pl.BlockSpec((D, GATE_COLS), lambda i, j: (0, 0)),
        ],
        out_specs=[
            pl.BlockSpec((tm, tn), lambda i, j: (i, j)),
            pl.BlockSpec((tm, GATE_COLS), lambda i, j: (i, 0)),
        ],
        out_shape=[
            jax.ShapeDtypeStruct((T, N), BF16),
            jax.ShapeDtypeStruct((T, GATE_COLS), F32),
        ],
        scratch_shapes=[pltpu.VMEM((tm, D), BF16)],
        compiler_params=pltpu.CompilerParams(
            dimension_semantics=("arbitrary", "arbitrary"),
            vmem_limit_bytes=V7X_VMEM_LIMIT),
        name="in_proj",
    )(x2, gain, w_main, w_gate)


def _unit_lower_inverse(n_mat):
    c = n_mat.shape[0]
    rows = lax.broadcasted_iota(jnp.int32, (c, c), 0)
    cols = lax.broadcasted_iota(jnp.int32, (c, c), 1)
    inv = jnp.where(rows == cols, 1.0, 0.0) + n_mat
    power = n_mat
    span = 2
    while span < c:
        pb = power.astype(BF16)
        power = _dot(pb, pb)
        inv = inv + _dot(inv.astype(BF16), power.astype(BF16))
        span *= 2
    return inv


def _gdn_kernel(alog_ref, dtb_ref, q_ref, k_ref, v_ref, z_ref, gate_ref,
                cwq_ref, cwk_ref, cwv_ref, dnn_ref, y_ref,
                state_ref, tail_ref, xbuf_ref, *, blk, n_heads):
    h = pl.program_id(1)
    t = pl.program_id(2)
    dk = DN_HEAD_DIM

    @pl.when(t == 0)
    def _():
        state_ref[...] = jnp.zeros_like(state_ref)
        tail_ref[...] = jnp.zeros_like(tail_ref)

    def conv_silu(i, x_ref, cw_ref):
        x = x_ref[...].astype(F32)
        xbuf_ref[i, 0:8, :] = tail_ref[i]
        xbuf_ref[i, 8:8 + blk, :] = x
        tail_ref[i] = x[blk - 8:blk, :]
        cw = cw_ref[...]
        y = x * cw[CONV_WIDTH - 1:CONV_WIDTH, :]
        for s in range(1, CONV_WIDTH):
            tap = CONV_WIDTH - 1 - s
            y = y + xbuf_ref[i, 8 - s:8 - s + blk, :] * cw[tap:tap + 1, :]
        return y * _sigmoid(y)

    q = conv_silu(0, q_ref, cwq_ref)
    k = conv_silu(1, k_ref, cwk_ref)
    v = conv_silu(2, v_ref, cwv_ref)
    q = q * lax.rsqrt(jnp.sum(q * q, axis=-1, keepdims=True) + 1e-6) * (dk ** -0.5)
    k = k * lax.rsqrt(jnp.sum(k * k, axis=-1, keepdims=True) + 1e-6)

    gate = gate_ref[...]
    lane = lax.broadcasted_iota(jnp.int32, gate.shape, 1)
    b_col = jnp.sum(jnp.where(lane == h, gate, 0.0), axis=-1, keepdims=True)
    a_col = jnp.sum(jnp.where(lane == n_heads + h, gate, 0.0), axis=-1, keepdims=True)
    b_bc = jnp.broadcast_to(b_col, (blk, dk))
    a_bc = jnp.broadcast_to(a_col, (blk, dk))
    beta = _sigmoid(b_bc)
    a_scale = jnp.exp(jnp.full((1, dk), alog_ref[h], F32))
    xs = a_bc + dtb_ref[h]
    softplus = jnp.maximum(xs, 0.0) + jnp.log(1.0 + jnp.exp(-jnp.abs(xs)))
    g = -a_scale * softplus

    pos = lax.broadcasted_iota(jnp.int32, (blk, dk), 0) & (CHUNK - 1)
    gc = g
    step = 1
    while step < CHUNK:
        gc = gc + jnp.where(pos >= step, pltpu.roll(gc, step, 0), 0.0)
        step *= 2

    ri = lax.broadcasted_iota(jnp.int32, (CHUNK, CHUNK), 0)
    ci = lax.broadcasted_iota(jnp.int32, (CHUNK, CHUNK), 1)
    tril = ri >= ci
    strict = ri > ci
    eye = ri == ci
    gain = dnn_ref[...]

    for c in range(blk // CHUNK):
        r0 = c * CHUNK
        qc = q[r0:r0 + CHUNK]
        kc = k[r0:r0 + CHUNK]
        vc = v[r0:r0 + CHUNK]
        bc = beta[r0:r0 + CHUNK]
        gcc = gc[r0:r0 + CHUNK]
        g_last = gcc[CHUNK - 1:CHUNK, :]
        eg = jnp.exp(gcc)

        g_sq = gcc[:, 0:CHUNK]
        g_row = jnp.sum(jnp.where(eye, g_sq, 0.0), axis=0, keepdims=True)
        decay = jnp.exp(jnp.where(tril, g_sq - g_row, -jnp.inf))

        kb = kc * bc
        vb = vc * bc
        k_bf = kc.astype(BF16)
        a_mat = _dot_nt(kb.astype(BF16), k_bf)
        n_mat = jnp.where(strict, -(a_mat * decay), 0.0)
        t_inv = _unit_lower_inverse(n_mat).astype(BF16)

        rhs = jnp.concatenate([kb * eg, vb], axis=1).astype(BF16)
        wu = _dot(t_inv, rhs)
        wu_bf = wu.astype(BF16)
        qk = jnp.where(tril, _dot_nt(qc.astype(BF16), k_bf) * decay, 0.0).astype(BF16)
        kdec = kc * jnp.exp(g_last - gcc)
        kd_t = kdec.T.astype(BF16)
        gr = _dot(kd_t, wu_bf)
        qw = _dot(qk, wu_bf)
        q_eff = (qc * eg - qw[:, :dk]).astype(BF16)
        p_loc = qw[:, dk:]

        state = state_ref[...]
        s_bf = state.astype(BF16)
        o = _dot(q_eff, s_bf) + p_loc
        state_ref[...] = (state * jnp.exp(g_last) - _dot(gr[:, :dk].astype(BF16), s_bf)
                          + gr[:, dk:])

        zc = z_ref[r0:r0 + CHUNK, :].astype(F32)
        o = o * lax.rsqrt(jnp.mean(o * o, axis=-1, keepdims=True) + 1e-6) * gain
        y_ref[r0:r0 + CHUNK, :] = (o * (zc * _sigmoid(zc))).astype(y_ref.dtype)


def _gdn(proj3, gate3, conv_w, a_log, dt_bias, dn_norm, *, n_heads, blk):
    B, S, _ = proj3.shape
    dk = DN_HEAD_DIM
    smem = pl.BlockSpec(memory_space=pltpu.SMEM)

    def head_cols(group):
        return pl.BlockSpec((None, blk, dk), lambda b, h, t: (b, t, group * n_heads + h))

    def conv_cols(group):
        return pl.BlockSpec((CONV_WIDTH, dk), lambda b, h, t: (0, group * n_heads + h))

    return pl.pallas_call(
        functools.partial(_gdn_kernel, blk=blk, n_heads=n_heads),
        grid=(B, n_heads, S // blk),
        in_specs=[
            smem, smem,
            head_cols(0), head_cols(1), head_cols(2), head_cols(3),
            pl.BlockSpec((None, blk, GATE_COLS), lambda b, h, t: (b, t, 0)),
            conv_cols(0), conv_cols(1), conv_cols(2),
            pl.BlockSpec((1, dk), lambda b, h, t: (0, 0)),
        ],
        out_specs=pl.BlockSpec((None, blk, dk), lambda b, h, t: (b, t, h)),
        out_shape=jax.ShapeDtypeStruct((B, S, n_heads * dk), BF16),
        scratch_shapes=[
            pltpu.VMEM((dk, dk), F32),
            pltpu.VMEM((3, 8, dk), F32),
            pltpu.VMEM((3, 8 + blk, dk), F32),
        ],
        compiler_params=pltpu.CompilerParams(
            dimension_semantics=("arbitrary", "arbitrary", "arbitrary")),
        name="gdn",
    )(a_log, dt_bias, proj3, proj3, proj3, proj3, gate3, conv_w, conv_w, conv_w, dn_norm)


def _t5_bucket_starts():
    max_exact = NUM_BUCKETS // 2
    n = np.arange(0, MAX_DISTANCE + 1)
    nf = np.maximum(n, 1).astype(np.float32)
    large = max_exact + (np.log(nf / max_exact) / math.log(MAX_DISTANCE / max_exact)
                         * (NUM_BUCKETS - max_exact)).astype(np.int32)
    bucket = np.where(n < max_exact, n, np.minimum(large, NUM_BUCKETS - 1))
    assert bucket[MAX_DISTANCE] == NUM_BUCKETS - 1 and np.all(np.diff(bucket) >= 0)
    starts = [(0, int(bucket[0]))]
    for d in range(1, MAX_DISTANCE + 1):
        if bucket[d] != bucket[d - 1]:
            starts.append((d, int(bucket[d])))
    return starts


def _diff_attn_kernel(qi_tab, kj_tab, rb_ref, q_ref, k_ref, v_ref,
                      lq1_ref, lk1_ref, lq2_ref, lk2_ref, dfn_ref, y_ref,
                      bias_ref, qs_ref, m_ref, l_ref, acc_ref, *, tq, n_heads, lam_init):
    h = pl.program_id(1)
    p = pl.program_id(2)
    qi = qi_tab[p]
    kj = kj_tab[p]
    d = DF_HEAD_DIM

    @pl.when(p == 0)
    def _():
        rows = lax.broadcasted_iota(jnp.int32, (tq, tq), 0)
        cols = lax.broadcasted_iota(jnp.int32, (tq, tq), 1)
        for idx in range(2):
            dist = rows - cols + idx * tq
            starts = _t5_bucket_starts()
            tile = jnp.full((tq, tq), rb_ref[starts[0][1] * n_heads + h], F32)
            for first, bucket in starts[1:]:
                tile = jnp.where(dist >= first, rb_ref[bucket * n_heads + h], tile)
            if idx == 0:
                tile = jnp.where(dist >= 0, tile, -jnp.inf)
            bias_ref[idx] = tile
        bias_ref[2] = jnp.full((tq, tq), rb_ref[(NUM_BUCKETS - 1) * n_heads + h], F32)

    @pl.when(kj == 0)
    def _():
        qs_ref[...] = (q_ref[...].astype(F32) * (d ** -0.5)).astype(BF16)
        m_ref[...] = jnp.full(m_ref.shape, -jnp.inf, F32)
        l_ref[...] = jnp.zeros_like(l_ref)
        acc_ref[...] = jnp.zeros_like(acc_ref)

    bias = bias_ref[jnp.minimum(qi - kj, 2)]
    v = v_ref[...]
    for m in range(2):
        s = _dot_nt(qs_ref[:, m * d:(m + 1) * d], k_ref[:, m * d:(m + 1) * d]) + bias
        m_prev = m_ref[m]
        m_new = jnp.maximum(m_prev, jnp.max(s, axis=-1, keepdims=True))
        alpha = jnp.exp(m_prev - m_new)
        pr = jnp.exp(s - m_new)
        l_ref[m] = alpha * l_ref[m] + jnp.sum(pr, axis=-1, keepdims=True)
        acc_ref[m] = alpha * acc_ref[m] + _dot(pr.astype(BF16), v)
        m_ref[m] = m_new

    @pl.when(kj == qi)
    def _():
        lam = (jnp.exp(jnp.sum(lq1_ref[...] * lk1_ref[...], axis=-1, keepdims=True))
               - jnp.exp(jnp.sum(lq2_ref[...] * lk2_ref[...], axis=-1, keepdims=True))
               + lam_init)
        o = acc_ref[0] * (1.0 / l_ref[0]) - acc_ref[1] * (lam / l_ref[1])
        o = o * lax.rsqrt(jnp.mean(o * o, axis=-1, keepdims=True) + 1e-5) * dfn_ref[...]
        y_ref[...] = (o * (1.0 - lam_init)).astype(y_ref.dtype)


def _diff_attn(proj3, rel_bias, lq1, lk1, lq2, lk2, df_norm, *, n_heads, col0, tq, lam_init):
    B, S, _ = proj3.shape
    d2 = 2 * DF_HEAD_DIM
    nq = S // tq
    pairs = [(i, j) for i in range(nq) for j in range(i + 1)]
    qi_tab = jnp.asarray([pq[0] for pq in pairs], jnp.int32)
    kj_tab = jnp.asarray([pq[1] for pq in pairs], jnp.int32)
    cb = col0 // d2
    vec = lambda n: pl.BlockSpec((1, n), lambda b, h, p, qt, kt: (0, 0))
    grid_spec = pltpu.PrefetchScalarGridSpec(
        num_scalar_prefetch=2,
        grid=(B, n_heads, len(pairs)),
        in_specs=[
            pl.BlockSpec(memory_space=pltpu.SMEM),
            pl.BlockSpec((None, tq, d2), lambda b, h, p, qt, kt: (b, qt[p], cb + h)),
            pl.BlockSpec((None, tq, d2), lambda b, h, p, qt, kt: (b, kt[p], cb + n_heads + h)),
            pl.BlockSpec((None, tq, d2), lambda b, h, p, qt, kt: (b, kt[p], cb + 2 * n_heads + h)),
            vec(DF_HEAD_DIM), vec(DF_HEAD_DIM), vec(DF_HEAD_DIM), vec(DF_HEAD_DIM), vec(d2),
        ],
        out_specs=pl.BlockSpec((None, tq, d2), lambda b, h, p, qt, kt: (b, qt[p], h)),
        scratch_shapes=[
            pltpu.VMEM((3, tq, tq), F32),
            pltpu.VMEM((tq, d2), BF16),
            pltpu.VMEM((2, tq, 1), F32),
            pltpu.VMEM((2, tq, 1), F32),
            pltpu.VMEM((2, tq, d2), F32),
        ],
    )
    return pl.pallas_call(
        functools.partial(_diff_attn_kernel, tq=tq, n_heads=n_heads, lam_init=lam_init),
        grid_spec=grid_spec,
        out_shape=jax.ShapeDtypeStruct((B, S, n_heads * d2), BF16),
        compiler_params=pltpu.CompilerParams(
            dimension_semantics=("arbitrary", "arbitrary", "arbitrary")),
        name="diff_attn",
    )(qi_tab, kj_tab, rel_bias, proj3, proj3, proj3, lq1, lk1, lq2, lk2, df_norm)


def _out_proj_kernel(x_ref, ya_ref, yb_ref, wa_ref, wb_ref, h_ref):
    h_ref[...] = x_ref[...] + _dot(ya_ref[...], wa_ref[...]) + _dot(yb_ref[...], wb_ref[...])


def _out_proj(x2, y_dn, y_df, w_o, *, tm, tn):
    T, D = x2.shape
    ka = y_dn.shape[1]
    kb = y_df.shape[1]
    assert ka == kb
    return pl.pallas_call(
        _out_proj_kernel,
        grid=(T // tm, D // tn),
        in_specs=[
            pl.BlockSpec((tm, tn), lambda i, j: (i, j)),
            pl.BlockSpec((tm, ka), lambda i, j: (i, 0)),
            pl.BlockSpec((tm, kb), lambda i, j: (i, 0)),
            pl.BlockSpec((ka, tn), lambda i, j: (0, j)),
            pl.BlockSpec((kb, tn), lambda i, j: (1, j)),
        ],
        out_specs=pl.BlockSpec((tm, tn), lambda i, j: (i, j)),
        out_shape=jax.ShapeDtypeStruct((T, D), F32),
        compiler_params=pltpu.CompilerParams(
            dimension_semantics=("arbitrary", "arbitrary"),
            vmem_limit_bytes=V7X_VMEM_LIMIT),
        name="out_proj",
    )(x2, y_dn, y_df, w_o, w_o)


def _mlp_kernel(h_ref, g_ref, wu_ref, wd_ref, gf_ref, o_ref, u_ref, acc_ref):
    f = pl.program_id(1)

    @pl.when(f == 0)
    def _():
        x = h_ref[...]
        ms = jnp.mean(x * x, axis=-1, keepdims=True)
        u_ref[...] = (x * lax.rsqrt(ms + 1e-6) * g_ref[...]).astype(BF16)
        acc_ref[...] = jnp.zeros_like(acc_ref)

    hid = jnp.maximum(_dot(u_ref[...], wu_ref[...]), 0.0)
    acc_ref[...] += _dot((hid * hid).astype(BF16), wd_ref[...])

    @pl.when(f == pl.num_programs(1) - 1)
    def _():
        y = h_ref[...] + acc_ref[...]
        ms = jnp.mean(y * y, axis=-1, keepdims=True)
        o_ref[...] = y * lax.rsqrt(ms + 1e-6) * gf_ref[...]


def _mlp(h1, gain, w_up, w_down, final_gain, *, tm, tf):
    T, D = h1.shape
    Fdim = w_up.shape[1]
    return pl.pallas_call(
        _mlp_kernel,
        grid=(T // tm, Fdim // tf),
        in_specs=[
            pl.BlockSpec((tm, D), lambda i, f: (i, 0)),
            pl.BlockSpec((1, D), lambda i, f: (0, 0)),
            pl.BlockSpec((D, tf), lambda i, f: (0, f)),
            pl.BlockSpec((tf, D), lambda i, f: (f, 0)),
            pl.BlockSpec((1, D), lambda i, f: (0, 0)),
        ],
        out_specs=pl.BlockSpec((tm, D), lambda i, f: (i, 0)),
        out_shape=jax.ShapeDtypeStruct((T, D), F32),
        scratch_shapes=[pltpu.VMEM((tm, D), BF16), pltpu.VMEM((tm, D), F32)],
        compiler_params=pltpu.CompilerParams(
            dimension_semantics=("arbitrary", "arbitrary"),
            vmem_limit_bytes=V7X_VMEM_LIMIT),
        name="mlp",
    )(h1, gain, w_up, w_down, final_gain)


def _tile(n, pref):
    if n <= pref:
        return n
    t = pref - pref % 128
    while t > 128 and n % t:
        t -= 128
    assert n % t == 0
    return t


def kernel(x, attn_norm, w_in, conv_w, a_log, dt_bias, dn_norm, lambda_q1, lambda_k1,
           lambda_q2, lambda_k2, df_norm, rel_bias, w_o, mlp_norm, w_up, w_down, final_norm):
    B, S, D = x.shape
    depth = attn_norm.shape[0]
    n_dn = a_log.shape[1]
    n_df = rel_bias.shape[1]
    dn_dim = n_dn * DN_HEAD_DIM
    df_dim = n_df * 2 * DF_HEAD_DIM
    T = B * S
    gate0 = 4 * dn_dim
    assert w_in.shape[2] == gate0 + 2 * n_dn + 3 * df_dim and 2 * n_dn <= GATE_COLS

    assert depth == 1
    l = 0
    h = x.reshape(T, D)

    wl = w_in[l]
    w_main = jnp.concatenate([wl[:, :gate0], wl[:, gate0 + 2 * n_dn:]], axis=1).astype(BF16)
    w_gate = jnp.pad(wl[:, gate0:gate0 + 2 * n_dn],
                     ((0, 0), (0, GATE_COLS - 2 * n_dn))).astype(BF16)
    w_o_bf = w_o[l].astype(BF16)
    w_up_bf = w_up[l].astype(BF16)
    w_down_bf = w_down[l].astype(BF16)

    proj, gates = _in_proj(h, attn_norm[l][None, :], w_main, w_gate,
                           tm=_tile(T, 1024), tn=_tile(w_main.shape[1], 1024))
    proj3 = proj.reshape(B, S, -1)
    gate3 = gates.reshape(B, S, GATE_COLS)

    y_dn = _gdn(proj3, gate3, conv_w[l], a_log[l], dt_bias[l], dn_norm[l][None, :],
                n_heads=n_dn, blk=_tile(S, 256))
    lam_init = 0.8 - 0.6 * math.exp(-0.3 * l)
    y_df = _diff_attn(proj3, rel_bias.reshape(-1),
                      lambda_q1[l][None, :], lambda_k1[l][None, :],
                      lambda_q2[l][None, :], lambda_k2[l][None, :], df_norm[l][None, :],
                      n_heads=n_df, col0=gate0, tq=_tile(S, 512), lam_init=lam_init)

    h1 = _out_proj(h, y_dn.reshape(T, dn_dim), y_df.reshape(T, df_dim), w_o_bf,
                   tm=_tile(T, 1024), tn=_tile(D, 1024))
    out = _mlp(h1, mlp_norm[l][None, :], w_up_bf, w_down_bf, final_norm[None, :],
               tm=_tile(T, 512), tf=_tile(w_up_bf.shape[1], 1024))
    return out.reshape(B, S, D)
```

```python
import functools
import math

import numpy as np
import jax
import jax.numpy as jnp
from jax import lax
from jax.experimental import pallas as pl
from jax.experimental.pallas import tpu as pltpu

F32 = jnp.float32
BF16 = jnp.bfloat16

DN_HEAD_DIM = 128
DF_HEAD_DIM = 128
CONV_WIDTH = 4
CHUNK = 64
NUM_BUCKETS = 32
MAX_DISTANCE = 128
GATE_COLS = 128

V7X_VMEM_LIMIT = 56 * 1024 * 1024


def _dot(a, b):
    return jnp.dot(a, b, preferred_element_type=F32)


def _dot_nt(a, b):
    return lax.dot_general(a, b, (((1,), (1,)), ((), ())), preferred_element_type=F32)


def _sigmoid(x):
    return 1.0 / (1.0 + jnp.exp(-x))


def _in_proj_kernel(x_ref, g_ref, w_ref, wg_ref, proj_ref, gate_ref, u_ref):
    @pl.when(pl.program_id(1) == 0)
    def _():
        x = x_ref[...]
        ms = jnp.mean(x * x, axis=-1, keepdims=True)
        u = (x * lax.rsqrt(ms + 1e-6) * g_ref[...]).astype(BF16)
        u_ref[...] = u
        gate_ref[...] = _dot(u, wg_ref[...])

    proj_ref[...] = _dot(u_ref[...], w_ref[...]).astype(BF16)


def _in_proj(x2, gain, w_main, w_gate, *, tm, tn):
    T, D = x2.shape
    N = w_main.shape[1]
    return pl.pallas_call(
        _in_proj_kernel,
        grid=(T // tm, N // tn),
        in_specs=[
            pl.BlockSpec((tm, D), lambda i, j: (i, 0)),
            pl.BlockSpec((1, D), lambda i, j: (0, 0)),
            pl.BlockSpec((D, tn), lambda i, j: (0, j)),
            pl.BlockSpec((D, GATE_COLS), lambda i, j: (0, 0)),
        ],
        out_specs=[
            pl.BlockSpec((tm, tn), lambda i, j: (i, j)),
            pl.BlockSpec((tm, GATE_COLS), lambda i, j: (i, 0)),
        ],
        out_shape=[
            jax.ShapeDtypeStruct((T, N), BF16),
            jax.ShapeDtypeStruct((T, GATE_COLS), F32),
        ],
        scratch_shapes=[pltpu.VMEM((tm, D), BF16)],
        compiler_params=pltpu.CompilerParams(
            dimension_semantics=("arbitrary", "arbitrary"),
            vmem_limit_bytes=V7X_VMEM_LIMIT),
        name="in_proj",
    )(x2, gain, w_main, w_gate)


def _gdn_kernel(q_ref, k_ref, v_ref, z_ref, gate_ref, cw_ref, alog_ref, dtb_ref, dnn_ref,
                y_ref, state_ref, tail_ref, xbuf_ref, *, blk, n_heads):
    dk = DN_HEAD_DIM
    dn = n_heads * dk
    n_chunks = blk // CHUNK

    @pl.when(pl.program_id(1) == 0)
    def _():
        state_ref[...] = jnp.zeros_like(state_ref)
        tail_ref[...] = jnp.zeros_like(tail_ref)

    for i, x_ref in enumerate((q_ref, k_ref, v_ref)):
        xbuf_ref[i, 0:8, :] = tail_ref[i]
        xbuf_ref[i, 8:8 + blk, :] = x_ref[...].astype(F32)
        tail_ref[i] = xbuf_ref[i, blk:blk + 8, :]

    def conv_silu(i, h):
        lo = i * dn + h * dk
        cw = cw_ref[:, lo:lo + dk]
        y = None
        for s in range(CONV_WIDTH):
            tap = CONV_WIDTH - 1 - s
            term = xbuf_ref[i, 8 - s:8 - s + blk, h * dk:(h + 1) * dk] * cw[tap:tap + 1, :]
            y = term if y is None else y + term
        return y * _sigmoid(y)

    gate = gate_ref[...]
    beta_all = _sigmoid(gate)
    xs = gate + dtb_ref[...]
    softplus = jnp.maximum(xs, 0.0) + jnp.log(1.0 + jnp.exp(-jnp.abs(xs)))
    g_all = -jnp.exp(alog_ref[...]) * softplus
    pos = lax.broadcasted_iota(jnp.int32, g_all.shape, 0) & (CHUNK - 1)
    gc_all = g_all
    step = 1
    while step < CHUNK:
        gc_all = gc_all + jnp.where(pos >= step, pltpu.roll(gc_all, step, 0), 0.0)
        step *= 2

    ri = lax.broadcasted_iota(jnp.int32, (CHUNK, CHUNK), 0)
    ci = lax.broadcasted_iota(jnp.int32, (CHUNK, CHUNK), 1)
    tril = ri >= ci
    strict = ri > ci
    eye = ri == ci
    gain = dnn_ref[...]

    heads = []
    for h in range(n_heads):
        q = conv_silu(0, h)
        k = conv_silu(1, h)
        v = conv_silu(2, h)
        q = q * lax.rsqrt(jnp.sum(q * q, axis=-1, keepdims=True) + 1e-6) * (dk ** -0.5)
        k = k * lax.rsqrt(jnp.sum(k * k, axis=-1, keepdims=True) + 1e-6)
        beta = jnp.broadcast_to(beta_all[:, h:h + 1], (blk, dk))
        gc = jnp.broadcast_to(gc_all[:, n_heads + h:n_heads + h + 1], (blk, dk))
        heads.append((q, k, v, beta, gc))

    items = [(h, c) for c in range(n_chunks) for h in range(n_heads)]
    st = []
    for h, c in items:
        q, k, v, beta, gc = heads[h]
        rows = slice(c * CHUNK, (c + 1) * CHUNK)
        qc, kc, vc, bc, gcc = q[rows], k[rows], v[rows], beta[rows], gc[rows]
        g_last = gcc[CHUNK - 1:CHUNK, :]
        eg = jnp.exp(gcc)
        g_sq = gcc[:, 0:CHUNK]
        g_row = jnp.sum(jnp.where(eye, g_sq, 0.0), axis=0, keepdims=True)
        decay = jnp.exp(jnp.where(tril, g_sq - g_row, -jnp.inf))
        kb = kc * bc
        lhs = jnp.concatenate([kb, qc], axis=0).astype(BF16)
        aq = _dot_nt(lhs, kc.astype(BF16))
        n_mat = jnp.where(strict, -(aq[:CHUNK] * decay), 0.0)
        st.append(dict(
            qe=qc * eg, g_last=g_last,
            rhs=jnp.concatenate([kb * eg, vc * bc], axis=1).astype(BF16),
            qk=jnp.where(tril, aq[CHUNK:] * decay, 0.0).astype(BF16),
            kd_t=(kc * jnp.exp(g_last - gcc)).T.astype(BF16),
            power=n_mat, inv=jnp.where(eye, 1.0, 0.0) + n_mat))

    span = 2
    while span < CHUNK:
        for s in st:
            pb = s["power"].astype(BF16)
            s["power"] = _dot(pb, pb)
        for s in st:
            s["inv"] = s["inv"] + _dot(s["inv"].astype(BF16), s["power"].astype(BF16))
        span *= 2

    for s in st:
        s["wu"] = _dot(s["inv"].astype(BF16), s["rhs"]).astype(BF16)
    for s in st:
        s["gr"] = _dot(s["kd_t"], s["wu"])
        qw = _dot(s["qk"], s["wu"])
        s["q_eff"] = (s["qe"] - qw[:, :dk]).astype(BF16)
        s["p_loc"] = qw[:, dk:]

    for (h, c), s in zip(items, st):
        state = state_ref[h]
        s_bf = state.astype(BF16)
        o = _dot(s["q_eff"], s_bf) + s["p_loc"]
        state_ref[h] = (state * jnp.exp(s["g_last"]) - _dot(s["gr"][:, :dk].astype(BF16), s_bf)
                        + s["gr"][:, dk:])
        rows = slice(c * CHUNK, (c + 1) * CHUNK)
        zc = z_ref[rows, h * dk:(h + 1) * dk].astype(F32)
        o = o * lax.rsqrt(jnp.mean(o * o, axis=-1, keepdims=True) + 1e-6) * gain
        y_ref[rows, h * dk:(h + 1) * dk] = (o * (zc * _sigmoid(zc))).astype(y_ref.dtype)


def _gdn(proj3, gate3, conv_w, alog_row, dtb_row, dn_norm, *, n_heads, blk):
    B, S, _ = proj3.shape
    dk = DN_HEAD_DIM
    dn = n_heads * dk

    def group(idx):
        return pl.BlockSpec((None, blk, dn), lambda b, t: (b, t, idx))

    def whole(arr):
        return pl.BlockSpec(arr.shape, lambda b, t: (0,) * arr.ndim)

    return pl.pallas_call(
        functools.partial(_gdn_kernel, blk=blk, n_heads=n_heads),
        grid=(B, S // blk),
        in_specs=[
            group(0), group(1), group(2), group(3),
            pl.BlockSpec((None, blk, GATE_COLS), lambda b, t: (b, t, 0)),
            whole(conv_w), whole(alog_row), whole(dtb_row), whole(dn_norm),
        ],
        out_specs=pl.BlockSpec((None, blk, dn), lambda b, t: (b, t, 0)),
        out_shape=jax.ShapeDtypeStruct((B, S, dn), BF16),
        scratch_shapes=[
            pltpu.VMEM((n_heads, dk, dk), F32),
            pltpu.VMEM((3, 8, dn), F32),
            pltpu.VMEM((3, 8 + blk, dn), F32),
        ],
        compiler_params=pltpu.CompilerParams(
            dimension_semantics=("arbitrary", "arbitrary"),
            vmem_limit_bytes=V7X_VMEM_LIMIT),
        name="gdn",
    )(proj3, proj3, proj3, proj3, gate3, conv_w, alog_row, dtb_row, dn_norm)


def _t5_bucket_starts():
    max_exact = NUM_BUCKETS // 2
    n = np.arange(0, MAX_DISTANCE + 1)
    nf = np.maximum(n, 1).astype(np.float32)
    large = max_exact + (np.log(nf / max_exact) / math.log(MAX_DISTANCE / max_exact)
                         * (NUM_BUCKETS - max_exact)).astype(np.int32)
    bucket = np.where(n < max_exact, n, np.minimum(large, NUM_BUCKETS - 1))
    assert bucket[MAX_DISTANCE] == NUM_BUCKETS - 1 and np.all(np.diff(bucket) >= 0)
    starts = [(0, int(bucket[0]))]
    for d in range(1, MAX_DISTANCE + 1):
        if bucket[d] != bucket[d - 1]:
            starts.append((d, int(bucket[d])))
    return starts


def _diff_attn_kernel(qi_tab, kj_tab, rb_ref, q_ref, k_ref, v_ref,
                      lq1_ref, lk1_ref, lq2_ref, lk2_ref, dfn_ref, y_ref,
                      bias_ref, qs_ref, m_ref, l_ref, acc_ref, *, tq, n_heads, lam_init):
    h = pl.program_id(1)
    p = pl.program_id(2)
    qi = qi_tab[p]
    kj = kj_tab[p]
    d = DF_HEAD_DIM

    @pl.when(p == 0)
    def _():
        rows = lax.broadcasted_iota(jnp.int32, (tq, tq), 0)
        cols = lax.broadcasted_iota(jnp.int32, (tq, tq), 1)
        for idx in range(2):
            dist = rows - cols + idx * tq
            starts = _t5_bucket_starts()
            tile = jnp.full((tq, tq), rb_ref[starts[0][1] * n_heads + h], F32)
            for first, bucket in starts[1:]:
                tile = jnp.where(dist >= first, rb_ref[bucket * n_heads + h], tile)
            if idx == 0:
                tile = jnp.where(dist >= 0, tile, -jnp.inf)
            bias_ref[idx] = tile
        bias_ref[2] = jnp.full((tq, tq), rb_ref[(NUM_BUCKETS - 1) * n_heads + h], F32)

    @pl.when(kj == 0)
    def _():
        qs_ref[...] = (q_ref[...].astype(F32) * (d ** -0.5)).astype(BF16)
        m_ref[...] = jnp.full(m_ref.shape, -jnp.inf, F32)
        l_ref[...] = jnp.zeros_like(l_ref)
        acc_ref[...] = jnp.zeros_like(acc_ref)

    bias = bias_ref[jnp.minimum(qi - kj, 2)]
    v = v_ref[...]
    for m in range(2):
        s = _dot_nt(qs_ref[:, m * d:(m + 1) * d], k_ref[:, m * d:(m + 1) * d]) + bias
        m_prev = m_ref[m]
        m_new = jnp.maximum(m_prev, jnp.max(s, axis=-1, keepdims=True))
        alpha = jnp.exp(m_prev - m_new)
        pr = jnp.exp(s - m_new)
        l_ref[m] = alpha * l_ref[m] + jnp.sum(pr, axis=-1, keepdims=True)
        acc_ref[m] = alpha * acc_ref[m] + _dot(pr.astype(BF16), v)
        m_ref[m] = m_new

    @pl.when(kj == qi)
    def _():
        lam = (jnp.exp(jnp.sum(lq1_ref[...] * lk1_ref[...], axis=-1, keepdims=True))
               - jnp.exp(jnp.sum(lq2_ref[...] * lk2_ref[...], axis=-1, keepdims=True))
               + lam_init)
        o = acc_ref[0] * (1.0 / l_ref[0]) - acc_ref[1] * (lam / l_ref[1])
        o = o * lax.rsqrt(jnp.mean(o * o, axis=-1, keepdims=True) + 1e-5) * dfn_ref[...]
        y_ref[...] = (o * (1.0 - lam_init)).astype(y_ref.dtype)


def _diff_attn(proj3, rel_bias, lq1, lk1, lq2, lk2, df_norm, *, n_heads, col0, tq, lam_init):
    B, S, _ = proj3.shape
    d2 = 2 * DF_HEAD_DIM
    nq = S // tq
    pairs = [(i, j) for i in range(nq) for j in range(i + 1)]
    qi_tab = jnp.asarray([pq[0] for pq in pairs], jnp.int32)
    kj_tab = jnp.asarray([pq[1] for pq in pairs], jnp.int32)
    cb = col0 // d2
    vec = lambda n: pl.BlockSpec((1, n), lambda b, h, p, qt, kt: (0, 0))
    grid_spec = pltpu.PrefetchScalarGridSpec(
        num_scalar_prefetch=2,
        grid=(B, n_heads, len(pairs)),
        in_specs=[
            pl.BlockSpec(memory_space=pltpu.SMEM),
            pl.BlockSpec((None, tq, d2), lambda b, h, p, qt, kt: (b, qt[p], cb + h)),
            pl.BlockSpec((None, tq, d2), lambda b, h, p, qt, kt: (b, kt[p], cb + n_heads + h)),
            pl.BlockSpec((None, tq, d2), lambda b, h, p, qt, kt: (b, kt[p], cb + 2 * n_heads + h)),
            vec(DF_HEAD_DIM), vec(DF_HEAD_DIM), vec(DF_HEAD_DIM), vec(DF_HEAD_DIM), vec(d2),
        ],
        out_specs=pl.BlockSpec((None, tq, d2), lambda b, h, p, qt, kt: (b, qt[p], h)),
        scratch_shapes=[
            pltpu.VMEM((3, tq, tq), F32),
            pltpu.VMEM((tq, d2), BF16),
            pltpu.VMEM((2, tq, 1), F32),
            pltpu.VMEM((2, tq, 1), F32),
            pltpu.VMEM((2, tq, d2), F32),
        ],
    )
    return pl.pallas_call(
        functools.partial(_diff_attn_kernel, tq=tq, n_heads=n_heads, lam_init=lam_init),
        grid_spec=grid_spec,
        out_shape=jax.ShapeDtypeStruct((B, S, n_heads * d2), BF16),
        compiler_params=pltpu.CompilerParams(
            dimension_semantics=("arbitrary", "arbitrary", "arbitrary")),
        name="diff_attn",
    )(qi_tab, kj_tab, rel_bias, proj3, proj3, proj3, lq1, lk1, lq2, lk2, df_norm)


def _out_proj_kernel(x_ref, ya_ref, yb_ref, wa_ref, wb_ref, h_ref):
    h_ref[...] = x_ref[...] + _dot(ya_ref[...], wa_ref[...]) + _dot(yb_ref[...], wb_ref[...])


def _out_proj(x2, y_dn, y_df, w_o, *, tm, tn):
    T, D = x2.shape
    ka = y_dn.shape[1]
    kb = y_df.shape[1]
    assert ka == kb
    return pl.pallas_call(
        _out_proj_kernel,
        grid=(T // tm, D // tn),
        in_specs=[
            pl.BlockSpec((tm, tn), lambda i, j: (i, j)),
            pl.BlockSpec((tm, ka), lambda i, j: (i, 0)),
            pl.BlockSpec((tm, kb), lambda i, j: (i, 0)),
            pl.BlockSpec((ka, tn), lambda i, j: (0, j)),
            pl.BlockSpec((kb, tn), lambda i, j: (1, j)),
        ],
        out_specs=pl.BlockSpec((tm, tn), lambda i, j: (i, j)),
        out_shape=jax.ShapeDtypeStruct((T, D), F32),
        compiler_params=pltpu.CompilerParams(
            dimension_semantics=("arbitrary", "arbitrary"),
            vmem_limit_bytes=V7X_VMEM_LIMIT),
        name="out_proj",
    )(x2, y_dn, y_df, w_o, w_o)


def _mlp_kernel(h_ref, g_ref, wu_ref, wd_ref, gf_ref, o_ref, u_ref, acc_ref):
    f = pl.program_id(1)

    @pl.when(f == 0)
    def _():
        x = h_ref[...]
        ms = jnp.mean(x * x, axis=-1, keepdims=True)
        u_ref[...] = (x * lax.rsqrt(ms + 1e-6) * g_ref[...]).astype(BF16)
        acc_ref[...] = jnp.zeros_like(acc_ref)

    hid = jnp.maximum(_dot(u_ref[...], wu_ref[...]), 0.0)
    acc_ref[...] += _dot((hid * hid).astype(BF16), wd_ref[...])

    @pl.when(f == pl.num_programs(1) - 1)
    def _():
        y = h_ref[...] + acc_ref[...]
        ms = jnp.mean(y * y, axis=-1, keepdims=True)
        o_ref[...] = y * lax.rsqrt(ms + 1e-6) * gf_ref[...]


def _mlp(h1, gain, w_up, w_down, final_gain, *, tm, tf):
    T, D = h1.shape
    Fdim = w_up.shape[1]
    return pl.pallas_call(
        _mlp_kernel,
        grid=(T // tm, Fdim // tf),
        in_specs=[
            pl.BlockSpec((tm, D), lambda i, f: (i, 0)),
            pl.BlockSpec((1, D), lambda i, f: (0, 0)),
            pl.BlockSpec((D, tf), lambda i, f: (0, f)),
            pl.BlockSpec((tf, D), lambda i, f: (f, 0)),
            pl.BlockSpec((1, D), lambda i, f: (0, 0)),
        ],
        out_specs=pl.BlockSpec((tm, D), lambda i, f: (i, 0)),
        out_shape=jax.ShapeDtypeStruct((T, D), F32),
        scratch_shapes=[pltpu.VMEM((tm, D), BF16), pltpu.VMEM((tm, D), F32)],
        compiler_params=pltpu.CompilerParams(
            dimension_semantics=("arbitrary", "arbitrary"),
            vmem_limit_bytes=V7X_VMEM_LIMIT),
        name="mlp",
    )(h1, gain, w_up, w_down, final_gain)


def _tile(n, pref):
    if n <= pref:
        return n
    t = pref - pref % 128
    while t > 128 and n % t:
        t -= 128
    assert n % t == 0
    return t


def kernel(x, attn_norm, w_in, conv_w, a_log, dt_bias, dn_norm, lambda_q1, lambda_k1,
           lambda_q2, lambda_k2, df_norm, rel_bias, w_o, mlp_norm, w_up, w_down, final_norm):
    B, S, D = x.shape
    depth = attn_norm.shape[0]
    n_dn = a_log.shape[1]
    n_df = rel_bias.shape[1]
    dn_dim = n_dn * DN_HEAD_DIM
    df_dim = n_df * 2 * DF_HEAD_DIM
    T = B * S
    gate0 = 4 * dn_dim
    assert w_in.shape[2] == gate0 + 2 * n_dn + 3 * df_dim and 2 * n_dn <= GATE_COLS

    assert depth == 1
    l = 0
    h = x.reshape(T, D)

    wl = w_in[l]
    w_main = jnp.concatenate([wl[:, :gate0], wl[:, gate0 + 2 * n_dn:]], axis=1).astype(BF16)
    w_gate = jnp.pad(wl[:, gate0:gate0 + 2 * n_dn],
                     ((0, 0), (0, GATE_COLS - 2 * n_dn))).astype(BF16)
    w_o_bf = w_o[l].astype(BF16)
    w_up_bf = w_up[l].astype(BF16)
    w_down_bf = w_down[l].astype(BF16)

    proj, gates = _in_proj(h, attn_norm[l][None, :], w_main, w_gate,
                           tm=_tile(T, 1024), tn=_tile(w_main.shape[1], 1024))
    proj3 = proj.reshape(B, S, -1)
    gate3 = gates.reshape(B, S, GATE_COLS)

    gate_pad = ((0, 0), (n_dn, GATE_COLS - 2 * n_dn))
    alog_row = jnp.pad(a_log[l][None, :], gate_pad)
    dtb_row = jnp.pad(dt_bias[l][None, :], gate_pad)
    y_dn = _gdn(proj3, gate3, conv_w[l], alog_row, dtb_row, dn_norm[l][None, :],
                n_heads=n_dn, blk=_tile(S, 128))
    lam_init = 0.8 - 0.6 * math.exp(-0.3 * l)
    y_df = _diff_attn(proj3, rel_bias.reshape(-1),
                      lambda_q1[l][None, :], lambda_k1[l][None, :],
                      lambda_q2[l][None, :], lambda_k2[l][None, :], df_norm[l][None, :],
                      n_heads=n_df, col0=gate0, tq=_tile(S, 512), lam_init=lam_init)

    h1 = _out_proj(h, y_dn.reshape(T, dn_dim), y_df.reshape(T, df_dim), w_o_bf,
                   tm=_tile(T, 1024), tn=_tile(D, 1024))
    out = _mlp(h1, mlp_norm[l][None, :], w_up_bf, w_down_bf, final_norm[None, :],
               tm=_tile(T, 512), tf=_tile(w_up_bf.shape[1], 1024))
    return out.reshape(B, S, D)
```

```python
import functools
import math

import numpy as np
import jax
import jax.numpy as jnp
from jax import lax
from jax.experimental import pallas as pl
from jax.experimental.pallas import tpu as pltpu

F32 = jnp.float32
BF16 = jnp.bfloat16

DN_HEAD_DIM = 128
DF_HEAD_DIM = 128
CONV_WIDTH = 4
CHUNK = 64
NUM_BUCKETS = 32
MAX_DISTANCE = 128
GATE_COLS = 128
ATTN_SUB = 256
ATTN_LOGITS_AHEAD = 3

V7X_VMEM_LIMIT = 56 * 1024 * 1024


def _dot(a, b):
    return jnp.dot(a, b, preferred_element_type=F32)


def _dot_nt(a, b):
    return lax.dot_general(a, b, (((1,), (1,)), ((), ())), preferred_element_type=F32)


def _sigmoid(x):
    return 1.0 / (1.0 + jnp.exp(-x))


def _in_proj_kernel(x_ref, g_ref, w_ref, wg_ref, wvt_ref, proj_ref, gate_ref, vt_ref, u_ref,
                    *, n_main):
    j = pl.program_id(1)

    @pl.when(j == 0)
    def _():
        x = x_ref[...]
        ms = jnp.mean(x * x, axis=-1, keepdims=True)
        u = (x * lax.rsqrt(ms + 1e-6) * g_ref[...]).astype(BF16)
        u_ref[...] = u
        gate_ref[...] = _dot(u, wg_ref[...])

    @pl.when(j < n_main)
    def _():
        proj_ref[...] = _dot(u_ref[...], w_ref[...]).astype(BF16)

    @pl.when(j >= n_main)
    def _():
        vt_ref[...] = _dot_nt(wvt_ref[...], u_ref[...]).astype(BF16)


def _in_proj(x2, gain, w_main, w_gate, w_vt, *, batch, tm, tn):
    T, D = x2.shape
    N = w_main.shape[1]
    nv = w_vt.shape[0]
    S = T // batch
    assert S % tm == 0 and N % tn == 0 and nv % tn == 0
    n_main = N // tn
    per_seq = S // tm
    return pl.pallas_call(
        functools.partial(_in_proj_kernel, n_main=n_main),
        grid=(T // tm, n_main + nv // tn),
        in_specs=[
            pl.BlockSpec((tm, D), lambda i, j: (i, 0)),
            pl.BlockSpec((1, D), lambda i, j: (0, 0)),
            pl.BlockSpec((D, tn), lambda i, j: (0, jnp.minimum(j, n_main - 1))),
            pl.BlockSpec((D, GATE_COLS), lambda i, j: (0, 0)),
            pl.BlockSpec((tn, D), lambda i, j: (jnp.maximum(j - n_main, 0), 0)),
        ],
        out_specs=[
            pl.BlockSpec((tm, tn), lambda i, j: (i, jnp.minimum(j, n_main - 1))),
            pl.BlockSpec((tm, GATE_COLS), lambda i, j: (i, 0)),
            pl.BlockSpec((None, tn, tm),
                         lambda i, j: (i // per_seq, jnp.maximum(j - n_main, 0), i % per_seq)),
        ],
        out_shape=[
            jax.ShapeDtypeStruct((T, N), BF16),
            jax.ShapeDtypeStruct((T, GATE_COLS), F32),
            jax.ShapeDtypeStruct((batch, nv, S), BF16),
        ],
        scratch_shapes=[pltpu.VMEM((tm, D), BF16)],
        compiler_params=pltpu.CompilerParams(
            dimension_semantics=("arbitrary", "arbitrary"),
            vmem_limit_bytes=V7X_VMEM_LIMIT),
        name="in_proj",
    )(x2, gain, w_main, w_gate, w_vt)


def _gdn_kernel(q_ref, k_ref, v_ref, z_ref, gate_ref, cw_ref, alog_ref, dtb_ref, dnn_ref,
                y_ref, state_ref, tail_ref, xbuf_ref, *, blk, n_heads):
    dk = DN_HEAD_DIM
    dn = n_heads * dk
    n_chunks = blk // CHUNK

    @pl.when(pl.program_id(1) == 0)
    def _():
        state_ref[...] = jnp.zeros_like(state_ref)
        tail_ref[...] = jnp.zeros_like(tail_ref)

    for i, x_ref in enumerate((q_ref, k_ref, v_ref)):
        xbuf_ref[i, 0:8, :] = tail_ref[i]
        xbuf_ref[i, 8:8 + blk, :] = x_ref[...].astype(F32)
        tail_ref[i] = xbuf_ref[i, blk:blk + 8, :]

    def conv_silu(i, h):
        lo = i * dn + h * dk
        cw = cw_ref[:, lo:lo + dk]
        y = None
        for s in range(CONV_WIDTH):
            tap = CONV_WIDTH - 1 - s
            term = xbuf_ref[i, 8 - s:8 - s + blk, h * dk:(h + 1) * dk] * cw[tap:tap + 1, :]
            y = term if y is None else y + term
        return y * _sigmoid(y)

    gate = gate_ref[...]
    beta_all = _sigmoid(gate)
    xs = gate + dtb_ref[...]
    softplus = jnp.maximum(xs, 0.0) + jnp.log(1.0 + jnp.exp(-jnp.abs(xs)))
    g_all = -jnp.exp(alog_ref[...]) * softplus
    pos = lax.broadcasted_iota(jnp.int32, g_all.shape, 0) & (CHUNK - 1)
    gc_all = g_all
    step = 1
    while step < CHUNK:
        gc_all = gc_all + jnp.where(pos >= step, pltpu.roll(gc_all, step, 0), 0.0)
        step *= 2

    ri = lax.broadcasted_iota(jnp.int32, (CHUNK, CHUNK), 0)
    ci = lax.broadcasted_iota(jnp.int32, (CHUNK, CHUNK), 1)
    tril = ri >= ci
    strict = ri > ci
    eye = ri == ci
    gain = dnn_ref[...]

    heads = []
    for h in range(n_heads):
        q = conv_silu(0, h)
        k = conv_silu(1, h)
        v = conv_silu(2, h)
        q = q * lax.rsqrt(jnp.sum(q * q, axis=-1, keepdims=True) + 1e-6) * (dk ** -0.5)
        k = k * lax.rsqrt(jnp.sum(k * k, axis=-1, keepdims=True) + 1e-6)
        beta = jnp.broadcast_to(beta_all[:, h:h + 1], (blk, dk))
        gc = jnp.broadcast_to(gc_all[:, n_heads + h:n_heads + h + 1], (blk, dk))
        heads.append((q, k, v, beta, gc))

    items = [(h, c) for c in range(n_chunks) for h in range(n_heads)]
    st = []
    for h, c in items:
        q, k, v, beta, gc = heads[h]
        rows = slice(c * CHUNK, (c + 1) * CHUNK)
        qc, kc, vc, bc, gcc = q[rows], k[rows], v[rows], beta[rows], gc[rows]
        g_last = gcc[CHUNK - 1:CHUNK, :]
        eg = jnp.exp(gcc)
        g_sq = gcc[:, 0:CHUNK]
        g_row = jnp.sum(jnp.where(eye, g_sq, 0.0), axis=0, keepdims=True)
        decay = jnp.exp(jnp.where(tril, g_sq - g_row, -jnp.inf))
        kb = kc * bc
        lhs = jnp.concatenate([kb, qc], axis=0).astype(BF16)
        aq = _dot_nt(lhs, kc.astype(BF16))
        n_mat = jnp.where(strict, -(aq[:CHUNK] * decay), 0.0)
        st.append(dict(
            qe=qc * eg, g_last=g_last,
            rhs=jnp.concatenate([kb * eg, vc * bc], axis=1).astype(BF16),
            qk=jnp.where(tril, aq[CHUNK:] * decay, 0.0).astype(BF16),
            kd_t=(kc * jnp.exp(g_last - gcc)).T.astype(BF16),
            power=n_mat, inv=jnp.where(eye, 1.0, 0.0) + n_mat))

    span = 2
    while span < CHUNK:
        for s in st:
            pb = s["power"].astype(BF16)
            s["power"] = _dot(pb, pb)
        for s in st:
            s["inv"] = s["inv"] + _dot(s["inv"].astype(BF16), s["power"].astype(BF16))
        span *= 2

    for s in st:
        s["wu"] = _dot(s["inv"].astype(BF16), s["rhs"]).astype(BF16)
    for s in st:
        s["gr"] = _dot(s["kd_t"], s["wu"])
        qw = _dot(s["qk"], s["wu"])
        s["q_eff"] = (s["qe"] - qw[:, :dk]).astype(BF16)
        s["p_loc"] = qw[:, dk:]

    for (h, c), s in zip(items, st):
        state = state_ref[h]
        s_bf = state.astype(BF16)
        o = _dot(s["q_eff"], s_bf) + s["p_loc"]
        state_ref[h] = (state * jnp.exp(s["g_last"]) - _dot(s["gr"][:, :dk].astype(BF16), s_bf)
                        + s["gr"][:, dk:])
        rows = slice(c * CHUNK, (c + 1) * CHUNK)
        zc = z_ref[rows, h * dk:(h + 1) * dk].astype(F32)
        o = o * lax.rsqrt(jnp.mean(o * o, axis=-1, keepdims=True) + 1e-6) * gain
        y_ref[rows, h * dk:(h + 1) * dk] = (o * (zc * _sigmoid(zc))).astype(y_ref.dtype)


def _gdn(proj3, gate3, conv_w, alog_row, dtb_row, dn_norm, *, n_heads, blk):
    B, S, _ = proj3.shape
    dk = DN_HEAD_DIM
    dn = n_heads * dk

    def group(idx):
        return pl.BlockSpec((None, blk, dn), lambda b, t: (b, t, idx))

    def whole(arr):
        return pl.BlockSpec(arr.shape, lambda b, t: (0,) * arr.ndim)

    return pl.pallas_call(
        functools.partial(_gdn_kernel, blk=blk, n_heads=n_heads),
        grid=(B, S // blk),
        in_specs=[
            group(0), group(1), group(2), group(3),
            pl.BlockSpec((None, blk, GATE_COLS), lambda b, t: (b, t, 0)),
            whole(conv_w), whole(alog_row), whole(dtb_row), whole(dn_norm),
        ],
        out_specs=pl.BlockSpec((None, blk, dn), lambda b, t: (b, t, 0)),
        out_shape=jax.ShapeDtypeStruct((B, S, dn), BF16),
        scratch_shapes=[
            pltpu.VMEM((n_heads, dk, dk), F32),
            pltpu.VMEM((3, 8, dn), F32),
            pltpu.VMEM((3, 8 + blk, dn), F32),
        ],
        compiler_params=pltpu.CompilerParams(
            dimension_semantics=("arbitrary", "arbitrary"),
            vmem_limit_bytes=V7X_VMEM_LIMIT),
        name="gdn",
    )(proj3, proj3, proj3, proj3, gate3, conv_w, alog_row, dtb_row, dn_norm)


def _t5_bucket_starts():
    max_exact = NUM_BUCKETS // 2
    n = np.arange(0, MAX_DISTANCE + 1)
    nf = np.maximum(n, 1).astype(np.float32)
    large = max_exact + (np.log(nf / max_exact) / math.log(MAX_DISTANCE / max_exact)
                         * (NUM_BUCKETS - max_exact)).astype(np.int32)
    bucket = np.where(n < max_exact, n, np.minimum(large, NUM_BUCKETS - 1))
    assert bucket[MAX_DISTANCE] == NUM_BUCKETS - 1 and np.all(np.diff(bucket) >= 0)
    starts = [(0, int(bucket[0]))]
    for d in range(1, MAX_DISTANCE + 1):
        if bucket[d] != bucket[d - 1]:
            starts.append((d, int(bucket[d])))
    return starts


def _diff_attn_kernel(qi_tab, kj_tab, rb_ref, q_ref, k_ref, vt_ref,
                      lq1_ref, lk1_ref, lq2_ref, lk2_ref, dfn_ref, y_ref,
                      bias_ref, qs_ref, m_ref, l_ref, acc_ref, *, tq, n_heads, lam_init):
    h = pl.program_id(1)
    p = pl.program_id(2)
    qi = qi_tab[p]
    kj = kj_tab[p]
    d = DF_HEAD_DIM
    sub = ATTN_SUB
    n_sub = tq // sub
    log2e = math.log2(math.e)

    @pl.when(p == 0)
    def _():
        keys = lax.broadcasted_iota(jnp.int32, (sub, sub), 0)
        qrys = lax.broadcasted_iota(jnp.int32, (sub, sub), 1)
        starts = _t5_bucket_starts()
        for idx in range(2):
            dist = qrys - keys + idx * sub
            tile = jnp.full((sub, sub), rb_ref[starts[0][1] * n_heads + h], F32)
            for first, bucket in starts[1:]:
                tile = jnp.where(dist >= first, rb_ref[bucket * n_heads + h], tile)
            tile = tile * log2e
            if idx == 0:
                tile = jnp.where(dist >= 0, tile, -jnp.inf)
            bias_ref[idx] = tile
        bias_ref[2] = jnp.full((sub, sub), rb_ref[(NUM_BUCKETS - 1) * n_heads + h] * log2e, F32)
        bias_ref[3] = jnp.full((sub, sub), -jnp.inf, F32)

    @pl.when(kj == 0)
    def _():
        qs_ref[...] = (q_ref[...].astype(F32) * (d ** -0.5 * log2e)).astype(BF16)
        m_ref[...] = jnp.full(m_ref.shape, -jnp.inf, F32)
        l_ref[...] = jnp.zeros_like(l_ref)
        acc_ref[...] = jnp.zeros_like(acc_ref)

    chains = [(qb, m) for qb in range(n_sub) for m in range(2)]

    def raw_logits(qb, m):
        return _dot_nt(k_ref[:, m * d:(m + 1) * d],
                       qs_ref[qb * sub:(qb + 1) * sub, m * d:(m + 1) * d])

    def near_logits(qb, m):
        s = raw_logits(qb, m)
        parts = []
        for kb in range(n_sub):
            off = (qi - kj) * n_sub + (qb - kb)
            idx = jnp.where(off < 0, 3, jnp.minimum(off, 2))
            parts.append(s[kb * sub:(kb + 1) * sub] + bias_ref[idx])
        return jnp.concatenate(parts, axis=0)

    def accumulate(qb, m, s, shift):
        cols = slice(qb * sub, (qb + 1) * sub)
        m_prev = m_ref[m, :, cols]
        m_new = jnp.maximum(m_prev, jnp.max(s, axis=0, keepdims=True) + shift)
        alpha = jnp.exp2(m_prev - m_new)
        pr = jnp.exp2(s - (m_new - shift))
        l_ref[m, :, cols] = alpha * l_ref[m, :, cols] + jnp.sum(pr, axis=0, keepdims=True)
        pv = _dot(vt_ref[...], pr.astype(BF16))
        acc_ref[m, :, cols] = alpha * acc_ref[m, :, cols] + pv
        m_ref[m, :, cols] = m_new

    def run(logits_fn, shift):
        ahead = ATTN_LOGITS_AHEAD
        pending = [logits_fn(*ch) for ch in chains[:ahead]]
        for i, ch in enumerate(chains):
            if i + ahead < len(chains):
                pending.append(logits_fn(*chains[i + ahead]))
            accumulate(*ch, pending.pop(0), shift)

    far = qi - kj >= 2

    @pl.when(far)
    def _():
        run(raw_logits, rb_ref[(NUM_BUCKETS - 1) * n_heads + h] * log2e)

    @pl.when(jnp.logical_not(far))
    def _():
        run(near_logits, 0.0)

    @pl.when(kj == qi)
    def _():
        lam = (jnp.exp(jnp.sum(lq1_ref[...] * lk1_ref[...], axis=-1, keepdims=True))
               - jnp.exp(jnp.sum(lq2_ref[...] * lk2_ref[...], axis=-1, keepdims=True))
               + lam_init)
        o = acc_ref[0] * (1.0 / l_ref[0]) - acc_ref[1] * (lam / l_ref[1])
        o = o * lax.rsqrt(jnp.mean(o * o, axis=0, keepdims=True) + 1e-5) * dfn_ref[...]
        y_ref[...] = (o * (1.0 - lam_init)).T.astype(y_ref.dtype)


def _diff_attn(proj3, v_t, rel_bias, lq1, lk1, lq2, lk2, df_norm_col, *, n_heads, col0, tq,
               lam_init):
    B, S, _ = proj3.shape
    d2 = 2 * DF_HEAD_DIM
    assert tq % ATTN_SUB == 0 and ATTN_SUB >= MAX_DISTANCE
    nq = S // tq
    pairs = [(i, j) for i in range(nq) for j in range(i + 1)]
    qi_tab = jnp.asarray([pq[0] for pq in pairs], jnp.int32)
    kj_tab = jnp.asarray([pq[1] for pq in pairs], jnp.int32)
    cb = col0 // d2
    vec = lambda n: pl.BlockSpec((1, n), lambda b, h, p, qt, kt: (0, 0))
    grid_spec = pltpu.PrefetchScalarGridSpec(
        num_scalar_prefetch=2,
        grid=(B, n_heads, len(pairs)),
        in_specs=[
            pl.BlockSpec(memory_space=pltpu.SMEM),
            pl.BlockSpec((None, tq, d2), lambda b, h, p, qt, kt: (b, qt[p], cb + h)),
            pl.BlockSpec((None, tq, d2), lambda b, h, p, qt, kt: (b, kt[p], cb + n_heads + h)),
            pl.BlockSpec((None, d2, tq), lambda b, h, p, qt, kt: (b, h, kt[p])),
            vec(DF_HEAD_DIM), vec(DF_HEAD_DIM), vec(DF_HEAD_DIM), vec(DF_HEAD_DIM),
            pl.BlockSpec((d2, 1), lambda b, h, p, qt, kt: (0, 0)),
        ],
        out_specs=pl.BlockSpec((None, tq, d2), lambda b, h, p, qt, kt: (b, qt[p], h)),
        scratch_shapes=[
            pltpu.VMEM((4, ATTN_SUB, ATTN_SUB), F32),
            pltpu.VMEM((tq, d2), BF16),
            pltpu.VMEM((2, 1, tq), F32),
            pltpu.VMEM((2, 1, tq), F32),
            pltpu.VMEM((2, d2, tq), F32),
        ],
    )
    return pl.pallas_call(
        functools.partial(_diff_attn_kernel, tq=tq, n_heads=n_heads, lam_init=lam_init),
        grid_spec=grid_spec,
        out_shape=jax.ShapeDtypeStruct((B, S, n_heads * d2), BF16),
        compiler_params=pltpu.CompilerParams(
            dimension_semantics=("arbitrary", "arbitrary", "arbitrary")),
        name="diff_attn",
    )(qi_tab, kj_tab, rel_bias, proj3, proj3, v_t, lq1, lk1, lq2, lk2, df_norm_col)


def _out_proj_kernel(x_ref, ya_ref, yb_ref, wa_ref, wb_ref, h_ref):
    h_ref[...] = x_ref[...] + _dot(ya_ref[...], wa_ref[...]) + _dot(yb_ref[...], wb_ref[...])


def _out_proj(x2, y_dn, y_df, w_o, *, tm, tn):
    T, D = x2.shape
    ka = y_dn.shape[1]
    kb = y_df.shape[1]
    assert ka == kb
    return pl.pallas_call(
        _out_proj_kernel,
        grid=(T // tm, D // tn),
        in_specs=[
            pl.BlockSpec((tm, tn), lambda i, j: (i, j)),
            pl.BlockSpec((tm, ka), lambda i, j: (i, 0)),
            pl.BlockSpec((tm, kb), lambda i, j: (i, 0)),
            pl.BlockSpec((ka, tn), lambda i, j: (0, j)),
            pl.BlockSpec((kb, tn), lambda i, j: (1, j)),
        ],
        out_specs=pl.BlockSpec((tm, tn), lambda i, j: (i, j)),
        out_shape=jax.ShapeDtypeStruct((T, D), F32),
        compiler_params=pltpu.CompilerParams(
            dimension_semantics=("arbitrary", "arbitrary"),
            vmem_limit_bytes=V7X_VMEM_LIMIT),
        name="out_proj",
    )(x2, y_dn, y_df, w_o, w_o)


def _mlp_kernel(h_ref, g_ref, wu_ref, wd_ref, gf_ref, o_ref, u_ref, acc_ref):
    f = pl.program_id(1)

    @pl.when(f == 0)
    def _():
        x = h_ref[...]
        ms = jnp.mean(x * x, axis=-1, keepdims=True)
        u_ref[...] = (x * lax.rsqrt(ms + 1e-6) * g_ref[...]).astype(BF16)
        acc_ref[...] = jnp.zeros_like(acc_ref)

    hid = jnp.maximum(_dot(u_ref[...], wu_ref[...]), 0.0)
    acc_ref[...] += _dot((hid * hid).astype(BF16), wd_ref[...])

    @pl.when(f == pl.num_programs(1) - 1)
    def _():
        y = h_ref[...] + acc_ref[...]
        ms = jnp.mean(y * y, axis=-1, keepdims=True)
        o_ref[...] = y * lax.rsqrt(ms + 1e-6) * gf_ref[...]


def _mlp(h1, gain, w_up, w_down, final_gain, *, tm, tf):
    T, D = h1.shape
    Fdim = w_up.shape[1]
    return pl.pallas_call(
        _mlp_kernel,
        grid=(T // tm, Fdim // tf),
        in_specs=[
            pl.BlockSpec((tm, D), lambda i, f: (i, 0)),
            pl.BlockSpec((1, D), lambda i, f: (0, 0)),
            pl.BlockSpec((D, tf), lambda i, f: (0, f)),
            pl.BlockSpec((tf, D), lambda i, f: (f, 0)),
            pl.BlockSpec((1, D), lambda i, f: (0, 0)),
        ],
        out_specs=pl.BlockSpec((tm, D), lambda i, f: (i, 0)),
        out_shape=jax.ShapeDtypeStruct((T, D), F32),
        scratch_shapes=[pltpu.VMEM((tm, D), BF16), pltpu.VMEM((tm, D), F32)],
        compiler_params=pltpu.CompilerParams(
            dimension_semantics=("arbitrary", "arbitrary"),
            vmem_limit_bytes=V7X_VMEM_LIMIT),
        name="mlp",
    )(h1, gain, w_up, w_down, final_gain)


def _tile(n, pref):
    if n <= pref:
        return n
    t = pref - pref % 128
    while t > 128 and n % t:
        t -= 128
    assert n % t == 0
    return t


def kernel(x, attn_norm, w_in, conv_w, a_log, dt_bias, dn_norm, lambda_q1, lambda_k1,
           lambda_q2, lambda_k2, df_norm, rel_bias, w_o, mlp_norm, w_up, w_down, final_norm):
    B, S, D = x.shape
    depth = attn_norm.shape[0]
    n_dn = a_log.shape[1]
    n_df = rel_bias.shape[1]
    dn_dim = n_dn * DN_HEAD_DIM
    df_dim = n_df * 2 * DF_HEAD_DIM
    T = B * S
    gate0 = 4 * dn_dim
    assert w_in.shape[2] == gate0 + 2 * n_dn + 3 * df_dim and 2 * n_dn <= GATE_COLS

    assert depth == 1
    l = 0
    h = x.reshape(T, D)

    wl = w_in[l]
    dfq0 = gate0 + 2 * n_dn
    dfv0 = dfq0 + 2 * df_dim
    w_main = jnp.concatenate([wl[:, :gate0], wl[:, dfq0:dfv0]], axis=1).astype(BF16)
    w_gate = jnp.pad(wl[:, gate0:dfq0], ((0, 0), (0, GATE_COLS - 2 * n_dn))).astype(BF16)
    w_vt = wl[:, dfv0:].T.astype(BF16)
    w_o_bf = w_o[l].astype(BF16)
    w_up_bf = w_up[l].astype(BF16)
    w_down_bf = w_down[l].astype(BF16)

    proj, gates, v_t = _in_proj(h, attn_norm[l][None, :], w_main, w_gate, w_vt, batch=B,
                                tm=_tile(S, 1024),
                                tn=_tile(math.gcd(w_main.shape[1], df_dim), 1024))
    proj3 = proj.reshape(B, S, -1)
    gate3 = gates.reshape(B, S, GATE_COLS)

    gate_pad = ((0, 0), (n_dn, GATE_COLS - 2 * n_dn))
    alog_row = jnp.pad(a_log[l][None, :], gate_pad)
    dtb_row = jnp.pad(dt_bias[l][None, :], gate_pad)
    y_dn = _gdn(proj3, gate3, conv_w[l], alog_row, dtb_row, dn_norm[l][None, :],
                n_heads=n_dn, blk=_tile(S, 128))
    lam_init = 0.8 - 0.6 * math.exp(-0.3 * l)
    y_df = _diff_attn(proj3, v_t, rel_bias.reshape(-1),
                      lambda_q1[l][None, :], lambda_k1[l][None, :],
                      lambda_q2[l][None, :], lambda_k2[l][None, :], df_norm[l][:, None],
                      n_heads=n_df, col0=gate0, tq=_tile(S, 512), lam_init=lam_init)

    h1 = _out_proj(h, y_dn.reshape(T, dn_dim), y_df.reshape(T, df_dim), w_o_bf,
                   tm=_tile(T, 1024), tn=_tile(D, 1024))
    out = _mlp(h1, mlp_norm[l][None, :], w_up_bf, w_down_bf, final_norm[None, :],
               tm=_tile(T, 512), tf=_tile(w_up_bf.shape[1], 1024))
    return out.reshape(B, S, D)
```

```python
import functools
import math

import numpy as np
import jax
import jax.numpy as jnp
from jax import lax
from jax.experimental import pallas as pl
from jax.experimental.pallas import tpu as pltpu

F32 = jnp.float32
BF16 = jnp.bfloat16

DN_HEAD_DIM = 128
DF_HEAD_DIM = 128
CONV_WIDTH = 4
CHUNK = 64
NUM_BUCKETS = 32
MAX_DISTANCE = 128
GATE_COLS = 128
ATTN_SUB = 256
ATTN_BLOCK = 512

V7X_VMEM_LIMIT = 56 * 1024 * 1024


def _dot(a, b):
    return jnp.dot(a, b, preferred_element_type=F32)


def _dot_nt(a, b):
    return lax.dot_general(a, b, (((1,), (1,)), ((), ())), preferred_element_type=F32)


def _sigmoid(x):
    return 1.0 / (1.0 + jnp.exp(-x))


def _in_proj_kernel(x_ref, g_ref, w_ref, wg_ref, wvt_ref, proj_ref, gate_ref, vt_ref, u_ref,
                    *, n_main):
    j = pl.program_id(1)

    @pl.when(j == 0)
    def _():
        x = x_ref[...]
        ms = jnp.mean(x * x, axis=-1, keepdims=True)
        u = (x * lax.rsqrt(ms + 1e-6) * g_ref[...]).astype(BF16)
        u_ref[...] = u
        gate_ref[...] = _dot(u, wg_ref[...])

    @pl.when(j < n_main)
    def _():
        proj_ref[...] = _dot(u_ref[...], w_ref[...]).astype(BF16)

    @pl.when(j >= n_main)
    def _():
        vt = _dot_nt(wvt_ref[...], u_ref[...]).astype(BF16)
        tv = vt_ref.shape[-1]
        for c in range(vt_ref.shape[0]):
            vt_ref[c] = vt[:, c * tv:(c + 1) * tv]


def _in_proj(x2, gain, w_main, w_gate, w_vt, *, batch, tm, tn, tv):
    T, D = x2.shape
    N = w_main.shape[1]
    nv = w_vt.shape[0]
    S = T // batch
    assert S % tm == 0 and N % tn == 0 and nv % tn == 0 and tm % tv == 0
    n_main = N // tn
    per_seq = S // tm
    return pl.pallas_call(
        functools.partial(_in_proj_kernel, n_main=n_main),
        grid=(T // tm, n_main + nv // tn),
        in_specs=[
            pl.BlockSpec((tm, D), lambda i, j: (i, 0)),
            pl.BlockSpec((1, D), lambda i, j: (0, 0)),
            pl.BlockSpec((D, tn), lambda i, j: (0, jnp.minimum(j, n_main - 1))),
            pl.BlockSpec((D, GATE_COLS), lambda i, j: (0, 0)),
            pl.BlockSpec((tn, D), lambda i, j: (jnp.maximum(j - n_main, 0), 0)),
        ],
        out_specs=[
            pl.BlockSpec((tm, tn), lambda i, j: (i, jnp.minimum(j, n_main - 1))),
            pl.BlockSpec((tm, GATE_COLS), lambda i, j: (i, 0)),
            pl.BlockSpec((None, tm // tv, tn, tv),
                         lambda i, j: (i // per_seq, i % per_seq, jnp.maximum(j - n_main, 0), 0)),
        ],
        out_shape=[
            jax.ShapeDtypeStruct((T, N), BF16),
            jax.ShapeDtypeStruct((T, GATE_COLS), F32),
            jax.ShapeDtypeStruct((batch, S // tv, nv, tv), BF16),
        ],
        scratch_shapes=[pltpu.VMEM((tm, D), BF16)],
        compiler_params=pltpu.CompilerParams(
            dimension_semantics=("arbitrary", "arbitrary"),
            vmem_limit_bytes=V7X_VMEM_LIMIT),
        name="in_proj",
    )(x2, gain, w_main, w_gate, w_vt)


def _gdn_kernel(q_ref, k_ref, v_ref, z_ref, gate_ref, cw_ref, alog_ref, dtb_ref, dnn_ref,
                y_ref, state_ref, tail_ref, xbuf_ref, *, blk, n_heads):
    dk = DN_HEAD_DIM
    dn = n_heads * dk
    n_chunks = blk // CHUNK

    @pl.when(pl.program_id(1) == 0)
    def _():
        state_ref[...] = jnp.zeros_like(state_ref)
        tail_ref[...] = jnp.zeros_like(tail_ref)

    for i, x_ref in enumerate((q_ref, k_ref, v_ref)):
        xbuf_ref[i, 0:8, :] = tail_ref[i]
        xbuf_ref[i, 8:8 + blk, :] = x_ref[...].astype(F32)
        tail_ref[i] = xbuf_ref[i, blk:blk + 8, :]

    def conv_silu(i, h):
        lo = i * dn + h * dk
        cw = cw_ref[:, lo:lo + dk]
        y = None
        for s in range(CONV_WIDTH):
            tap = CONV_WIDTH - 1 - s
            term = xbuf_ref[i, 8 - s:8 - s + blk, h * dk:(h + 1) * dk] * cw[tap:tap + 1, :]
            y = term if y is None else y + term
        return y * _sigmoid(y)

    gate = gate_ref[...]
    beta_all = _sigmoid(gate)
    xs = gate + dtb_ref[...]
    softplus = jnp.maximum(xs, 0.0) + jnp.log(1.0 + jnp.exp(-jnp.abs(xs)))
    g_all = -jnp.exp(alog_ref[...]) * softplus
    pos = lax.broadcasted_iota(jnp.int32, g_all.shape, 0) & (CHUNK - 1)
    gc_all = g_all
    step = 1
    while step < CHUNK:
        gc_all = gc_all + jnp.where(pos >= step, pltpu.roll(gc_all, step, 0), 0.0)
        step *= 2

    ri = lax.broadcasted_iota(jnp.int32, (CHUNK, CHUNK), 0)
    ci = lax.broadcasted_iota(jnp.int32, (CHUNK, CHUNK), 1)
    tril = ri >= ci
    strict = ri > ci
    eye = ri == ci
    gain = dnn_ref[...]

    heads = []
    for h in range(n_heads):
        q = conv_silu(0, h)
        k = conv_silu(1, h)
        v = conv_silu(2, h)
        q = q * lax.rsqrt(jnp.sum(q * q, axis=-1, keepdims=True) + 1e-6) * (dk ** -0.5)
        k = k * lax.rsqrt(jnp.sum(k * k, axis=-1, keepdims=True) + 1e-6)
        beta = jnp.broadcast_to(beta_all[:, h:h + 1], (blk, dk))
        gc = jnp.broadcast_to(gc_all[:, n_heads + h:n_heads + h + 1], (blk, dk))
        heads.append((q, k, v, beta, gc))

    items = [(h, c) for c in range(n_chunks) for h in range(n_heads)]
    st = []
    for h, c in items:
        q, k, v, beta, gc = heads[h]
        rows = slice(c * CHUNK, (c + 1) * CHUNK)
        qc, kc, vc, bc, gcc = q[rows], k[rows], v[rows], beta[rows], gc[rows]
        g_last = gcc[CHUNK - 1:CHUNK, :]
        eg = jnp.exp(gcc)
        g_sq = gcc[:, 0:CHUNK]
        g_row = jnp.sum(jnp.where(eye, g_sq, 0.0), axis=0, keepdims=True)
        decay = jnp.exp(jnp.where(tril, g_sq - g_row, -jnp.inf))
        kb = kc * bc
        lhs = jnp.concatenate([kb, qc], axis=0).astype(BF16)
        aq = _dot_nt(lhs, kc.astype(BF16))
        n_mat = jnp.where(strict, -(aq[:CHUNK] * decay), 0.0)
        st.append(dict(
            qe=qc * eg, g_last=g_last,
            rhs=jnp.concatenate([kb * eg, vc * bc], axis=1).astype(BF16),
            qk=jnp.where(tril, aq[CHUNK:] * decay, 0.0).astype(BF16),
            kd_t=(kc * jnp.exp(g_last - gcc)).T.astype(BF16),
            power=n_mat, inv=jnp.where(eye, 1.0, 0.0) + n_mat))

    span = 2
    while span < CHUNK:
        for s in st:
            pb = s["power"].astype(BF16)
            s["power"] = _dot(pb, pb)
        for s in st:
            s["inv"] = s["inv"] + _dot(s["inv"].astype(BF16), s["power"].astype(BF16))
        span *= 2

    for s in st:
        s["wu"] = _dot(s["inv"].astype(BF16), s["rhs"]).astype(BF16)
    for s in st:
        s["gr"] = _dot(s["kd_t"], s["wu"])
        qw = _dot(s["qk"], s["wu"])
        s["q_eff"] = (s["qe"] - qw[:, :dk]).astype(BF16)
        s["p_loc"] = qw[:, dk:]

    for (h, c), s in zip(items, st):
        state = state_ref[h]
        s_bf = state.astype(BF16)
        o = _dot(s["q_eff"], s_bf) + s["p_loc"]
        state_ref[h] = (state * jnp.exp(s["g_last"]) - _dot(s["gr"][:, :dk].astype(BF16), s_bf)
                        + s["gr"][:, dk:])
        rows = slice(c * CHUNK, (c + 1) * CHUNK)
        zc = z_ref[rows, h * dk:(h + 1) * dk].astype(F32)
        o = o * lax.rsqrt(jnp.mean(o * o, axis=-1, keepdims=True) + 1e-6) * gain
        y_ref[rows, h * dk:(h + 1) * dk] = (o * (zc * _sigmoid(zc))).astype(y_ref.dtype)


def _gdn(proj3, gate3, conv_w, alog_row, dtb_row, dn_norm, *, n_heads, blk):
    B, S, _ = proj3.shape
    dk = DN_HEAD_DIM
    dn = n_heads * dk

    def group(idx):
        return pl.BlockSpec((None, blk, dn), lambda b, t: (b, t, idx))

    def whole(arr):
        return pl.BlockSpec(arr.shape, lambda b, t: (0,) * arr.ndim)

    return pl.pallas_call(
        functools.partial(_gdn_kernel, blk=blk, n_heads=n_heads),
        grid=(B, S // blk),
        in_specs=[
            group(0), group(1), group(2), group(3),
            pl.BlockSpec((None, blk, GATE_COLS), lambda b, t: (b, t, 0)),
            whole(conv_w), whole(alog_row), whole(dtb_row), whole(dn_norm),
        ],
        out_specs=pl.BlockSpec((None, blk, dn), lambda b, t: (b, t, 0)),
        out_shape=jax.ShapeDtypeStruct((B, S, dn), BF16),
        scratch_shapes=[
            pltpu.VMEM((n_heads, dk, dk), F32),
            pltpu.VMEM((3, 8, dn), F32),
            pltpu.VMEM((3, 8 + blk, dn), F32),
        ],
        compiler_params=pltpu.CompilerParams(
            dimension_semantics=("arbitrary", "arbitrary"),
            vmem_limit_bytes=V7X_VMEM_LIMIT),
        name="gdn",
    )(proj3, proj3, proj3, proj3, gate3, conv_w, alog_row, dtb_row, dn_norm)


def _t5_bucket_starts():
    max_exact = NUM_BUCKETS // 2
    n = np.arange(0, MAX_DISTANCE + 1)
    nf = np.maximum(n, 1).astype(np.float32)
    large = max_exact + (np.log(nf / max_exact) / math.log(MAX_DISTANCE / max_exact)
                         * (NUM_BUCKETS - max_exact)).astype(np.int32)
    bucket = np.where(n < max_exact, n, np.minimum(large, NUM_BUCKETS - 1))
    assert bucket[MAX_DISTANCE] == NUM_BUCKETS - 1 and np.all(np.diff(bucket) >= 0)
    starts = [(0, int(bucket[0]))]
    for d in range(1, MAX_DISTANCE + 1):
        if bucket[d] != bucket[d - 1]:
            starts.append((d, int(bucket[d])))
    return starts


def _diff_attn_kernel(rb_ref, q_ref, k_ref, vt_ref, lq1_ref, lk1_ref, lq2_ref, lk2_ref, dfn_ref,
                      y_ref, bias_ref, qs_ref, s_ref, p_ref, m_ref, l_ref, acc_ref,
                      *, tq, n_heads, lam_init):
    h = pl.program_id(1)
    qi = pl.program_id(2)
    d = DF_HEAD_DIM
    sub = ATTN_SUB
    n_sub = tq // sub
    log2e = math.log2(math.e)

    @pl.when(qi == 0)
    def _():
        keys = lax.broadcasted_iota(jnp.int32, (sub, sub), 0)
        qrys = lax.broadcasted_iota(jnp.int32, (sub, sub), 1)
        starts = _t5_bucket_starts()
        for idx in range(2):
            dist = qrys - keys + idx * sub
            tile = jnp.full((sub, sub), rb_ref[starts[0][1] * n_heads + h], F32)
            for first, bucket in starts[1:]:
                tile = jnp.where(dist >= first, rb_ref[bucket * n_heads + h], tile)
            tile = tile * log2e
            if idx == 0:
                tile = jnp.where(dist >= 0, tile, -jnp.inf)
            bias_ref[idx] = tile
        bias_ref[2] = jnp.full((sub, sub), rb_ref[(NUM_BUCKETS - 1) * n_heads + h] * log2e, F32)
        bias_ref[3] = jnp.full((sub, sub), -jnp.inf, F32)

    qs_ref[...] = (q_ref[...].astype(F32) * (d ** -0.5 * log2e)).astype(BF16)
    m_ref[...] = jnp.full(m_ref.shape, -jnp.inf, F32)
    l_ref[...] = jnp.zeros_like(l_ref)
    acc_ref[...] = jnp.zeros_like(acc_ref)

    chains = [(qb, m) for qb in range(n_sub) for m in range(2)]

    far_shift = rb_ref[(NUM_BUCKETS - 1) * n_heads + h] * log2e

    def produce(j, c):
        qb, m = chains[c]
        k0 = pl.multiple_of(j * tq, tq)
        s_ref[c] = _dot_nt(k_ref[pl.ds(k0, tq), m * d:(m + 1) * d],
                           qs_ref[qb * sub:(qb + 1) * sub, m * d:(m + 1) * d])

    def softmax(j, c, far):
        qb, m = chains[c]
        if far:
            s = s_ref[c]
            shift = far_shift
        else:
            parts = []
            for kb in range(n_sub):
                off = (qi - j) * n_sub + (qb - kb)
                idx = jnp.where(off < 0, 3, jnp.minimum(off, 2))
                parts.append(s_ref[c, kb * sub:(kb + 1) * sub, :] + bias_ref[idx])
            s = jnp.concatenate(parts, axis=0)
            shift = 0.0
        cols = slice(qb * sub, (qb + 1) * sub)
        m_prev = m_ref[m, :, cols]
        m_new = jnp.maximum(m_prev, jnp.max(s, axis=0, keepdims=True) + shift)
        alpha = jnp.exp2(m_prev - m_new)
        pr = jnp.exp2(s - (m_new - shift))
        l_ref[m, :, cols] = alpha * l_ref[m, :, cols] + jnp.sum(pr, axis=0, keepdims=True)
        acc_ref[m, :, cols] = alpha * acc_ref[m, :, cols]
        m_ref[m, :, cols] = m_new
        return pr.astype(BF16)

    def add_values(j, c, p):
        qb, m = chains[c]
        cols = slice(qb * sub, (qb + 1) * sub)
        acc_ref[m, :, cols] += _dot(vt_ref[j], p)

    n_chains = len(chains)
    for c in range(n_chains):
        produce(0, c)
    p_ref[...] = softmax(0, 0, False)

    def trip(j, far):
        add_values(j, 0, p_ref[...])
        produce(j + 1, 0)
        for c in range(1, n_chains):
            add_values(j, c, softmax(j, c, far))
            produce(j + 1, c)
        p_ref[...] = softmax(j + 1, 0, far)

    n_far = jnp.maximum(qi - 2, 0)
    lax.fori_loop(0, n_far, lambda j, carry: (trip(j, True), carry)[1], 0)
    lax.fori_loop(n_far, qi, lambda j, carry: (trip(j, False), carry)[1], 0)
    add_values(qi, 0, p_ref[...])
    for c in range(1, n_chains):
        add_values(qi, c, softmax(qi, c, False))

    lam = (jnp.exp(jnp.sum(lq1_ref[...] * lk1_ref[...], axis=-1, keepdims=True))
           - jnp.exp(jnp.sum(lq2_ref[...] * lk2_ref[...], axis=-1, keepdims=True))
           + lam_init)
    o = acc_ref[0] * (1.0 / l_ref[0]) - acc_ref[1] * (lam / l_ref[1])
    o = o * lax.rsqrt(jnp.mean(o * o, axis=0, keepdims=True) + 1e-5) * dfn_ref[...]
    y_ref[...] = (o * (1.0 - lam_init)).T.astype(y_ref.dtype)


def _diff_attn(proj3, v_t, rel_bias, lq1, lk1, lq2, lk2, df_norm_col, *, n_heads, col0, tq,
               lam_init):
    B, S, _ = proj3.shape
    d2 = 2 * DF_HEAD_DIM
    assert tq % ATTN_SUB == 0 and ATTN_SUB >= MAX_DISTANCE and S % tq == 0
    nq = S // tq
    n_chains = 2 * (tq // ATTN_SUB)
    cb = col0 // d2
    vec = lambda n: pl.BlockSpec((1, n), lambda b, h, i: (0, 0))
    return pl.pallas_call(
        functools.partial(_diff_attn_kernel, tq=tq, n_heads=n_heads, lam_init=lam_init),
        grid=(B, n_heads, nq),
        in_specs=[
            pl.BlockSpec(memory_space=pltpu.SMEM),
            pl.BlockSpec((None, tq, d2), lambda b, h, i: (b, i, cb + h)),
            pl.BlockSpec((None, S, d2), lambda b, h, i: (b, 0, cb + n_heads + h)),
            pl.BlockSpec((None, nq, d2, tq), lambda b, h, i: (b, 0, h, 0)),
            vec(DF_HEAD_DIM), vec(DF_HEAD_DIM), vec(DF_HEAD_DIM), vec(DF_HEAD_DIM),
            pl.BlockSpec((d2, 1), lambda b, h, i: (0, 0)),
        ],
        out_specs=pl.BlockSpec((None, tq, d2), lambda b, h, i: (b, i, h)),
        out_shape=jax.ShapeDtypeStruct((B, S, n_heads * d2), BF16),
        scratch_shapes=[
            pltpu.VMEM((4, ATTN_SUB, ATTN_SUB), F32),
            pltpu.VMEM((tq, d2), BF16),
            pltpu.VMEM((n_chains, tq, ATTN_SUB), F32),
            pltpu.VMEM((tq, ATTN_SUB), BF16),
            pltpu.VMEM((2, 1, tq), F32),
            pltpu.VMEM((2, 1, tq), F32),
            pltpu.VMEM((2, d2, tq), F32),
        ],
        compiler_params=pltpu.CompilerParams(
            dimension_semantics=("arbitrary", "arbitrary", "arbitrary"),
            vmem_limit_bytes=V7X_VMEM_LIMIT),
        name="diff_attn",
    )(rel_bias, proj3, proj3, v_t, lq1, lk1, lq2, lk2, df_norm_col)


def _out_proj_kernel(x_ref, ya_ref, yb_ref, wa_ref, wb_ref, h_ref):
    h_ref[...] = x_ref[...] + _dot(ya_ref[...], wa_ref[...]) + _dot(yb_ref[...], wb_ref[...])


def _out_proj(x2, y_dn, y_df, w_o, *, tm, tn):
    T, D = x2.shape
    ka = y_dn.shape[1]
    kb = y_df.shape[1]
    assert ka == kb
    return pl.pallas_call(
        _out_proj_kernel,
        grid=(T // tm, D // tn),
        in_specs=[
            pl.BlockSpec((tm, tn), lambda i, j: (i, j)),
            pl.BlockSpec((tm, ka), lambda i, j: (i, 0)),
            pl.BlockSpec((tm, kb), lambda i, j: (i, 0)),
            pl.BlockSpec((ka, tn), lambda i, j: (0, j)),
            pl.BlockSpec((kb, tn), lambda i, j: (1, j)),
        ],
        out_specs=pl.BlockSpec((tm, tn), lambda i, j: (i, j)),
        out_shape=jax.ShapeDtypeStruct((T, D), F32),
        compiler_params=pltpu.CompilerParams(
            dimension_semantics=("arbitrary", "arbitrary"),
            vmem_limit_bytes=V7X_VMEM_LIMIT),
        name="out_proj",
    )(x2, y_dn, y_df, w_o, w_o)


def _mlp_kernel(h_ref, g_ref, wu_ref, wd_ref, gf_ref, o_ref, u_ref, acc_ref):
    f = pl.program_id(1)

    @pl.when(f == 0)
    def _():
        x = h_ref[...]
        ms = jnp.mean(x * x, axis=-1, keepdims=True)
        u_ref[...] = (x * lax.rsqrt(ms + 1e-6) * g_ref[...]).astype(BF16)
        acc_ref[...] = jnp.zeros_like(acc_ref)

    hid = jnp.maximum(_dot(u_ref[...], wu_ref[...]), 0.0)
    acc_ref[...] += _dot((hid * hid).astype(BF16), wd_ref[...])

    @pl.when(f == pl.num_programs(1) - 1)
    def _():
        y = h_ref[...] + acc_ref[...]
        ms = jnp.mean(y * y, axis=-1, keepdims=True)
        o_ref[...] = y * lax.rsqrt(ms + 1e-6) * gf_ref[...]


def _mlp(h1, gain, w_up, w_down, final_gain, *, tm, tf):
    T, D = h1.shape
    Fdim = w_up.shape[1]
    return pl.pallas_call(
        _mlp_kernel,
        grid=(T // tm, Fdim // tf),
        in_specs=[
            pl.BlockSpec((tm, D), lambda i, f: (i, 0)),
            pl.BlockSpec((1, D), lambda i, f: (0, 0)),
            pl.BlockSpec((D, tf), lambda i, f: (0, f)),
            pl.BlockSpec((tf, D), lambda i, f: (f, 0)),
            pl.BlockSpec((1, D), lambda i, f: (0, 0)),
        ],
        out_specs=pl.BlockSpec((tm, D), lambda i, f: (i, 0)),
        out_shape=jax.ShapeDtypeStruct((T, D), F32),
        scratch_shapes=[pltpu.VMEM((tm, D), BF16), pltpu.VMEM((tm, D), F32)],
        compiler_params=pltpu.CompilerParams(
            dimension_semantics=("arbitrary", "arbitrary"),
            vmem_limit_bytes=V7X_VMEM_LIMIT),
        name="mlp",
    )(h1, gain, w_up, w_down, final_gain)


def _tile(n, pref):
    if n <= pref:
        return n
    t = pref - pref % 128
    while t > 128 and n % t:
        t -= 128
    assert n % t == 0
    return t


def kernel(x, attn_norm, w_in, conv_w, a_log, dt_bias, dn_norm, lambda_q1, lambda_k1,
           lambda_q2, lambda_k2, df_norm, rel_bias, w_o, mlp_norm, w_up, w_down, final_norm):
    B, S, D = x.shape
    depth = attn_norm.shape[0]
    n_dn = a_log.shape[1]
    n_df = rel_bias.shape[1]
    dn_dim = n_dn * DN_HEAD_DIM
    df_dim = n_df * 2 * DF_HEAD_DIM
    T = B * S
    gate0 = 4 * dn_dim
    assert w_in.shape[2] == gate0 + 2 * n_dn + 3 * df_dim and 2 * n_dn <= GATE_COLS

    assert depth == 1
    l = 0
    h = x.reshape(T, D)

    wl = w_in[l]
    dfq0 = gate0 + 2 * n_dn
    dfv0 = dfq0 + 2 * df_dim
    w_main = jnp.concatenate([wl[:, :gate0], wl[:, dfq0:dfv0]], axis=1).astype(BF16)
    w_gate = jnp.pad(wl[:, gate0:dfq0], ((0, 0), (0, GATE_COLS - 2 * n_dn))).astype(BF16)
    w_vt = wl[:, dfv0:].T.astype(BF16)
    w_o_bf = w_o[l].astype(BF16)
    w_up_bf = w_up[l].astype(BF16)
    w_down_bf = w_down[l].astype(BF16)

    proj, gates, v_t = _in_proj(h, attn_norm[l][None, :], w_main, w_gate, w_vt, batch=B,
                                tm=_tile(S, 1024),
                                tn=_tile(math.gcd(w_main.shape[1], df_dim), 1024),
                                tv=_tile(S, ATTN_BLOCK))
    proj3 = proj.reshape(B, S, -1)
    gate3 = gates.reshape(B, S, GATE_COLS)

    gate_pad = ((0, 0), (n_dn, GATE_COLS - 2 * n_dn))
    alog_row = jnp.pad(a_log[l][None, :], gate_pad)
    dtb_row = jnp.pad(dt_bias[l][None, :], gate_pad)
    y_dn = _gdn(proj3, gate3, conv_w[l], alog_row, dtb_row, dn_norm[l][None, :],
                n_heads=n_dn, blk=_tile(S, 128))
    lam_init = 0.8 - 0.6 * math.exp(-0.3 * l)
    y_df = _diff_attn(proj3, v_t, rel_bias.reshape(-1),
                      lambda_q1[l][None, :], lambda_k1[l][None, :],
                      lambda_q2[l][None, :], lambda_k2[l][None, :], df_norm[l][:, None],
                      n_heads=n_df, col0=gate0, tq=_tile(S, ATTN_BLOCK), lam_init=lam_init)

    h1 = _out_proj(h, y_dn.reshape(T, dn_dim), y_df.reshape(T, df_dim), w_o_bf,
                   tm=_tile(T, 1024), tn=_tile(D, 1024))
    out = _mlp(h1, mlp_norm[l][None, :], w_up_bf, w_down_bf, final_norm[None, :],
               tm=_tile(T, 512), tf=_tile(w_up_bf.shape[1], 1024))
    return out.reshape(B, S, D)
```

```python
import functools
import math

import numpy as np
import jax
import jax.numpy as jnp
from jax import lax
from jax.experimental import pallas as pl
from jax.experimental.pallas import tpu as pltpu

F32 = jnp.float32
BF16 = jnp.bfloat16

DN_HEAD_DIM = 128
DF_HEAD_DIM = 128
CONV_WIDTH = 4
CHUNK = 64
NUM_BUCKETS = 32
MAX_DISTANCE = 128
GATE_COLS = 128
IN_PROJ_COLS = 512
IN_PROJ_ROWS = 256
GDN_BLOCK = 512
GDN_GROUP = 4
ATTN_SUB = 256
ATTN_BLOCK = 512

V7X_VMEM_LIMIT = 56 * 1024 * 1024


def _dot(a, b):
    return jnp.dot(a, b, preferred_element_type=F32)


def _dot_nt(a, b):
    return lax.dot_general(a, b, (((1,), (1,)), ((), ())), preferred_element_type=F32)


def _sigmoid(x):
    return 1.0 / (1.0 + jnp.exp(-x))


def _in_proj_kernel(x_ref, g_ref, w_ref, wg_ref, wvt_ref, cw_ref, proj_ref, gate_ref, vt_ref,
                    u_ref, hist_ref, cbuf_ref, *, bounds, per_seq, q_scale):
    i = pl.program_id(0)
    j = pl.program_id(1)
    tm, tn = proj_ref.shape
    b0, b1, b2, b3, b4, b5 = bounds

    @pl.when(j == 0)
    def _():
        x = x_ref[...]
        ms = jnp.mean(x * x, axis=-1, keepdims=True)
        u = (x * lax.rsqrt(ms + 1e-6) * g_ref[...]).astype(BF16)
        u_ref[...] = u
        gate_ref[...] = _dot(u, wg_ref[...])

    n_row_chunks = tm // IN_PROJ_ROWS
    first_of_seq = (i % per_seq) == 0

    def raw_rows(c):
        rows = slice(c * IN_PROJ_ROWS, (c + 1) * IN_PROJ_ROWS)
        cbuf_ref[8 + c * IN_PROJ_ROWS:8 + (c + 1) * IN_PROJ_ROWS, :] = _dot(u_ref[rows, :], w_ref[...])

    def conv_silu(c):
        r0 = 8 + c * IN_PROJ_ROWS
        cw = cw_ref[...]
        y = None
        for s in range(CONV_WIDTH):
            tap = CONV_WIDTH - 1 - s
            term = cbuf_ref[r0 - s:r0 - s + IN_PROJ_ROWS, :] * cw[tap:tap + 1, :]
            y = term if y is None else y + term
        return y * _sigmoid(y)

    def l2norm_heads(y, scale):
        outs = []
        for h in range(tn // DN_HEAD_DIM):
            yh = y[:, h * DN_HEAD_DIM:(h + 1) * DN_HEAD_DIM]
            outs.append(yh * (lax.rsqrt(jnp.sum(yh * yh, axis=-1, keepdims=True) + 1e-6) * scale))
        return jnp.concatenate(outs, axis=1)

    def project(epilogue, conv=False):
        if conv:
            cbuf_ref[0:8, :] = jnp.where(first_of_seq, 0.0, hist_ref[j])
        raw_rows(0)
        for c in range(n_row_chunks):
            if c + 1 < n_row_chunks:
                raw_rows(c + 1)
            rows = slice(c * IN_PROJ_ROWS, (c + 1) * IN_PROJ_ROWS)
            proj_ref[rows, :] = epilogue(c).astype(BF16)
        if conv:
            hist_ref[j] = cbuf_ref[tm:tm + 8, :]

    def raw(c):
        return cbuf_ref[8 + c * IN_PROJ_ROWS:8 + (c + 1) * IN_PROJ_ROWS, :]

    def silu_rows(c):
        z = raw(c)
        return z * _sigmoid(z)

    @pl.when(j < b0)
    def _():
        project(lambda c: l2norm_heads(conv_silu(c), DN_HEAD_DIM ** -0.5), conv=True)

    @pl.when((j >= b0) & (j < b1))
    def _():
        project(lambda c: l2norm_heads(conv_silu(c), 1.0), conv=True)

    @pl.when((j >= b1) & (j < b2))
    def _():
        project(conv_silu, conv=True)

    @pl.when((j >= b2) & (j < b3))
    def _():
        project(silu_rows)

    @pl.when((j >= b3) & (j < b4))
    def _():
        project(lambda c: raw(c) * q_scale)

    @pl.when((j >= b4) & (j < b5))
    def _():
        project(raw)

    @pl.when(j >= b5)
    def _():
        vt = _dot_nt(wvt_ref[...], u_ref[...]).astype(BF16)
        tv = vt_ref.shape[-1]
        for c in range(vt_ref.shape[0]):
            vt_ref[c] = vt[:, c * tv:(c + 1) * tv]


def _in_proj(x2, gain, w_main, w_gate, w_vt, conv_w, *, batch, dn_dim, df_dim, tm, tn, tv,
             q_scale):
    T, D = x2.shape
    N = w_main.shape[1]
    nv = w_vt.shape[0]
    S = T // batch
    assert N == 4 * dn_dim + 2 * df_dim and nv == df_dim
    assert S % tm == 0 and dn_dim % tn == 0 and df_dim % tn == 0 and tm % tv == 0
    assert tn % DN_HEAD_DIM == 0 and conv_w.shape == (CONV_WIDTH, 3 * dn_dim)
    assert tm % IN_PROJ_ROWS == 0
    n_dn, n_df = dn_dim // tn, df_dim // tn
    bounds = (n_dn, 2 * n_dn, 3 * n_dn, 4 * n_dn, 4 * n_dn + n_df, 4 * n_dn + 2 * n_df)
    n_main = bounds[-1]
    n_conv = bounds[2]
    per_seq = S // tm
    return pl.pallas_call(
        functools.partial(_in_proj_kernel, bounds=bounds, per_seq=per_seq, q_scale=q_scale),
        grid=(T // tm, n_main + nv // tn),
        in_specs=[
            pl.BlockSpec((tm, D), lambda i, j: (i, 0)),
            pl.BlockSpec((1, D), lambda i, j: (0, 0)),
            pl.BlockSpec((D, tn), lambda i, j: (0, jnp.minimum(j, n_main - 1))),
            pl.BlockSpec((D, GATE_COLS), lambda i, j: (0, 0)),
            pl.BlockSpec((tn, D), lambda i, j: (jnp.maximum(j - n_main, 0), 0)),
            pl.BlockSpec((CONV_WIDTH, tn), lambda i, j: (0, jnp.minimum(j, n_conv - 1))),
        ],
        out_specs=[
            pl.BlockSpec((tm, tn), lambda i, j: (i, jnp.minimum(j, n_main - 1))),
            pl.BlockSpec((tm, GATE_COLS), lambda i, j: (i, 0)),
            pl.BlockSpec((None, tm // tv, tn, tv),
                         lambda i, j: (i // per_seq, i % per_seq, jnp.maximum(j - n_main, 0), 0)),
        ],
        out_shape=[
            jax.ShapeDtypeStruct((T, N), BF16),
            jax.ShapeDtypeStruct((T, GATE_COLS), F32),
            jax.ShapeDtypeStruct((batch, S // tv, nv, tv), BF16),
        ],
        scratch_shapes=[
            pltpu.VMEM((tm, D), BF16),
            pltpu.VMEM((n_conv, 8, tn), F32),
            pltpu.VMEM((8 + tm, tn), F32),
        ],
        compiler_params=pltpu.CompilerParams(
            dimension_semantics=("arbitrary", "arbitrary"),
            vmem_limit_bytes=V7X_VMEM_LIMIT),
        name="in_proj",
    )(x2, gain, w_main, w_gate, w_vt, conv_w)


def _gdn_kernel(q_ref, k_ref, v_ref, z_ref, gate_ref, alog_ref, dtb_ref, dnn_ref,
                y_ref, state_ref, *, blk, n_heads):
    dk = DN_HEAD_DIM
    n_chunks = blk // CHUNK

    @pl.when(pl.program_id(1) == 0)
    def _():
        state_ref[...] = jnp.zeros_like(state_ref)

    gate = gate_ref[...]
    beta_all = _sigmoid(gate)
    xs = gate + dtb_ref[...]
    softplus = jnp.maximum(xs, 0.0) + jnp.log(1.0 + jnp.exp(-jnp.abs(xs)))
    g_all = -jnp.exp(alog_ref[...]) * softplus
    pos = lax.broadcasted_iota(jnp.int32, g_all.shape, 0) & (CHUNK - 1)
    gc_all = g_all
    step = 1
    while step < CHUNK:
        gc_all = gc_all + jnp.where(pos >= step, pltpu.roll(gc_all, step, 0), 0.0)
        step *= 2

    ri = lax.broadcasted_iota(jnp.int32, (CHUNK, CHUNK), 0)
    ci = lax.broadcasted_iota(jnp.int32, (CHUNK, CHUNK), 1)
    tril = ri >= ci
    strict = ri > ci
    eye = ri == ci
    gain = dnn_ref[...]

    def first_stage(c):
        rows = slice(c * CHUNK, (c + 1) * CHUNK)
        items = []
        for h in range(n_heads):
            cols = slice(h * dk, (h + 1) * dk)
            q_bf = q_ref[rows, cols]
            k_bf = k_ref[rows, cols]
            qc = q_bf.astype(F32)
            kc = k_bf.astype(F32)
            vc = v_ref[rows, cols].astype(F32)
            bc = jnp.broadcast_to(beta_all[rows, h:h + 1], (CHUNK, dk))
            gcc = jnp.broadcast_to(gc_all[rows, n_heads + h:n_heads + h + 1], (CHUNK, dk))
            g_last = gcc[CHUNK - 1:CHUNK, :]
            eg = jnp.exp(gcc)
            g_sq = gcc[:, 0:CHUNK]
            g_row = jnp.sum(jnp.where(eye, g_sq, 0.0), axis=0, keepdims=True)
            decay = jnp.exp(jnp.where(tril, g_sq - g_row, -jnp.inf))
            kb = kc * bc
            lhs = jnp.concatenate([kb.astype(BF16), q_bf], axis=0)
            aq = _dot_nt(lhs, k_bf)
            n_mat = jnp.where(strict, -(aq[:CHUNK] * decay), 0.0)
            items.append(dict(
                h=h, qe=qc * eg, g_last=g_last,
                rhs=jnp.concatenate([kb * eg, vc * bc], axis=1).astype(BF16),
                qk=jnp.where(tril, aq[CHUNK:] * decay, 0.0).astype(BF16),
                kd_t=(kc * jnp.exp(g_last - gcc)).T.astype(BF16),
                power=n_mat, inv=jnp.where(eye, 1.0, 0.0) + n_mat))
        return items

    def matrix_stages(items):
        span = 2
        while span < CHUNK:
            for s in items:
                pb = s["power"].astype(BF16)
                s["power"] = _dot(pb, pb)
            for s in items:
                s["inv"] = s["inv"] + _dot(s["inv"].astype(BF16), s["power"].astype(BF16))
            span *= 2
        for s in items:
            s["wu"] = _dot(s["inv"].astype(BF16), s["rhs"]).astype(BF16)
        for s in items:
            s["gr"] = _dot(s["kd_t"], s["wu"])
            qw = _dot(s["qk"], s["wu"])
            s["q_eff"] = (s["qe"] - qw[:, :dk]).astype(BF16)
            s["p_loc"] = qw[:, dk:]

    def state_stage(c, items):
        rows = slice(c * CHUNK, (c + 1) * CHUNK)
        for s in items:
            h = s["h"]
            state = state_ref[h]
            s_bf = state.astype(BF16)
            o = _dot(s["q_eff"], s_bf) + s["p_loc"]
            state_ref[h] = (state * jnp.exp(s["g_last"])
                            - _dot(s["gr"][:, :dk].astype(BF16), s_bf) + s["gr"][:, dk:])
            zs = z_ref[rows, h * dk:(h + 1) * dk].astype(F32)
            o = o * lax.rsqrt(jnp.mean(o * o, axis=-1, keepdims=True) + 1e-6) * gain
            y_ref[rows, h * dk:(h + 1) * dk] = (o * zs).astype(y_ref.dtype)

    groups = [list(range(g, min(g + GDN_GROUP, n_chunks))) for g in range(0, n_chunks, GDN_GROUP)]
    cur = [first_stage(c) for c in groups[0]]
    for gi, chunk_ids in enumerate(groups):
        nxt = [first_stage(c) for c in groups[gi + 1]] if gi + 1 < len(groups) else None
        matrix_stages([s for items in cur for s in items])
        for c, items in zip(chunk_ids, cur):
            state_stage(c, items)
        cur = nxt


def _gdn(proj3, gate3, alog_row, dtb_row, dn_norm, *, n_heads, blk):
    B, S, _ = proj3.shape
    dk = DN_HEAD_DIM
    dn = n_heads * dk

    def group(idx):
        return pl.BlockSpec((None, blk, dn), lambda b, t: (b, t, idx))

    def whole(arr):
        return pl.BlockSpec(arr.shape, lambda b, t: (0,) * arr.ndim)

    return pl.pallas_call(
        functools.partial(_gdn_kernel, blk=blk, n_heads=n_heads),
        grid=(B, S // blk),
        in_specs=[
            group(0), group(1), group(2), group(3),
            pl.BlockSpec((None, blk, GATE_COLS), lambda b, t: (b, t, 0)),
            whole(alog_row), whole(dtb_row), whole(dn_norm),
        ],
        out_specs=pl.BlockSpec((None, blk, dn), lambda b, t: (b, t, 0)),
        out_shape=jax.ShapeDtypeStruct((B, S, dn), BF16),
        scratch_shapes=[pltpu.VMEM((n_heads, dk, dk), F32)],
        compiler_params=pltpu.CompilerParams(
            dimension_semantics=("arbitrary", "arbitrary"),
            vmem_limit_bytes=V7X_VMEM_LIMIT),
        name="gdn",
    )(proj3, proj3, proj3, proj3, gate3, alog_row, dtb_row, dn_norm)


def _t5_bucket_starts():
    max_exact = NUM_BUCKETS // 2
    n = np.arange(0, MAX_DISTANCE + 1)
    nf = np.maximum(n, 1).astype(np.float32)
    large = max_exact + (np.log(nf / max_exact) / math.log(MAX_DISTANCE / max_exact)
                         * (NUM_BUCKETS - max_exact)).astype(np.int32)
    bucket = np.where(n < max_exact, n, np.minimum(large, NUM_BUCKETS - 1))
    assert bucket[MAX_DISTANCE] == NUM_BUCKETS - 1 and np.all(np.diff(bucket) >= 0)
    starts = [(0, int(bucket[0]))]
    for d in range(1, MAX_DISTANCE + 1):
        if bucket[d] != bucket[d - 1]:
            starts.append((d, int(bucket[d])))
    return starts


def _diff_attn_kernel(rb_ref, q_ref, k_ref, vt_ref, lq1_ref, lk1_ref, lq2_ref, lk2_ref, dfn_ref,
                      y_ref, bias_ref, s_ref, p_ref, m_ref, l_ref, acc_ref,
                      *, tq, n_heads, lam_init):
    h = pl.program_id(1)
    qi = pl.program_id(2)
    d = DF_HEAD_DIM
    sub = ATTN_SUB
    n_sub = tq // sub
    log2e = math.log2(math.e)

    @pl.when(qi == 0)
    def _():
        keys = lax.broadcasted_iota(jnp.int32, (sub, sub), 0)
        qrys = lax.broadcasted_iota(jnp.int32, (sub, sub), 1)
        starts = _t5_bucket_starts()
        for idx in range(2):
            dist = qrys - keys + idx * sub
            tile = jnp.full((sub, sub), rb_ref[starts[0][1] * n_heads + h], F32)
            for first, bucket in starts[1:]:
                tile = jnp.where(dist >= first, rb_ref[bucket * n_heads + h], tile)
            tile = tile * log2e
            if idx == 0:
                tile = jnp.where(dist >= 0, tile, -jnp.inf)
            bias_ref[idx] = tile
        bias_ref[2] = jnp.full((sub, sub), rb_ref[(NUM_BUCKETS - 1) * n_heads + h] * log2e, F32)
        bias_ref[3] = jnp.full((sub, sub), -jnp.inf, F32)

    m_ref[...] = jnp.full(m_ref.shape, -jnp.inf, F32)
    l_ref[...] = jnp.zeros_like(l_ref)
    acc_ref[...] = jnp.zeros_like(acc_ref)

    chains = [(qb, m) for qb in range(n_sub) for m in range(2)]

    far_shift = rb_ref[(NUM_BUCKETS - 1) * n_heads + h] * log2e

    def produce(j, c):
        qb, m = chains[c]
        k0 = pl.multiple_of(j * tq, tq)
        s_ref[c] = _dot_nt(k_ref[pl.ds(k0, tq), m * d:(m + 1) * d],
                           q_ref[qb * sub:(qb + 1) * sub, m * d:(m + 1) * d])

    def softmax(j, c, far):
        qb, m = chains[c]
        if far:
            s = s_ref[c]
            shift = far_shift
        else:
            parts = []
            for kb in range(n_sub):
                off = (qi - j) * n_sub + (qb - kb)
                idx = jnp.where(off < 0, 3, jnp.minimum(off, 2))
                parts.append(s_ref[c, kb * sub:(kb + 1) * sub, :] + bias_ref[idx])
            s = jnp.concatenate(parts, axis=0)
            shift = 0.0
        cols = slice(qb * sub, (qb + 1) * sub)
        m_prev = m_ref[m, :, cols]
        m_new = jnp.maximum(m_prev, jnp.max(s, axis=0, keepdims=True) + shift)
        alpha = jnp.exp2(m_prev - m_new)
        pr = jnp.exp2(s - (m_new - shift))
        l_ref[m, :, cols] = alpha * l_ref[m, :, cols] + jnp.sum(pr, axis=0, keepdims=True)
        acc_ref[m, :, cols] = alpha * acc_ref[m, :, cols]
        m_ref[m, :, cols] = m_new
        return pr.astype(BF16)

    def add_values(j, c, p):
        qb, m = chains[c]
        cols = slice(qb * sub, (qb + 1) * sub)
        acc_ref[m, :, cols] += _dot(vt_ref[j], p)

    n_chains = len(chains)
    for c in range(n_chains):
        produce(0, c)
    p_ref[...] = softmax(0, 0, False)

    def trip(j, far):
        add_values(j, 0, p_ref[...])
        produce(j + 1, 0)
        for c in range(1, n_chains):
            add_values(j, c, softmax(j, c, far))
            produce(j + 1, c)
        p_ref[...] = softmax(j + 1, 0, far)

    n_far = jnp.maximum(qi - 2, 0)
    lax.fori_loop(0, n_far, lambda j, carry: (trip(j, True), carry)[1], 0)
    lax.fori_loop(n_far, qi, lambda j, carry: (trip(j, False), carry)[1], 0)
    add_values(qi, 0, p_ref[...])
    for c in range(1, n_chains):
        add_values(qi, c, softmax(qi, c, False))

    lam = (jnp.exp(jnp.sum(lq1_ref[...] * lk1_ref[...], axis=-1, keepdims=True))
           - jnp.exp(jnp.sum(lq2_ref[...] * lk2_ref[...], axis=-1, keepdims=True))
           + lam_init)
    o = acc_ref[0] * (1.0 / l_ref[0]) - acc_ref[1] * (lam / l_ref[1])
    o = o * lax.rsqrt(jnp.mean(o * o, axis=0, keepdims=True) + 1e-5) * dfn_ref[...]
    y_ref[...] = (o * (1.0 - lam_init)).T.astype(y_ref.dtype)


def _diff_attn(proj3, v_t, rel_bias, lq1, lk1, lq2, lk2, df_norm_col, *, n_heads, col0, tq,
               lam_init):
    B, S, _ = proj3.shape
    d2 = 2 * DF_HEAD_DIM
    assert tq % ATTN_SUB == 0 and ATTN_SUB >= MAX_DISTANCE and S % tq == 0
    nq = S // tq
    n_chains = 2 * (tq // ATTN_SUB)
    cb = col0 // d2
    vec = lambda n: pl.BlockSpec((1, n), lambda b, h, i: (0, 0))
    return pl.pallas_call(
        functools.partial(_diff_attn_kernel, tq=tq, n_heads=n_heads, lam_init=lam_init),
        grid=(B, n_heads, nq),
        in_specs=[
            pl.BlockSpec(memory_space=pltpu.SMEM),
            pl.BlockSpec((None, tq, d2), lambda b, h, i: (b, i, cb + h)),
            pl.BlockSpec((None, S, d2), lambda b, h, i: (b, 0, cb + n_heads + h)),
            pl.BlockSpec((None, nq, d2, tq), lambda b, h, i: (b, 0, h, 0)),
            vec(DF_HEAD_DIM), vec(DF_HEAD_DIM), vec(DF_HEAD_DIM), vec(DF_HEAD_DIM),
            pl.BlockSpec((d2, 1), lambda b, h, i: (0, 0)),
        ],
        out_specs=pl.BlockSpec((None, tq, d2), lambda b, h, i: (b, i, h)),
        out_shape=jax.ShapeDtypeStruct((B, S, n_heads * d2), BF16),
        scratch_shapes=[
            pltpu.VMEM((4, ATTN_SUB, ATTN_SUB), F32),
            pltpu.VMEM((n_chains, tq, ATTN_SUB), F32),
            pltpu.VMEM((tq, ATTN_SUB), BF16),
            pltpu.VMEM((2, 1, tq), F32),
            pltpu.VMEM((2, 1, tq), F32),
            pltpu.VMEM((2, d2, tq), F32),
        ],
        compiler_params=pltpu.CompilerParams(
            dimension_semantics=("arbitrary", "arbitrary", "arbitrary"),
            vmem_limit_bytes=V7X_VMEM_LIMIT),
        name="diff_attn",
    )(rel_bias, proj3, proj3, v_t, lq1, lk1, lq2, lk2, df_norm_col)


def _out_proj_kernel(x_ref, ya_ref, yb_ref, wa_ref, wb_ref, h_ref):
    h_ref[...] = x_ref[...] + _dot(ya_ref[...], wa_ref[...]) + _dot(yb_ref[...], wb_ref[...])


def _out_proj(x2, y_dn, y_df, w_o, *, tm, tn):
    T, D = x2.shape
    ka = y_dn.shape[1]
    kb = y_df.shape[1]
    assert ka == kb
    return pl.pallas_call(
        _out_proj_kernel,
        grid=(T // tm, D // tn),
        in_specs=[
            pl.BlockSpec((tm, tn), lambda i, j: (i, j)),
            pl.BlockSpec((tm, ka), lambda i, j: (i, 0)),
            pl.BlockSpec((tm, kb), lambda i, j: (i, 0)),
            pl.BlockSpec((ka, tn), lambda i, j: (0, j)),
            pl.BlockSpec((kb, tn), lambda i, j: (1, j)),
        ],
        out_specs=pl.BlockSpec((tm, tn), lambda i, j: (i, j)),
        out_shape=jax.ShapeDtypeStruct((T, D), F32),
        compiler_params=pltpu.CompilerParams(
            dimension_semantics=("arbitrary", "arbitrary"),
            vmem_limit_bytes=V7X_VMEM_LIMIT),
        name="out_proj",
    )(x2, y_dn, y_df, w_o, w_o)


def _mlp_kernel(h_ref, g_ref, wu_ref, wd_ref, gf_ref, o_ref, u_ref, acc_ref):
    f = pl.program_id(1)

    @pl.when(f == 0)
    def _():
        x = h_ref[...]
        ms = jnp.mean(x * x, axis=-1, keepdims=True)
        u_ref[...] = (x * lax.rsqrt(ms + 1e-6) * g_ref[...]).astype(BF16)
        acc_ref[...] = jnp.zeros_like(acc_ref)

    hid = jnp.maximum(_dot(u_ref[...], wu_ref[...]), 0.0)
    acc_ref[...] += _dot((hid * hid).astype(BF16), wd_ref[...])

    @pl.when(f == pl.num_programs(1) - 1)
    def _():
        y = h_ref[...] + acc_ref[...]
        ms = jnp.mean(y * y, axis=-1, keepdims=True)
        o_ref[...] = y * lax.rsqrt(ms + 1e-6) * gf_ref[...]


def _mlp(h1, gain, w_up, w_down, final_gain, *, tm, tf):
    T, D = h1.shape
    Fdim = w_up.shape[1]
    return pl.pallas_call(
        _mlp_kernel,
        grid=(T // tm, Fdim // tf),
        in_specs=[
            pl.BlockSpec((tm, D), lambda i, f: (i, 0)),
            pl.BlockSpec((1, D), lambda i, f: (0, 0)),
            pl.BlockSpec((D, tf), lambda i, f: (0, f)),
            pl.BlockSpec((tf, D), lambda i, f: (f, 0)),
            pl.BlockSpec((1, D), lambda i, f: (0, 0)),
        ],
        out_specs=pl.BlockSpec((tm, D), lambda i, f: (i, 0)),
        out_shape=jax.ShapeDtypeStruct((T, D), F32),
        scratch_shapes=[pltpu.VMEM((tm, D), BF16), pltpu.VMEM((tm, D), F32)],
        compiler_params=pltpu.CompilerParams(
            dimension_semantics=("arbitrary", "arbitrary"),
            vmem_limit_bytes=V7X_VMEM_LIMIT),
        name="mlp",
    )(h1, gain, w_up, w_down, final_gain)


def _tile(n, pref):
    if n <= pref:
        return n
    t = pref - pref % 128
    while t > 128 and n % t:
        t -= 128
    assert n % t == 0
    return t


def kernel(x, attn_norm, w_in, conv_w, a_log, dt_bias, dn_norm, lambda_q1, lambda_k1,
           lambda_q2, lambda_k2, df_norm, rel_bias, w_o, mlp_norm, w_up, w_down, final_norm):
    B, S, D = x.shape
    depth = attn_norm.shape[0]
    n_dn = a_log.shape[1]
    n_df = rel_bias.shape[1]
    dn_dim = n_dn * DN_HEAD_DIM
    df_dim = n_df * 2 * DF_HEAD_DIM
    T = B * S
    gate0 = 4 * dn_dim
    assert w_in.shape[2] == gate0 + 2 * n_dn + 3 * df_dim and 2 * n_dn <= GATE_COLS

    assert depth == 1
    l = 0
    h = x.reshape(T, D)

    wl = w_in[l]
    dfq0 = gate0 + 2 * n_dn
    dfv0 = dfq0 + 2 * df_dim
    w_main = jnp.concatenate([wl[:, :gate0], wl[:, dfq0:dfv0]], axis=1).astype(BF16)
    w_gate = jnp.pad(wl[:, gate0:dfq0], ((0, 0), (0, GATE_COLS - 2 * n_dn))).astype(BF16)
    w_vt = wl[:, dfv0:].T.astype(BF16)
    w_o_bf = w_o[l].astype(BF16)
    w_up_bf = w_up[l].astype(BF16)
    w_down_bf = w_down[l].astype(BF16)

    proj, gates, v_t = _in_proj(h, attn_norm[l][None, :], w_main, w_gate, w_vt, conv_w[l],
                                batch=B, dn_dim=dn_dim, df_dim=df_dim, tm=_tile(S, 1024),
                                tn=_tile(math.gcd(dn_dim, df_dim), IN_PROJ_COLS),
                                tv=_tile(S, ATTN_BLOCK),
                                q_scale=DF_HEAD_DIM ** -0.5 * math.log2(math.e))
    proj3 = proj.reshape(B, S, -1)
    gate3 = gates.reshape(B, S, GATE_COLS)

    gate_pad = ((0, 0), (n_dn, GATE_COLS - 2 * n_dn))
    alog_row = jnp.pad(a_log[l][None, :], gate_pad)
    dtb_row = jnp.pad(dt_bias[l][None, :], gate_pad)
    y_dn = _gdn(proj3, gate3, alog_row, dtb_row, dn_norm[l][None, :],
                n_heads=n_dn, blk=_tile(S, GDN_BLOCK))
    lam_init = 0.8 - 0.6 * math.exp(-0.3 * l)
    y_df = _diff_attn(proj3, v_t, rel_bias.reshape(-1),
                      lambda_q1[l][None, :], lambda_k1[l][None, :],
                      lambda_q2[l][None, :], lambda_k2[l][None, :], df_norm[l][:, None],
                      n_heads=n_df, col0=gate0, tq=_tile(S, ATTN_BLOCK), lam_init=lam_init)

    h1 = _out_proj(h, y_dn.reshape(T, dn_dim), y_df.reshape(T, df_dim), w_o_bf,
                   tm=_tile(T, 1024), tn=_tile(D, 1024))
    out = _mlp(h1, mlp_norm[l][None, :], w_up_bf, w_down_bf, final_norm[None, :],
               tm=_tile(T, 512), tf=_tile(w_up_bf.shape[1], 1024))
    return out.reshape(B, S, D)
```

```python
import functools
import math

import numpy as np
import jax
import jax.numpy as jnp
from jax import lax
from jax.experimental import pallas as pl
from jax.experimental.pallas import tpu as pltpu

F32 = jnp.float32
BF16 = jnp.bfloat16

DN_HEAD_DIM = 128
DF_HEAD_DIM = 128
CONV_WIDTH = 4
CHUNK = 64
NUM_BUCKETS = 32
MAX_DISTANCE = 128
GATE_COLS = 128
IN_PROJ_ROWS = 512
IN_PROJ_COLS = 1024
IN_PROJ_SUB = 256
GDN_BLOCK = 512
GDN_GROUP = 4
ATTN_SUB = 256
ATTN_BLOCK = 512

V7X_VMEM_LIMIT = 56 * 1024 * 1024


def _dot(a, b):
    return jnp.dot(a, b, preferred_element_type=F32)


def _dot_nt(a, b):
    return lax.dot_general(a, b, (((1,), (1,)), ((), ())), preferred_element_type=F32)


def _sigmoid(x):
    return 1.0 / (1.0 + jnp.exp(-x))


def _in_proj_kernel(x_ref, g_ref, w_ref, wg_ref, wvt_ref, cw_ref, proj_ref, gate_ref, vt_ref,
                    u_ref, hist_ref, cbuf_ref, *, bounds, per_seq, q_scale):
    i = pl.program_id(0)
    j = pl.program_id(1)
    tm, tn = proj_ref.shape
    b0, b1, b2, b3, b4, b5 = bounds

    @pl.when(j == 0)
    def _():
        x = x_ref[...]
        ms = jnp.mean(x * x, axis=-1, keepdims=True)
        u = (x * lax.rsqrt(ms + 1e-6) * g_ref[...]).astype(BF16)
        u_ref[...] = u
        gate_ref[...] = _dot(u, wg_ref[...])

    n_col_chunks = tn // IN_PROJ_SUB
    first_of_seq = (i % per_seq) == 0

    def chunk_cols(c):
        return slice(c * IN_PROJ_SUB, (c + 1) * IN_PROJ_SUB)

    def raw_cols(c):
        cbuf_ref[8:8 + tm, chunk_cols(c)] = _dot(u_ref[...], w_ref[:, chunk_cols(c)])

    def raw(c):
        return cbuf_ref[8:8 + tm, chunk_cols(c)]

    def conv_silu(c):
        cw = cw_ref[:, chunk_cols(c)]
        y = None
        for s in range(CONV_WIDTH):
            tap = CONV_WIDTH - 1 - s
            term = cbuf_ref[8 - s:8 - s + tm, chunk_cols(c)] * cw[tap:tap + 1, :]
            y = term if y is None else y + term
        return y * _sigmoid(y)

    def l2norm_heads(y, scale):
        outs = []
        for h in range(IN_PROJ_SUB // DN_HEAD_DIM):
            yh = y[:, h * DN_HEAD_DIM:(h + 1) * DN_HEAD_DIM]
            outs.append(yh * (lax.rsqrt(jnp.sum(yh * yh, axis=-1, keepdims=True) + 1e-6) * scale))
        return jnp.concatenate(outs, axis=1)

    def silu_cols(c):
        z = raw(c)
        return z * _sigmoid(z)

    def project(epilogue, conv=False):
        if conv:
            cbuf_ref[0:8, :] = jnp.where(first_of_seq, 0.0, hist_ref[j])
        raw_cols(0)
        for c in range(n_col_chunks):
            if c + 1 < n_col_chunks:
                raw_cols(c + 1)
            proj_ref[:, chunk_cols(c)] = epilogue(c).astype(BF16)
        if conv:
            hist_ref[j] = cbuf_ref[tm:tm + 8, :]

    @pl.when(j < b0)
    def _():
        project(lambda c: l2norm_heads(conv_silu(c), DN_HEAD_DIM ** -0.5), conv=True)

    @pl.when((j >= b0) & (j < b1))
    def _():
        project(lambda c: l2norm_heads(conv_silu(c), 1.0), conv=True)

    @pl.when((j >= b1) & (j < b2))
    def _():
        project(conv_silu, conv=True)

    @pl.when((j >= b2) & (j < b3))
    def _():
        project(silu_cols)

    @pl.when((j >= b3) & (j < b4))
    def _():
        project(lambda c: raw(c) * q_scale)

    @pl.when((j >= b4) & (j < b5))
    def _():
        project(raw)

    @pl.when(j >= b5)
    def _():
        vt = _dot_nt(wvt_ref[...], u_ref[...]).astype(BF16)
        tv = vt_ref.shape[-1]
        for c in range(vt_ref.shape[0]):
            vt_ref[c] = vt[:, c * tv:(c + 1) * tv]


def _in_proj(x2, gain, w_main, w_gate, w_vt, conv_w, *, batch, dn_dim, df_dim, tm, tn, tv,
             q_scale):
    T, D = x2.shape
    N = w_main.shape[1]
    nv = w_vt.shape[0]
    S = T // batch
    assert N == 4 * dn_dim + 2 * df_dim and nv == df_dim
    assert S % tm == 0 and dn_dim % tn == 0 and df_dim % tn == 0 and tm % tv == 0
    assert tn % DN_HEAD_DIM == 0 and conv_w.shape == (CONV_WIDTH, 3 * dn_dim)
    assert tn % IN_PROJ_SUB == 0 and IN_PROJ_SUB % DN_HEAD_DIM == 0
    n_dn, n_df = dn_dim // tn, df_dim // tn
    bounds = (n_dn, 2 * n_dn, 3 * n_dn, 4 * n_dn, 4 * n_dn + n_df, 4 * n_dn + 2 * n_df)
    n_main = bounds[-1]
    n_conv = bounds[2]
    per_seq = S // tm
    return pl.pallas_call(
        functools.partial(_in_proj_kernel, bounds=bounds, per_seq=per_seq, q_scale=q_scale),
        grid=(T // tm, n_main + nv // tn),
        in_specs=[
            pl.BlockSpec((tm, D), lambda i, j: (i, 0)),
            pl.BlockSpec((1, D), lambda i, j: (0, 0)),
            pl.BlockSpec((D, tn), lambda i, j: (0, jnp.minimum(j, n_main - 1))),
            pl.BlockSpec((D, GATE_COLS), lambda i, j: (0, 0)),
            pl.BlockSpec((tn, D), lambda i, j: (jnp.maximum(j - n_main, 0), 0)),
            pl.BlockSpec((CONV_WIDTH, tn), lambda i, j: (0, jnp.minimum(j, n_conv - 1))),
        ],
        out_specs=[
            pl.BlockSpec((tm, tn), lambda i, j: (i, jnp.minimum(j, n_main - 1))),
            pl.BlockSpec((tm, GATE_COLS), lambda i, j: (i, 0)),
            pl.BlockSpec((None, tm // tv, tn, tv),
                         lambda i, j: (i // per_seq, i % per_seq, jnp.maximum(j - n_main, 0), 0)),
        ],
        out_shape=[
            jax.ShapeDtypeStruct((T, N), BF16),
            jax.ShapeDtypeStruct((T, GATE_COLS), F32),
            jax.ShapeDtypeStruct((batch, S // tv, nv, tv), BF16),
        ],
        scratch_shapes=[
            pltpu.VMEM((tm, D), BF16),
            pltpu.VMEM((n_conv, 8, tn), F32),
            pltpu.VMEM((8 + tm, tn), F32),
        ],
        compiler_params=pltpu.CompilerParams(
            dimension_semantics=("arbitrary", "arbitrary"),
            vmem_limit_bytes=V7X_VMEM_LIMIT),
        name="in_proj",
    )(x2, gain, w_main, w_gate, w_vt, conv_w)


def _gdn_kernel(q_ref, k_ref, v_ref, z_ref, gate_ref, alog_ref, dtb_ref, dnn_ref,
                y_ref, state_ref, *, blk, n_heads):
    dk = DN_HEAD_DIM
    n_chunks = blk // CHUNK

    @pl.when(pl.program_id(1) == 0)
    def _():
        state_ref[...] = jnp.zeros_like(state_ref)

    gate = gate_ref[...]
    beta_all = _sigmoid(gate)
    xs = gate + dtb_ref[...]
    softplus = jnp.maximum(xs, 0.0) + jnp.log(1.0 + jnp.exp(-jnp.abs(xs)))
    g_all = -jnp.exp(alog_ref[...]) * softplus
    pos = lax.broadcasted_iota(jnp.int32, g_all.shape, 0) & (CHUNK - 1)
    gc_all = g_all
    step = 1
    while step < CHUNK:
        gc_all = gc_all + jnp.where(pos >= step, pltpu.roll(gc_all, step, 0), 0.0)
        step *= 2

    ri = lax.broadcasted_iota(jnp.int32, (CHUNK, CHUNK), 0)
    ci = lax.broadcasted_iota(jnp.int32, (CHUNK, CHUNK), 1)
    tril = ri >= ci
    strict = ri > ci
    eye = ri == ci
    gain = dnn_ref[...]

    def first_stage(c):
        rows = slice(c * CHUNK, (c + 1) * CHUNK)
        items = []
        for h in range(n_heads):
            cols = slice(h * dk, (h + 1) * dk)
            q_bf = q_ref[rows, cols]
            k_bf = k_ref[rows, cols]
            qc = q_bf.astype(F32)
            kc = k_bf.astype(F32)
            vc = v_ref[rows, cols].astype(F32)
            bc = jnp.broadcast_to(beta_all[rows, h:h + 1], (CHUNK, dk))
            gcc = jnp.broadcast_to(gc_all[rows, n_heads + h:n_heads + h + 1], (CHUNK, dk))
            g_last = gcc[CHUNK - 1:CHUNK, :]
            eg = jnp.exp(gcc)
            g_sq = gcc[:, 0:CHUNK]
            g_row = jnp.sum(jnp.where(eye, g_sq, 0.0), axis=0, keepdims=True)
            decay = jnp.exp(jnp.where(tril, g_sq - g_row, -jnp.inf))
            kb = kc * bc
            lhs = jnp.concatenate([kb.astype(BF16), q_bf], axis=0)
            aq = _dot_nt(lhs, k_bf)
            n_mat = jnp.where(strict, -(aq[:CHUNK] * decay), 0.0)
            items.append(dict(
                h=h, qe=qc * eg, g_last=g_last,
                rhs=jnp.concatenate([kb * eg, vc * bc], axis=1).astype(BF16),
                qk=jnp.where(tril, aq[CHUNK:] * decay, 0.0).astype(BF16),
                kd_t=(kc * jnp.exp(g_last - gcc)).T.astype(BF16),
                power=n_mat, inv=jnp.where(eye, 1.0, 0.0) + n_mat))
        return items

    def matrix_stages(items):
        span = 2
        while span < CHUNK:
            for s in items:
                pb = s["power"].astype(BF16)
                s["power"] = _dot(pb, pb)
            for s in items:
                s["inv"] = s["inv"] + _dot(s["inv"].astype(BF16), s["power"].astype(BF16))
            span *= 2
        for s in items:
            s["wu"] = _dot(s["inv"].astype(BF16), s["rhs"]).astype(BF16)
        for s in items:
            s["gr"] = _dot(s["kd_t"], s["wu"])
            qw = _dot(s["qk"], s["wu"])
            s["q_eff"] = (s["qe"] - qw[:, :dk]).astype(BF16)
            s["p_loc"] = qw[:, dk:]

    def state_stage(c, items):
        rows = slice(c * CHUNK, (c + 1) * CHUNK)
        for s in items:
            h = s["h"]
            state = state_ref[h]
            s_bf = state.astype(BF16)
            o = _dot(s["q_eff"], s_bf) + s["p_loc"]
            state_ref[h] = (state * jnp.exp(s["g_last"])
                            - _dot(s["gr"][:, :dk].astype(BF16), s_bf) + s["gr"][:, dk:])
            zs = z_ref[rows, h * dk:(h + 1) * dk].astype(F32)
            o = o * lax.rsqrt(jnp.mean(o * o, axis=-1, keepdims=True) + 1e-6) * gain
            y_ref[rows, h * dk:(h + 1) * dk] = (o * zs).astype(y_ref.dtype)

    groups = [list(range(g, min(g + GDN_GROUP, n_chunks))) for g in range(0, n_chunks, GDN_GROUP)]
    cur = [first_stage(c) for c in groups[0]]
    for gi, chunk_ids in enumerate(groups):
        nxt = [first_stage(c) for c in groups[gi + 1]] if gi + 1 < len(groups) else None
        matrix_stages([s for items in cur for s in items])
        for c, items in zip(chunk_ids, cur):
            state_stage(c, items)
        cur = nxt


def _gdn(proj3, gate3, alog_row, dtb_row, dn_norm, *, n_heads, blk):
    B, S, _ = proj3.shape
    dk = DN_HEAD_DIM
    dn = n_heads * dk

    def group(idx):
        return pl.BlockSpec((None, blk, dn), lambda b, t: (b, t, idx))

    def whole(arr):
        return pl.BlockSpec(arr.shape, lambda b, t: (0,) * arr.ndim)

    return pl.pallas_call(
        functools.partial(_gdn_kernel, blk=blk, n_heads=n_heads),
        grid=(B, S // blk),
        in_specs=[
            group(0), group(1), group(2), group(3),
            pl.BlockSpec((None, blk, GATE_COLS), lambda b, t: (b, t, 0)),
            whole(alog_row), whole(dtb_row), whole(dn_norm),
        ],
        out_specs=pl.BlockSpec((None, blk, dn), lambda b, t: (b, t, 0)),
        out_shape=jax.ShapeDtypeStruct((B, S, dn), BF16),
        scratch_shapes=[pltpu.VMEM((n_heads, dk, dk), F32)],
        compiler_params=pltpu.CompilerParams(
            dimension_semantics=("arbitrary", "arbitrary"),
            vmem_limit_bytes=V7X_VMEM_LIMIT),
        name="gdn",
    )(proj3, proj3, proj3, proj3, gate3, alog_row, dtb_row, dn_norm)


def _t5_bucket_starts():
    max_exact = NUM_BUCKETS // 2
    n = np.arange(0, MAX_DISTANCE + 1)
    nf = np.maximum(n, 1).astype(np.float32)
    large = max_exact + (np.log(nf / max_exact) / math.log(MAX_DISTANCE / max_exact)
                         * (NUM_BUCKETS - max_exact)).astype(np.int32)
    bucket = np.where(n < max_exact, n, np.minimum(large, NUM_BUCKETS - 1))
    assert bucket[MAX_DISTANCE] == NUM_BUCKETS - 1 and np.all(np.diff(bucket) >= 0)
    starts = [(0, int(bucket[0]))]
    for d in range(1, MAX_DISTANCE + 1):
        if bucket[d] != bucket[d - 1]:
            starts.append((d, int(bucket[d])))
    return starts


def _diff_attn_kernel(rb_ref, q_ref, k_ref, vt_ref, lq1_ref, lk1_ref, lq2_ref, lk2_ref, dfn_ref,
                      y_ref, bias_ref, s_ref, p_ref, m_ref, l_ref, acc_ref,
                      *, tq, n_heads, lam_init):
    h = pl.program_id(1)
    qi = pl.program_id(2)
    d = DF_HEAD_DIM
    sub = ATTN_SUB
    n_sub = tq // sub
    log2e = math.log2(math.e)

    @pl.when(qi == 0)
    def _():
        keys = lax.broadcasted_iota(jnp.int32, (sub, sub), 0)
        qrys = lax.broadcasted_iota(jnp.int32, (sub, sub), 1)
        starts = _t5_bucket_starts()
        for idx in range(2):
            dist = qrys - keys + idx * sub
            tile = jnp.full((sub, sub), rb_ref[starts[0][1] * n_heads + h], F32)
            for first, bucket in starts[1:]:
                tile = jnp.where(dist >= first, rb_ref[bucket * n_heads + h], tile)
            tile = tile * log2e
            if idx == 0:
                tile = jnp.where(dist >= 0, tile, -jnp.inf)
            bias_ref[idx] = tile
        bias_ref[2] = jnp.full((sub, sub), rb_ref[(NUM_BUCKETS - 1) * n_heads + h] * log2e, F32)
        bias_ref[3] = jnp.full((sub, sub), -jnp.inf, F32)

    m_ref[...] = jnp.full(m_ref.shape, -jnp.inf, F32)
    l_ref[...] = jnp.zeros_like(l_ref)
    acc_ref[...] = jnp.zeros_like(acc_ref)

    chains = [(qb, m) for qb in range(n_sub) for m in range(2)]

    far_shift = rb_ref[(NUM_BUCKETS - 1) * n_heads + h] * log2e

    def produce(j, c):
        qb, m = chains[c]
        k0 = pl.multiple_of(j * tq, tq)
        s_ref[c] = _dot_nt(k_ref[pl.ds(k0, tq), m * d:(m + 1) * d],
                           q_ref[qb * sub:(qb + 1) * sub, m * d:(m + 1) * d])

    def softmax(j, c, far):
        qb, m = chains[c]
        if far:
            s = s_ref[c]
            shift = far_shift
        else:
            parts = []
            for kb in range(n_sub):
                off = (qi - j) * n_sub + (qb - kb)
                idx = jnp.where(off < 0, 3, jnp.minimum(off, 2))
                parts.append(s_ref[c, kb * sub:(kb + 1) * sub, :] + bias_ref[idx])
            s = jnp.concatenate(parts, axis=0)
            shift = 0.0
        cols = slice(qb * sub, (qb + 1) * sub)
        m_prev = m_ref[m, :, cols]
        m_new = jnp.maximum(m_prev, jnp.max(s, axis=0, keepdims=True) + shift)
        alpha = jnp.exp2(m_prev - m_new)
        pr = jnp.exp2(s - (m_new - shift))
        l_ref[m, :, cols] = alpha * l_ref[m, :, cols] + jnp.sum(pr, axis=0, keepdims=True)
        acc_ref[m, :, cols] = alpha * acc_ref[m, :, cols]
        m_ref[m, :, cols] = m_new
        return pr.astype(BF16)

    def add_values(j, c, p):
        qb, m = chains[c]
        cols = slice(qb * sub, (qb + 1) * sub)
        acc_ref[m, :, cols] += _dot(vt_ref[j], p)

    n_chains = len(chains)
    for c in range(n_chains):
        produce(0, c)
    p_ref[...] = softmax(0, 0, False)

    def trip(j, far):
        add_values(j, 0, p_ref[...])
        produce(j + 1, 0)
        for c in range(1, n_chains):
            add_values(j, c, softmax(j, c, far))
            produce(j + 1, c)
        p_ref[...] = softmax(j + 1, 0, far)

    n_far = jnp.maximum(qi - 2, 0)
    lax.fori_loop(0, n_far, lambda j, carry: (trip(j, True), carry)[1], 0)
    lax.fori_loop(n_far, qi, lambda j, carry: (trip(j, False), carry)[1], 0)
    add_values(qi, 0, p_ref[...])
    for c in range(1, n_chains):
        add_values(qi, c, softmax(qi, c, False))

    lam = (jnp.exp(jnp.sum(lq1_ref[...] * lk1_ref[...], axis=-1, keepdims=True))
           - jnp.exp(jnp.sum(lq2_ref[...] * lk2_ref[...], axis=-1, keepdims=True))
           + lam_init)
    o = acc_ref[0] * (1.0 / l_ref[0]) - acc_ref[1] * (lam / l_ref[1])
    o = o * lax.rsqrt(jnp.mean(o * o, axis=0, keepdims=True) + 1e-5) * dfn_ref[...]
    y_ref[...] = (o * (1.0 - lam_init)).T.astype(y_ref.dtype)


def _diff_attn(proj3, v_t, rel_bias, lq1, lk1, lq2, lk2, df_norm_col, *, n_heads, col0, tq,
               lam_init):
    B, S, _ = proj3.shape
    d2 = 2 * DF_HEAD_DIM
    assert tq % ATTN_SUB == 0 and ATTN_SUB >= MAX_DISTANCE and S % tq == 0
    nq = S // tq
    n_chains = 2 * (tq // ATTN_SUB)
    cb = col0 // d2
    vec = lambda n: pl.BlockSpec((1, n), lambda b, h, i: (0, 0))
    return pl.pallas_call(
        functools.partial(_diff_attn_kernel, tq=tq, n_heads=n_heads, lam_init=lam_init),
        grid=(B, n_heads, nq),
        in_specs=[
            pl.BlockSpec(memory_space=pltpu.SMEM),
            pl.BlockSpec((None, tq, d2), lambda b, h, i: (b, i, cb + h)),
            pl.BlockSpec((None, S, d2), lambda b, h, i: (b, 0, cb + n_heads + h)),
            pl.BlockSpec((None, nq, d2, tq), lambda b, h, i: (b, 0, h, 0)),
            vec(DF_HEAD_DIM), vec(DF_HEAD_DIM), vec(DF_HEAD_DIM), vec(DF_HEAD_DIM),
            pl.BlockSpec((d2, 1), lambda b, h, i: (0, 0)),
        ],
        out_specs=pl.BlockSpec((None, tq, d2), lambda b, h, i: (b, i, h)),
        out_shape=jax.ShapeDtypeStruct((B, S, n_heads * d2), BF16),
        scratch_shapes=[
            pltpu.VMEM((4, ATTN_SUB, ATTN_SUB), F32),
            pltpu.VMEM((n_chains, tq, ATTN_SUB), F32),
            pltpu.VMEM((tq, ATTN_SUB), BF16),
            pltpu.VMEM((2, 1, tq), F32),
            pltpu.VMEM((2, 1, tq), F32),
            pltpu.VMEM((2, d2, tq), F32),
        ],
        compiler_params=pltpu.CompilerParams(
            dimension_semantics=("arbitrary", "arbitrary", "arbitrary"),
            vmem_limit_bytes=V7X_VMEM_LIMIT),
        name="diff_attn",
    )(rel_bias, proj3, proj3, v_t, lq1, lk1, lq2, lk2, df_norm_col)


def _out_proj_kernel(x_ref, ya_ref, yb_ref, wa_ref, wb_ref, h_ref):
    h_ref[...] = x_ref[...] + _dot(ya_ref[...], wa_ref[...]) + _dot(yb_ref[...], wb_ref[...])


def _out_proj(x2, y_dn, y_df, w_o, *, tm, tn):
    T, D = x2.shape
    ka = y_dn.shape[1]
    kb = y_df.shape[1]
    assert ka == kb
    return pl.pallas_call(
        _out_proj_kernel,
        grid=(T // tm, D // tn),
        in_specs=[
            pl.BlockSpec((tm, tn), lambda i, j: (i, j)),
            pl.BlockSpec((tm, ka), lambda i, j: (i, 0)),
            pl.BlockSpec((tm, kb), lambda i, j: (i, 0)),
            pl.BlockSpec((ka, tn), lambda i, j: (0, j)),
            pl.BlockSpec((kb, tn), lambda i, j: (1, j)),
        ],
        out_specs=pl.BlockSpec((tm, tn), lambda i, j: (i, j)),
        out_shape=jax.ShapeDtypeStruct((T, D), F32),
        compiler_params=pltpu.CompilerParams(
            dimension_semantics=("arbitrary", "arbitrary"),
            vmem_limit_bytes=V7X_VMEM_LIMIT),
        name="out_proj",
    )(x2, y_dn, y_df, w_o, w_o)


def _mlp_kernel(h_ref, g_ref, wu_ref, wd_ref, gf_ref, o_ref, u_ref, acc_ref):
    f = pl.program_id(1)

    @pl.when(f == 0)
    def _():
        x = h_ref[...]
        ms = jnp.mean(x * x, axis=-1, keepdims=True)
        u_ref[...] = (x * lax.rsqrt(ms + 1e-6) * g_ref[...]).astype(BF16)
        acc_ref[...] = jnp.zeros_like(acc_ref)

    hid = jnp.maximum(_dot(u_ref[...], wu_ref[...]), 0.0)
    acc_ref[...] += _dot((hid * hid).astype(BF16), wd_ref[...])

    @pl.when(f == pl.num_programs(1) - 1)
    def _():
        y = h_ref[...] + acc_ref[...]
        ms = jnp.mean(y * y, axis=-1, keepdims=True)
        o_ref[...] = y * lax.rsqrt(ms + 1e-6) * gf_ref[...]


def _mlp(h1, gain, w_up, w_down, final_gain, *, tm, tf):
    T, D = h1.shape
    Fdim = w_up.shape[1]
    return pl.pallas_call(
        _mlp_kernel,
        grid=(T // tm, Fdim // tf),
        in_specs=[
            pl.BlockSpec((tm, D), lambda i, f: (i, 0)),
            pl.BlockSpec((1, D), lambda i, f: (0, 0)),
            pl.BlockSpec((D, tf), lambda i, f: (0, f)),
            pl.BlockSpec((tf, D), lambda i, f: (f, 0)),
            pl.BlockSpec((1, D), lambda i, f: (0, 0)),
        ],
        out_specs=pl.BlockSpec((tm, D), lambda i, f: (i, 0)),
        out_shape=jax.ShapeDtypeStruct((T, D), F32),
        scratch_shapes=[pltpu.VMEM((tm, D), BF16), pltpu.VMEM((tm, D), F32)],
        compiler_params=pltpu.CompilerParams(
            dimension_semantics=("arbitrary", "arbitrary"),
            vmem_limit_bytes=V7X_VMEM_LIMIT),
        name="mlp",
    )(h1, gain, w_up, w_down, final_gain)


def _tile(n, pref):
    if n <= pref:
        return n
    t = pref - pref % 128
    while t > 128 and n % t:
        t -= 128
    assert n % t == 0
    return t


def kernel(x, attn_norm, w_in, conv_w, a_log, dt_bias, dn_norm, lambda_q1, lambda_k1,
           lambda_q2, lambda_k2, df_norm, rel_bias, w_o, mlp_norm, w_up, w_down, final_norm):
    B, S, D = x.shape
    depth = attn_norm.shape[0]
    n_dn = a_log.shape[1]
    n_df = rel_bias.shape[1]
    dn_dim = n_dn * DN_HEAD_DIM
    df_dim = n_df * 2 * DF_HEAD_DIM
    T = B * S
    gate0 = 4 * dn_dim
    assert w_in.shape[2] == gate0 + 2 * n_dn + 3 * df_dim and 2 * n_dn <= GATE_COLS

    assert depth == 1
    l = 0
    h = x.reshape(T, D)

    wl = w_in[l]
    dfq0 = gate0 + 2 * n_dn
    dfv0 = dfq0 + 2 * df_dim
    w_main = jnp.concatenate([wl[:, :gate0], wl[:, dfq0:dfv0]], axis=1).astype(BF16)
    w_gate = jnp.pad(wl[:, gate0:dfq0], ((0, 0), (0, GATE_COLS - 2 * n_dn))).astype(BF16)
    w_vt = wl[:, dfv0:].T.astype(BF16)
    w_o_bf = w_o[l].astype(BF16)
    w_up_bf = w_up[l].astype(BF16)
    w_down_bf = w_down[l].astype(BF16)

    proj, gates, v_t = _in_proj(h, attn_norm[l][None, :], w_main, w_gate, w_vt, conv_w[l],
                                batch=B, dn_dim=dn_dim, df_dim=df_dim, tm=_tile(S, IN_PROJ_ROWS),
                                tn=_tile(math.gcd(dn_dim, df_dim), IN_PROJ_COLS),
                                tv=_tile(S, ATTN_BLOCK),
                                q_scale=DF_HEAD_DIM ** -0.5 * math.log2(math.e))
    proj3 = proj.reshape(B, S, -1)
    gate3 = gates.reshape(B, S, GATE_COLS)

    gate_pad = ((0, 0), (n_dn, GATE_COLS - 2 * n_dn))
    alog_row = jnp.pad(a_log[l][None, :], gate_pad)
    dtb_row = jnp.pad(dt_bias[l][None, :], gate_pad)
    y_dn = _gdn(proj3, gate3, alog_row, dtb_row, dn_norm[l][None, :],
                n_heads=n_dn, blk=_tile(S, GDN_BLOCK))
    lam_init = 0.8 - 0.6 * math.exp(-0.3 * l)
    y_df = _diff_attn(proj3, v_t, rel_bias.reshape(-1),
                      lambda_q1[l][None, :], lambda_k1[l][None, :],
                      lambda_q2[l][None, :], lambda_k2[l][None, :], df_norm[l][:, None],
                      n_heads=n_df, col0=gate0, tq=_tile(S, ATTN_BLOCK), lam_init=lam_init)

    h1 = _out_proj(h, y_dn.reshape(T, dn_dim), y_df.reshape(T, df_dim), w_o_bf,
                   tm=_tile(T, 1024), tn=_tile(D, 1024))
    out = _mlp(h1, mlp_norm[l][None, :], w_up_bf, w_down_bf, final_norm[None, :],
               tm=_tile(T, 512), tf=_tile(w_up_bf.shape[1], 1024))
    return out.reshape(B, S, D)
```

```python
import functools
import math

import numpy as np
import jax
import jax.numpy as jnp
from jax import lax
from jax.experimental import pallas as pl
from jax.experimental.pallas import tpu as pltpu

F32 = jnp.float32
BF16 = jnp.bfloat16

DN_HEAD_DIM = 128
DF_HEAD_DIM = 128
CONV_WIDTH = 4
CHUNK = 64
NUM_BUCKETS = 32
MAX_DISTANCE = 128
GATE_COLS = 128
IN_PROJ_ROWS = 512
IN_PROJ_COLS = 1024
IN_PROJ_SUB = 256
GDN_BLOCK = 512
GDN_GROUP = 4
ATTN_SUB = 256
ATTN_BLOCK = 512

V7X_VMEM_LIMIT = 56 * 1024 * 1024


def _dot(a, b):
    return jnp.dot(a, b, preferred_element_type=F32)


def _dot_nt(a, b):
    return lax.dot_general(a, b, (((1,), (1,)), ((), ())), preferred_element_type=F32)


def _sigmoid(x):
    return 1.0 / (1.0 + jnp.exp(-x))


def _in_proj_kernel(x_ref, g_ref, wa_ref, wb_ref, wg_ref, wvt_ref, cw_ref, proj_ref, gate_ref,
                    vt_ref, u_ref, hist_ref, cbuf_ref, *, bounds, per_seq, q_scale):
    i = pl.program_id(0)
    j = pl.program_id(1)
    tm, tn = proj_ref.shape
    b0, b1, b2, b3, b4, b5 = bounds

    @pl.when(j == 0)
    def _():
        x = x_ref[...]
        ms = jnp.mean(x * x, axis=-1, keepdims=True)
        u = (x * lax.rsqrt(ms + 1e-6) * g_ref[...]).astype(BF16)
        u_ref[...] = u
        gate_ref[...] = _dot(u, wg_ref[...])

    n_col_chunks = tn // IN_PROJ_SUB
    first_of_seq = (i % per_seq) == 0

    def chunk_cols(c):
        return slice(c * IN_PROJ_SUB, (c + 1) * IN_PROJ_SUB)

    def raw_cols(w_ref, c):
        cbuf_ref[8:8 + tm, chunk_cols(c)] = _dot(u_ref[...], w_ref[:, chunk_cols(c)])

    def raw(c):
        return cbuf_ref[8:8 + tm, chunk_cols(c)]

    def conv_silu(c):
        cw = cw_ref[:, chunk_cols(c)]
        y = None
        for s in range(CONV_WIDTH):
            tap = CONV_WIDTH - 1 - s
            term = cbuf_ref[8 - s:8 - s + tm, chunk_cols(c)] * cw[tap:tap + 1, :]
            y = term if y is None else y + term
        return y * _sigmoid(y)

    def l2norm_heads(y, scale):
        outs = []
        for h in range(IN_PROJ_SUB // DN_HEAD_DIM):
            yh = y[:, h * DN_HEAD_DIM:(h + 1) * DN_HEAD_DIM]
            outs.append(yh * (lax.rsqrt(jnp.sum(yh * yh, axis=-1, keepdims=True) + 1e-6) * scale))
        return jnp.concatenate(outs, axis=1)

    def silu_cols(c):
        z = raw(c)
        return z * _sigmoid(z)

    def project(w_ref, epilogue, conv=False):
        if conv:
            cbuf_ref[0:8, :] = jnp.where(first_of_seq, 0.0, hist_ref[j])
        raw_cols(w_ref, 0)
        for c in range(n_col_chunks):
            if c + 1 < n_col_chunks:
                raw_cols(w_ref, c + 1)
            proj_ref[:, chunk_cols(c)] = epilogue(c).astype(BF16)
        if conv:
            hist_ref[j] = cbuf_ref[tm:tm + 8, :]

    @pl.when(j < b0)
    def _():
        project(wa_ref, lambda c: l2norm_heads(conv_silu(c), DN_HEAD_DIM ** -0.5), conv=True)

    @pl.when((j >= b0) & (j < b1))
    def _():
        project(wa_ref, lambda c: l2norm_heads(conv_silu(c), 1.0), conv=True)

    @pl.when((j >= b1) & (j < b2))
    def _():
        project(wa_ref, conv_silu, conv=True)

    @pl.when((j >= b2) & (j < b3))
    def _():
        project(wa_ref, silu_cols)

    @pl.when((j >= b3) & (j < b4))
    def _():
        project(wb_ref, lambda c: raw(c) * q_scale)

    @pl.when((j >= b4) & (j < b5))
    def _():
        project(wb_ref, raw)

    @pl.when(j >= b5)
    def _():
        vt = _dot_nt(wvt_ref[...], u_ref[...]).astype(BF16)
        tv = vt_ref.shape[-1]
        for c in range(vt_ref.shape[0]):
            vt_ref[c] = vt[:, c * tv:(c + 1) * tv]


def _in_proj(x2, gain, w_a, w_b, w_gate, w_vt, conv_w, *, batch, dn_dim, df_dim, tm, tn, tv,
             q_scale):
    T, D = x2.shape
    N = w_a.shape[1] + w_b.shape[1]
    nv = w_vt.shape[0]
    S = T // batch
    assert w_a.shape[1] == 4 * dn_dim and w_b.shape[1] == 2 * df_dim and nv == df_dim
    assert S % tm == 0 and dn_dim % tn == 0 and df_dim % tn == 0 and tm % tv == 0
    assert tn % DN_HEAD_DIM == 0 and conv_w.shape == (CONV_WIDTH, 3 * dn_dim)
    assert tn % IN_PROJ_SUB == 0 and IN_PROJ_SUB % DN_HEAD_DIM == 0
    n_dn, n_df = dn_dim // tn, df_dim // tn
    bounds = (n_dn, 2 * n_dn, 3 * n_dn, 4 * n_dn, 4 * n_dn + n_df, 4 * n_dn + 2 * n_df)
    n_main = bounds[-1]
    n_conv = bounds[2]
    n_a = bounds[3]
    per_seq = S // tm
    return pl.pallas_call(
        functools.partial(_in_proj_kernel, bounds=bounds, per_seq=per_seq, q_scale=q_scale),
        grid=(T // tm, n_main + nv // tn),
        in_specs=[
            pl.BlockSpec((tm, D), lambda i, j: (i, 0)),
            pl.BlockSpec((1, D), lambda i, j: (0, 0)),
            pl.BlockSpec((D, tn), lambda i, j: (0, jnp.minimum(j, n_a - 1))),
            pl.BlockSpec((D, tn), lambda i, j: (0, jnp.clip(j - n_a, 0, n_main - n_a - 1))),
            pl.BlockSpec((D, GATE_COLS), lambda i, j: (0, 0)),
            pl.BlockSpec((tn, D), lambda i, j: (jnp.maximum(j - n_main, 0), 0)),
            pl.BlockSpec((CONV_WIDTH, tn), lambda i, j: (0, jnp.minimum(j, n_conv - 1))),
        ],
        out_specs=[
            pl.BlockSpec((tm, tn), lambda i, j: (i, jnp.minimum(j, n_main - 1))),
            pl.BlockSpec((tm, GATE_COLS), lambda i, j: (i, 0)),
            pl.BlockSpec((None, tm // tv, tn, tv),
                         lambda i, j: (i // per_seq, i % per_seq, jnp.maximum(j - n_main, 0), 0)),
        ],
        out_shape=[
            jax.ShapeDtypeStruct((T, N), BF16),
            jax.ShapeDtypeStruct((T, GATE_COLS), F32),
            jax.ShapeDtypeStruct((batch, S // tv, nv, tv), BF16),
        ],
        scratch_shapes=[
            pltpu.VMEM((tm, D), BF16),
            pltpu.VMEM((n_conv, 8, tn), F32),
            pltpu.VMEM((8 + tm, tn), F32),
        ],
        compiler_params=pltpu.CompilerParams(
            dimension_semantics=("arbitrary", "arbitrary"),
            vmem_limit_bytes=V7X_VMEM_LIMIT),
        name="in_proj",
    )(x2, gain, w_a, w_b, w_gate, w_vt, conv_w)


def _gdn_kernel(q_ref, k_ref, v_ref, z_ref, gate_ref, alog_ref, dtb_ref, dnn_ref,
                y_ref, state_ref, *, blk, n_heads):
    dk = DN_HEAD_DIM
    n_chunks = blk // CHUNK

    @pl.when(pl.program_id(1) == 0)
    def _():
        state_ref[...] = jnp.zeros_like(state_ref)

    gate = gate_ref[...]
    beta_all = _sigmoid(gate)
    xs = gate + dtb_ref[...]
    softplus = jnp.maximum(xs, 0.0) + jnp.log(1.0 + jnp.exp(-jnp.abs(xs)))
    g_all = -jnp.exp(alog_ref[...]) * softplus
    pos = lax.broadcasted_iota(jnp.int32, g_all.shape, 0) & (CHUNK - 1)
    gc_all = g_all
    step = 1
    while step < CHUNK:
        gc_all = gc_all + jnp.where(pos >= step, pltpu.roll(gc_all, step, 0), 0.0)
        step *= 2

    ri = lax.broadcasted_iota(jnp.int32, (CHUNK, CHUNK), 0)
    ci = lax.broadcasted_iota(jnp.int32, (CHUNK, CHUNK), 1)
    tril = ri >= ci
    strict = ri > ci
    eye = ri == ci
    gain = dnn_ref[...]

    def first_stage(c):
        rows = slice(c * CHUNK, (c + 1) * CHUNK)
        items = []
        for h in range(n_heads):
            cols = slice(h * dk, (h + 1) * dk)
            q_bf = q_ref[rows, cols]
            k_bf = k_ref[rows, cols]
            qc = q_bf.astype(F32)
            kc = k_bf.astype(F32)
            vc = v_ref[rows, cols].astype(F32)
            bc = jnp.broadcast_to(beta_all[rows, h:h + 1], (CHUNK, dk))
            gcc = jnp.broadcast_to(gc_all[rows, n_heads + h:n_heads + h + 1], (CHUNK, dk))
            g_last = gcc[CHUNK - 1:CHUNK, :]
            eg = jnp.exp(gcc)
            g_sq = gcc[:, 0:CHUNK]
            g_row = jnp.sum(jnp.where(eye, g_sq, 0.0), axis=0, keepdims=True)
            decay = jnp.exp(jnp.where(tril, g_sq - g_row, -jnp.inf))
            kb = kc * bc
            lhs = jnp.concatenate([kb.astype(BF16), q_bf], axis=0)
            aq = _dot_nt(lhs, k_bf)
            n_mat = jnp.where(strict, -(aq[:CHUNK] * decay), 0.0)
            items.append(dict(
                h=h, qe=qc * eg, g_last=g_last,
                rhs=jnp.concatenate([kb * eg, vc * bc], axis=1).astype(BF16),
                qk=jnp.where(tril, aq[CHUNK:] * decay, 0.0).astype(BF16),
                kd_t=(kc * jnp.exp(g_last - gcc)).T.astype(BF16),
                power=n_mat, inv=jnp.where(eye, 1.0, 0.0) + n_mat))
        return items

    def matrix_stages(items):
        span = 2
        while span < CHUNK:
            for s in items:
                pb = s["power"].astype(BF16)
                s["power"] = _dot(pb, pb)
            for s in items:
                s["inv"] = s["inv"] + _dot(s["inv"].astype(BF16), s["power"].astype(BF16))
            span *= 2
        for s in items:
            s["wu"] = _dot(s["inv"].astype(BF16), s["rhs"]).astype(BF16)
        for s in items:
            s["gr"] = _dot(s["kd_t"], s["wu"])
            qw = _dot(s["qk"], s["wu"])
            s["q_eff"] = (s["qe"] - qw[:, :dk]).astype(BF16)
            s["p_loc"] = qw[:, dk:]

    def state_stage(c, items):
        rows = slice(c * CHUNK, (c + 1) * CHUNK)
        for s in items:
            h = s["h"]
            state = state_ref[h]
            s_bf = state.astype(BF16)
            o = _dot(s["q_eff"], s_bf) + s["p_loc"]
            state_ref[h] = (state * jnp.exp(s["g_last"])
                            - _dot(s["gr"][:, :dk].astype(BF16), s_bf) + s["gr"][:, dk:])
            zs = z_ref[rows, h * dk:(h + 1) * dk].astype(F32)
            o = o * lax.rsqrt(jnp.mean(o * o, axis=-1, keepdims=True) + 1e-6) * gain
            y_ref[rows, h * dk:(h + 1) * dk] = (o * zs).astype(y_ref.dtype)

    groups = [list(range(g, min(g + GDN_GROUP, n_chunks))) for g in range(0, n_chunks, GDN_GROUP)]
    cur = [first_stage(c) for c in groups[0]]
    for gi, chunk_ids in enumerate(groups):
        nxt = [first_stage(c) for c in groups[gi + 1]] if gi + 1 < len(groups) else None
        matrix_stages([s for items in cur for s in items])
        for c, items in zip(chunk_ids, cur):
            state_stage(c, items)
        cur = nxt


def _gdn(proj3, gate3, alog_row, dtb_row, dn_norm, *, n_heads, blk):
    B, S, _ = proj3.shape
    dk = DN_HEAD_DIM
    dn = n_heads * dk

    def group(idx):
        return pl.BlockSpec((None, blk, dn), lambda b, t: (b, t, idx))

    def whole(arr):
        return pl.BlockSpec(arr.shape, lambda b, t: (0,) * arr.ndim)

    return pl.pallas_call(
        functools.partial(_gdn_kernel, blk=blk, n_heads=n_heads),
        grid=(B, S // blk),
        in_specs=[
            group(0), group(1), group(2), group(3),
            pl.BlockSpec((None, blk, GATE_COLS), lambda b, t: (b, t, 0)),
            whole(alog_row), whole(dtb_row), whole(dn_norm),
        ],
        out_specs=pl.BlockSpec((None, blk, dn), lambda b, t: (b, t, 0)),
        out_shape=jax.ShapeDtypeStruct((B, S, dn), BF16),
        scratch_shapes=[pltpu.VMEM((n_heads, dk, dk), F32)],
        compiler_params=pltpu.CompilerParams(
            dimension_semantics=("arbitrary", "arbitrary"),
            vmem_limit_bytes=V7X_VMEM_LIMIT),
        name="gdn",
    )(proj3, proj3, proj3, proj3, gate3, alog_row, dtb_row, dn_norm)


def _t5_bucket_starts():
    max_exact = NUM_BUCKETS // 2
    n = np.arange(0, MAX_DISTANCE + 1)
    nf = np.maximum(n, 1).astype(np.float32)
    large = max_exact + (np.log(nf / max_exact) / math.log(MAX_DISTANCE / max_exact)
                         * (NUM_BUCKETS - max_exact)).astype(np.int32)
    bucket = np.where(n < max_exact, n, np.minimum(large, NUM_BUCKETS - 1))
    assert bucket[MAX_DISTANCE] == NUM_BUCKETS - 1 and np.all(np.diff(bucket) >= 0)
    starts = [(0, int(bucket[0]))]
    for d in range(1, MAX_DISTANCE + 1):
        if bucket[d] != bucket[d - 1]:
            starts.append((d, int(bucket[d])))
    return starts


def _diff_attn_kernel(rb_ref, q_ref, k_ref, vt_ref, lq1_ref, lk1_ref, lq2_ref, lk2_ref, dfn_ref,
                      y_ref, bias_ref, s_ref, p_ref, m_ref, l_ref, acc_ref,
                      *, tq, n_heads, lam_init):
    h = pl.program_id(1)
    qi = pl.program_id(2)
    d = DF_HEAD_DIM
    sub = ATTN_SUB
    n_sub = tq // sub
    log2e = math.log2(math.e)

    @pl.when(qi == 0)
    def _():
        keys = lax.broadcasted_iota(jnp.int32, (sub, sub), 0)
        qrys = lax.broadcasted_iota(jnp.int32, (sub, sub), 1)
        starts = _t5_bucket_starts()
        for idx in range(2):
            dist = qrys - keys + idx * sub
            tile = jnp.full((sub, sub), rb_ref[starts[0][1] * n_heads + h], F32)
            for first, bucket in starts[1:]:
                tile = jnp.where(dist >= first, rb_ref[bucket * n_heads + h], tile)
            tile = tile * log2e
            if idx == 0:
                tile = jnp.where(dist >= 0, tile, -jnp.inf)
            bias_ref[idx] = tile
        bias_ref[2] = jnp.full((sub, sub), rb_ref[(NUM_BUCKETS - 1) * n_heads + h] * log2e, F32)
        bias_ref[3] = jnp.full((sub, sub), -jnp.inf, F32)

    m_ref[...] = jnp.full(m_ref.shape, -jnp.inf, F32)
    l_ref[...] = jnp.zeros_like(l_ref)
    acc_ref[...] = jnp.zeros_like(acc_ref)

    chains = [(qb, m) for qb in range(n_sub) for m in range(2)]

    far_shift = rb_ref[(NUM_BUCKETS - 1) * n_heads + h] * log2e

    def produce(j, c):
        qb, m = chains[c]
        k0 = pl.multiple_of(j * tq, tq)
        s_ref[c] = _dot_nt(k_ref[pl.ds(k0, tq), m * d:(m + 1) * d],
                           q_ref[qb * sub:(qb + 1) * sub, m * d:(m + 1) * d])

    def softmax(j, c, far):
        qb, m = chains[c]
        if far:
            s = s_ref[c]
            shift = far_shift
        else:
            parts = []
            for kb in range(n_sub):
                off = (qi - j) * n_sub + (qb - kb)
                idx = jnp.where(off < 0, 3, jnp.minimum(off, 2))
                parts.append(s_ref[c, kb * sub:(kb + 1) * sub, :] + bias_ref[idx])
            s = jnp.concatenate(parts, axis=0)
            shift = 0.0
        cols = slice(qb * sub, (qb + 1) * sub)
        m_prev = m_ref[m, :, cols]
        m_new = jnp.maximum(m_prev, jnp.max(s, axis=0, keepdims=True) + shift)
        alpha = jnp.exp2(m_prev - m_new)
        pr = jnp.exp2(s - (m_new - shift))
        l_ref[m, :, cols] = alpha * l_ref[m, :, cols] + jnp.sum(pr, axis=0, keepdims=True)
        acc_ref[m, :, cols] = alpha * acc_ref[m, :, cols]
        m_ref[m, :, cols] = m_new
        return pr.astype(BF16)

    def add_values(j, c, p):
        qb, m = chains[c]
        cols = slice(qb * sub, (qb + 1) * sub)
        acc_ref[m, :, cols] += _dot(vt_ref[j], p)

    n_chains = len(chains)
    for c in range(n_chains):
        produce(0, c)
    p_ref[...] = softmax(0, 0, False)

    def trip(j, far):
        add_values(j, 0, p_ref[...])
        produce(j + 1, 0)
        for c in range(1, n_chains):
            add_values(j, c, softmax(j, c, far))
            produce(j + 1, c)
        p_ref[...] = softmax(j + 1, 0, far)

    n_far = jnp.maximum(qi - 2, 0)
    lax.fori_loop(0, n_far, lambda j, carry: (trip(j, True), carry)[1], 0)
    lax.fori_loop(n_far, qi, lambda j, carry: (trip(j, False), carry)[1], 0)
    add_values(qi, 0, p_ref[...])
    for c in range(1, n_chains):
        add_values(qi, c, softmax(qi, c, False))

    lam = (jnp.exp(jnp.sum(lq1_ref[...] * lk1_ref[...], axis=-1, keepdims=True))
           - jnp.exp(jnp.sum(lq2_ref[...] * lk2_ref[...], axis=-1, keepdims=True))
           + lam_init)
    o = acc_ref[0] * (1.0 / l_ref[0]) - acc_ref[1] * (lam / l_ref[1])
    o = o * lax.rsqrt(jnp.mean(o * o, axis=0, keepdims=True) + 1e-5) * dfn_ref[...]
    y_ref[...] = (o * (1.0 - lam_init)).T.astype(y_ref.dtype)


def _diff_attn(proj3, v_t, rel_bias, lq1, lk1, lq2, lk2, df_norm_col, *, n_heads, col0, tq,
               lam_init):
    B, S, _ = proj3.shape
    d2 = 2 * DF_HEAD_DIM
    assert tq % ATTN_SUB == 0 and ATTN_SUB >= MAX_DISTANCE and S % tq == 0
    nq = S // tq
    n_chains = 2 * (tq // ATTN_SUB)
    cb = col0 // d2
    vec = lambda n: pl.BlockSpec((1, n), lambda b, h, i: (0, 0))
    return pl.pallas_call(
        functools.partial(_diff_attn_kernel, tq=tq, n_heads=n_heads, lam_init=lam_init),
        grid=(B, n_heads, nq),
        in_specs=[
            pl.BlockSpec(memory_space=pltpu.SMEM),
            pl.BlockSpec((None, tq, d2), lambda b, h, i: (b, i, cb + h)),
            pl.BlockSpec((None, S, d2), lambda b, h, i: (b, 0, cb + n_heads + h)),
            pl.BlockSpec((None, nq, d2, tq), lambda b, h, i: (b, 0, h, 0)),
            vec(DF_HEAD_DIM), vec(DF_HEAD_DIM), vec(DF_HEAD_DIM), vec(DF_HEAD_DIM),
            pl.BlockSpec((d2, 1), lambda b, h, i: (0, 0)),
        ],
        out_specs=pl.BlockSpec((None, tq, d2), lambda b, h, i: (b, i, h)),
        out_shape=jax.ShapeDtypeStruct((B, S, n_heads * d2), BF16),
        scratch_shapes=[
            pltpu.VMEM((4, ATTN_SUB, ATTN_SUB), F32),
            pltpu.VMEM((n_chains, tq, ATTN_SUB), F32),
            pltpu.VMEM((tq, ATTN_SUB), BF16),
            pltpu.VMEM((2, 1, tq), F32),
            pltpu.VMEM((2, 1, tq), F32),
            pltpu.VMEM((2, d2, tq), F32),
        ],
        compiler_params=pltpu.CompilerParams(
            dimension_semantics=("arbitrary", "arbitrary", "arbitrary"),
            vmem_limit_bytes=V7X_VMEM_LIMIT),
        name="diff_attn",
    )(rel_bias, proj3, proj3, v_t, lq1, lk1, lq2, lk2, df_norm_col)


def _out_proj_kernel(x_ref, ya_ref, yb_ref, wa_ref, wb_ref, h_ref):
    h_ref[...] = x_ref[...] + _dot(ya_ref[...], wa_ref[...]) + _dot(yb_ref[...], wb_ref[...])


def _out_proj(x2, y_dn, y_df, w_o, *, tm, tn):
    T, D = x2.shape
    ka = y_dn.shape[1]
    kb = y_df.shape[1]
    assert ka == kb
    return pl.pallas_call(
        _out_proj_kernel,
        grid=(T // tm, D // tn),
        in_specs=[
            pl.BlockSpec((tm, tn), lambda i, j: (i, j)),
            pl.BlockSpec((tm, ka), lambda i, j: (i, 0)),
            pl.BlockSpec((tm, kb), lambda i, j: (i, 0)),
            pl.BlockSpec((ka, tn), lambda i, j: (0, j)),
            pl.BlockSpec((kb, tn), lambda i, j: (1, j)),
        ],
        out_specs=pl.BlockSpec((tm, tn), lambda i, j: (i, j)),
        out_shape=jax.ShapeDtypeStruct((T, D), F32),
        compiler_params=pltpu.CompilerParams(
            dimension_semantics=("arbitrary", "arbitrary"),
            vmem_limit_bytes=V7X_VMEM_LIMIT),
        name="out_proj",
    )(x2, y_dn, y_df, w_o, w_o)


def _mlp_kernel(h_ref, g_ref, wu_ref, wd_ref, gf_ref, o_ref, u_ref, acc_ref):
    f = pl.program_id(1)

    @pl.when(f == 0)
    def _():
        x = h_ref[...]
        ms = jnp.mean(x * x, axis=-1, keepdims=True)
        u_ref[...] = (x * lax.rsqrt(ms + 1e-6) * g_ref[...]).astype(BF16)
        acc_ref[...] = jnp.zeros_like(acc_ref)

    hid = jnp.maximum(_dot(u_ref[...], wu_ref[...]), 0.0)
    acc_ref[...] += _dot((hid * hid).astype(BF16), wd_ref[...])

    @pl.when(f == pl.num_programs(1) - 1)
    def _():
        y = h_ref[...] + acc_ref[...]
        ms = jnp.mean(y * y, axis=-1, keepdims=True)
        o_ref[...] = y * lax.rsqrt(ms + 1e-6) * gf_ref[...]


def _mlp(h1, gain, w_up, w_down, final_gain, *, tm, tf):
    T, D = h1.shape
    Fdim = w_up.shape[1]
    return pl.pallas_call(
        _mlp_kernel,
        grid=(T // tm, Fdim // tf),
        in_specs=[
            pl.BlockSpec((tm, D), lambda i, f: (i, 0)),
            pl.BlockSpec((1, D), lambda i, f: (0, 0)),
            pl.BlockSpec((D, tf), lambda i, f: (0, f)),
            pl.BlockSpec((tf, D), lambda i, f: (f, 0)),
            pl.BlockSpec((1, D), lambda i, f: (0, 0)),
        ],
        out_specs=pl.BlockSpec((tm, D), lambda i, f: (i, 0)),
        out_shape=jax.ShapeDtypeStruct((T, D), F32),
        scratch_shapes=[pltpu.VMEM((tm, D), BF16), pltpu.VMEM((tm, D), F32)],
        compiler_params=pltpu.CompilerParams(
            dimension_semantics=("arbitrary", "arbitrary"),
            vmem_limit_bytes=V7X_VMEM_LIMIT),
        name="mlp",
    )(h1, gain, w_up, w_down, final_gain)


def _tile(n, pref):
    if n <= pref:
        return n
    t = pref - pref % 128
    while t > 128 and n % t:
        t -= 128
    assert n % t == 0
    return t


def kernel(x, attn_norm, w_in, conv_w, a_log, dt_bias, dn_norm, lambda_q1, lambda_k1,
           lambda_q2, lambda_k2, df_norm, rel_bias, w_o, mlp_norm, w_up, w_down, final_norm):
    B, S, D = x.shape
    depth = attn_norm.shape[0]
    n_dn = a_log.shape[1]
    n_df = rel_bias.shape[1]
    dn_dim = n_dn * DN_HEAD_DIM
    df_dim = n_df * 2 * DF_HEAD_DIM
    T = B * S
    gate0 = 4 * dn_dim
    assert w_in.shape[2] == gate0 + 2 * n_dn + 3 * df_dim and 2 * n_dn <= GATE_COLS

    assert depth == 1
    l = 0
    h = x.reshape(T, D)

    wl = w_in[l]
    dfq0 = gate0 + 2 * n_dn
    dfv0 = dfq0 + 2 * df_dim
    w_a = wl[:, :gate0].astype(BF16)
    w_b = wl[:, dfq0:dfv0].astype(BF16)
    w_gate = wl[:, gate0:gate0 + GATE_COLS].astype(BF16)
    w_vt = wl[:, dfv0:].T.astype(BF16)
    w_o_bf = w_o[l].astype(BF16)
    w_up_bf = w_up[l].astype(BF16)
    w_down_bf = w_down[l].astype(BF16)

    proj, gates, v_t = _in_proj(h, attn_norm[l][None, :], w_a, w_b, w_gate, w_vt, conv_w[l],
                                batch=B, dn_dim=dn_dim, df_dim=df_dim, tm=_tile(S, IN_PROJ_ROWS),
                                tn=_tile(math.gcd(dn_dim, df_dim), IN_PROJ_COLS),
                                tv=_tile(S, ATTN_BLOCK),
                                q_scale=DF_HEAD_DIM ** -0.5 * math.log2(math.e))
    proj3 = proj.reshape(B, S, -1)
    gate3 = gates.reshape(B, S, GATE_COLS)

    gate_pad = ((0, 0), (n_dn, GATE_COLS - 2 * n_dn))
    alog_row = jnp.pad(a_log[l][None, :], gate_pad)
    dtb_row = jnp.pad(dt_bias[l][None, :], gate_pad)
    y_dn = _gdn(proj3, gate3, alog_row, dtb_row, dn_norm[l][None, :],
                n_heads=n_dn, blk=_tile(S, GDN_BLOCK))
    lam_init = 0.8 - 0.6 * math.exp(-0.3 * l)
    y_df = _diff_attn(proj3, v_t, rel_bias.reshape(-1),
                      lambda_q1[l][None, :], lambda_k1[l][None, :],
                      lambda_q2[l][None, :], lambda_k2[l][None, :], df_norm[l][:, None],
                      n_heads=n_df, col0=gate0, tq=_tile(S, ATTN_BLOCK), lam_init=lam_init)

    h1 = _out_proj(h, y_dn.reshape(T, dn_dim), y_df.reshape(T, df_dim), w_o_bf,
                   tm=_tile(T, 512), tn=_tile(D, 2048))
    out = _mlp(h1, mlp_norm[l][None, :], w_up_bf, w_down_bf, final_norm[None, :],
               tm=_tile(T, 512), tf=_tile(w_up_bf.shape[1], 1024))
    return out.reshape(B, S, D)
```

```python
import functools
import math

import numpy as np
import jax
import jax.numpy as jnp
from jax import lax
from jax.experimental import pallas as pl
from jax.experimental.pallas import tpu as pltpu

F32 = jnp.float32
BF16 = jnp.bfloat16

DN_HEAD_DIM = 128
DF_HEAD_DIM = 128
CONV_WIDTH = 4
CHUNK = 64
NUM_BUCKETS = 32
MAX_DISTANCE = 128
GATE_COLS = 128
IN_PROJ_ROWS = 512
IN_PROJ_COLS = 1024
IN_PROJ_SUB = 256
GDN_BLOCK = 512
GDN_GROUP = 4
ATTN_SUB = 256
ATTN_BLOCK = 512

V7X_VMEM_LIMIT = 56 * 1024 * 1024


def _dot(a, b):
    return jnp.dot(a, b, preferred_element_type=F32)


def _dot_nt(a, b):
    return lax.dot_general(a, b, (((1,), (1,)), ((), ())), preferred_element_type=F32)


def _sigmoid(x):
    return 1.0 / (1.0 + jnp.exp(-x))


def _in_proj_kernel(x_ref, g_ref, wa_ref, wb_ref, wg_ref, wvt_ref, cw_ref, proj_ref, gate_ref,
                    vt_ref, u_ref, hist_ref, cbuf_ref, *, bounds, per_seq, q_scale):
    i = pl.program_id(0)
    j = pl.program_id(1)
    tm, tn = proj_ref.shape
    b0, b1, b2, b3, b4, b5 = bounds

    @pl.when(j == 0)
    def _():
        x = x_ref[...]
        ms = jnp.mean(x * x, axis=-1, keepdims=True)
        u = (x * lax.rsqrt(ms + 1e-6) * g_ref[...]).astype(BF16)
        u_ref[...] = u
        gate_ref[...] = _dot(u, wg_ref[...])

    n_col_chunks = tn // IN_PROJ_SUB
    first_of_seq = (i % per_seq) == 0

    def chunk_cols(c):
        return slice(c * IN_PROJ_SUB, (c + 1) * IN_PROJ_SUB)

    def raw_cols(w_ref, c):
        cbuf_ref[8:8 + tm, chunk_cols(c)] = _dot(u_ref[...], w_ref[:, chunk_cols(c)])

    def raw(c):
        return cbuf_ref[8:8 + tm, chunk_cols(c)]

    def conv_silu(c):
        cw = cw_ref[:, chunk_cols(c)]
        y = None
        for s in range(CONV_WIDTH):
            tap = CONV_WIDTH - 1 - s
            term = cbuf_ref[8 - s:8 - s + tm, chunk_cols(c)] * cw[tap:tap + 1, :]
            y = term if y is None else y + term
        return y * _sigmoid(y)

    def l2norm_heads(y, scale):
        outs = []
        for h in range(IN_PROJ_SUB // DN_HEAD_DIM):
            yh = y[:, h * DN_HEAD_DIM:(h + 1) * DN_HEAD_DIM]
            outs.append(yh * (lax.rsqrt(jnp.sum(yh * yh, axis=-1, keepdims=True) + 1e-6) * scale))
        return jnp.concatenate(outs, axis=1)

    def silu_cols(c):
        z = raw(c)
        return z * _sigmoid(z)

    def project(w_ref, epilogue, conv=False):
        if conv:
            cbuf_ref[0:8, :] = jnp.where(first_of_seq, 0.0, hist_ref[j])
        raw_cols(w_ref, 0)
        for c in range(n_col_chunks):
            if c + 1 < n_col_chunks:
                raw_cols(w_ref, c + 1)
            proj_ref[:, chunk_cols(c)] = epilogue(c).astype(BF16)
        if conv:
            hist_ref[j] = cbuf_ref[tm:tm + 8, :]

    @pl.when(j < b0)
    def _():
        project(wa_ref, lambda c: l2norm_heads(conv_silu(c), DN_HEAD_DIM ** -0.5), conv=True)

    @pl.when((j >= b0) & (j < b1))
    def _():
        project(wa_ref, lambda c: l2norm_heads(conv_silu(c), 1.0), conv=True)

    @pl.when((j >= b1) & (j < b2))
    def _():
        project(wa_ref, conv_silu, conv=True)

    @pl.when((j >= b2) & (j < b3))
    def _():
        project(wa_ref, silu_cols)

    @pl.when((j >= b3) & (j < b4))
    def _():
        project(wb_ref, lambda c: raw(c) * q_scale)

    @pl.when((j >= b4) & (j < b5))
    def _():
        project(wb_ref, raw)

    @pl.when(j >= b5)
    def _():
        vt = _dot_nt(wvt_ref[...], u_ref[...]).astype(BF16)
        tv = vt_ref.shape[-1]
        for c in range(vt_ref.shape[0]):
            vt_ref[c] = vt[:, c * tv:(c + 1) * tv]


def _in_proj(x2, gain, w_a, w_b, w_gate, w_vt, conv_w, *, batch, dn_dim, df_dim, tm, tn, tv,
             q_scale):
    T, D = x2.shape
    N = w_a.shape[1] + w_b.shape[1]
    nv = w_vt.shape[0]
    S = T // batch
    assert w_a.shape[1] == 4 * dn_dim and w_b.shape[1] == 2 * df_dim and nv == df_dim
    assert S % tm == 0 and dn_dim % tn == 0 and df_dim % tn == 0 and tm % tv == 0
    assert tn % DN_HEAD_DIM == 0 and conv_w.shape == (CONV_WIDTH, 3 * dn_dim)
    assert tn % IN_PROJ_SUB == 0 and IN_PROJ_SUB % DN_HEAD_DIM == 0
    n_dn, n_df = dn_dim // tn, df_dim // tn
    bounds = (n_dn, 2 * n_dn, 3 * n_dn, 4 * n_dn, 4 * n_dn + n_df, 4 * n_dn + 2 * n_df)
    n_main = bounds[-1]
    n_conv = bounds[2]
    n_a = bounds[3]
    n_b = n_main - n_a
    per_seq = S // tm
    return pl.pallas_call(
        functools.partial(_in_proj_kernel, bounds=bounds, per_seq=per_seq, q_scale=q_scale),
        grid=(T // tm, n_main + nv // tn),
        in_specs=[
            pl.BlockSpec((tm, D), lambda i, j: (i, 0)),
            pl.BlockSpec((1, D), lambda i, j: (0, 0)),
            pl.BlockSpec((D, tn), lambda i, j: (0, jnp.where(j < n_a, j, 0))),
            pl.BlockSpec((D, tn), lambda i, j: (0, jnp.where(j < n_a, n_b - 1,
                                                             jnp.minimum(j - n_a, n_b - 1)))),
            pl.BlockSpec((D, GATE_COLS), lambda i, j: (0, 0)),
            pl.BlockSpec((tn, D), lambda i, j: (jnp.maximum(j - n_main, 0), 0)),
            pl.BlockSpec((CONV_WIDTH, tn), lambda i, j: (0, jnp.minimum(j, n_conv - 1))),
        ],
        out_specs=[
            pl.BlockSpec((tm, tn), lambda i, j: (i, jnp.minimum(j, n_main - 1))),
            pl.BlockSpec((tm, GATE_COLS), lambda i, j: (i, 0)),
            pl.BlockSpec((None, tm // tv, tn, tv),
                         lambda i, j: (i // per_seq, i % per_seq, jnp.maximum(j - n_main, 0), 0)),
        ],
        out_shape=[
            jax.ShapeDtypeStruct((T, N), BF16),
            jax.ShapeDtypeStruct((T, GATE_COLS), F32),
            jax.ShapeDtypeStruct((batch, S // tv, nv, tv), BF16),
        ],
        scratch_shapes=[
            pltpu.VMEM((tm, D), BF16),
            pltpu.VMEM((n_conv, 8, tn), F32),
            pltpu.VMEM((8 + tm, tn), F32),
        ],
        compiler_params=pltpu.CompilerParams(
            dimension_semantics=("arbitrary", "arbitrary"),
            vmem_limit_bytes=V7X_VMEM_LIMIT),
        name="in_proj",
    )(x2, gain, w_a, w_b, w_gate, w_vt, conv_w)


def _gdn_kernel(q_ref, k_ref, v_ref, z_ref, gate_ref, alog_ref, dtb_ref, dnn_ref,
                y_ref, state_ref, *, blk, n_heads):
    dk = DN_HEAD_DIM
    n_chunks = blk // CHUNK

    @pl.when(pl.program_id(1) == 0)
    def _():
        state_ref[...] = jnp.zeros_like(state_ref)

    gate = gate_ref[...]
    beta_all = _sigmoid(gate)
    xs = gate + dtb_ref[...]
    softplus = jnp.maximum(xs, 0.0) + jnp.log(1.0 + jnp.exp(-jnp.abs(xs)))
    g_all = -jnp.exp(alog_ref[...]) * softplus
    pos = lax.broadcasted_iota(jnp.int32, g_all.shape, 0) & (CHUNK - 1)
    gc_all = g_all
    step = 1
    while step < CHUNK:
        gc_all = gc_all + jnp.where(pos >= step, pltpu.roll(gc_all, step, 0), 0.0)
        step *= 2

    ri = lax.broadcasted_iota(jnp.int32, (CHUNK, CHUNK), 0)
    ci = lax.broadcasted_iota(jnp.int32, (CHUNK, CHUNK), 1)
    tril = ri >= ci
    strict = ri > ci
    eye = ri == ci
    gain = dnn_ref[...]

    def first_stage(c):
        rows = slice(c * CHUNK, (c + 1) * CHUNK)
        items = []
        for h in range(n_heads):
            cols = slice(h * dk, (h + 1) * dk)
            q_bf = q_ref[rows, cols]
            k_bf = k_ref[rows, cols]
            qc = q_bf.astype(F32)
            kc = k_bf.astype(F32)
            vc = v_ref[rows, cols].astype(F32)
            bc = jnp.broadcast_to(beta_all[rows, h:h + 1], (CHUNK, dk))
            gcc = jnp.broadcast_to(gc_all[rows, n_heads + h:n_heads + h + 1], (CHUNK, dk))
            g_last = gcc[CHUNK - 1:CHUNK, :]
            eg = jnp.exp(gcc)
            g_sq = gcc[:, 0:CHUNK]
            g_row = jnp.sum(jnp.where(eye, g_sq, 0.0), axis=0, keepdims=True)
            decay = jnp.exp(jnp.where(tril, g_sq - g_row, -jnp.inf))
            kb = kc * bc
            lhs = jnp.concatenate([kb.astype(BF16), q_bf], axis=0)
            aq = _dot_nt(lhs, k_bf)
            n_mat = jnp.where(strict, -(aq[:CHUNK] * decay), 0.0)
            items.append(dict(
                h=h, qe=qc * eg, g_last=g_last,
                rhs=jnp.concatenate([kb * eg, vc * bc], axis=1).astype(BF16),
                qk=jnp.where(tril, aq[CHUNK:] * decay, 0.0).astype(BF16),
                kd_t=(kc * jnp.exp(g_last - gcc)).T.astype(BF16),
                power=n_mat, inv=jnp.where(eye, 1.0, 0.0) + n_mat))
        return items

    def matrix_stages(items):
        span = 2
        while span < CHUNK:
            for s in items:
                pb = s["power"].astype(BF16)
                s["power"] = _dot(pb, pb)
            for s in items:
                s["inv"] = s["inv"] + _dot(s["inv"].astype(BF16), s["power"].astype(BF16))
            span *= 2
        for s in items:
            s["wu"] = _dot(s["inv"].astype(BF16), s["rhs"]).astype(BF16)
        for s in items:
            s["gr"] = _dot(s["kd_t"], s["wu"])
            qw = _dot(s["qk"], s["wu"])
            s["q_eff"] = (s["qe"] - qw[:, :dk]).astype(BF16)
            s["p_loc"] = qw[:, dk:]

    def state_stage(c, items):
        rows = slice(c * CHUNK, (c + 1) * CHUNK)
        for s in items:
            h = s["h"]
            state = state_ref[h]
            s_bf = state.astype(BF16)
            o = _dot(s["q_eff"], s_bf) + s["p_loc"]
            state_ref[h] = (state * jnp.exp(s["g_last"])
                            - _dot(s["gr"][:, :dk].astype(BF16), s_bf) + s["gr"][:, dk:])
            zs = z_ref[rows, h * dk:(h + 1) * dk].astype(F32)
            o = o * lax.rsqrt(jnp.mean(o * o, axis=-1, keepdims=True) + 1e-6) * gain
            y_ref[rows, h * dk:(h + 1) * dk] = (o * zs).astype(y_ref.dtype)

    groups = [list(range(g, min(g + GDN_GROUP, n_chunks))) for g in range(0, n_chunks, GDN_GROUP)]
    cur = [first_stage(c) for c in groups[0]]
    for gi, chunk_ids in enumerate(groups):
        nxt = [first_stage(c) for c in groups[gi + 1]] if gi + 1 < len(groups) else None
        matrix_stages([s for items in cur for s in items])
        for c, items in zip(chunk_ids, cur):
            state_stage(c, items)
        cur = nxt


def _gdn(proj3, gate3, alog_row, dtb_row, dn_norm, *, n_heads, blk):
    B, S, _ = proj3.shape
    dk = DN_HEAD_DIM
    dn = n_heads * dk

    def group(idx):
        return pl.BlockSpec((None, blk, dn), lambda b, t: (b, t, idx))

    def whole(arr):
        return pl.BlockSpec(arr.shape, lambda b, t: (0,) * arr.ndim)

    return pl.pallas_call(
        functools.partial(_gdn_kernel, blk=blk, n_heads=n_heads),
        grid=(B, S // blk),
        in_specs=[
            group(0), group(1), group(2), group(3),
            pl.BlockSpec((None, blk, GATE_COLS), lambda b, t: (b, t, 0)),
            whole(alog_row), whole(dtb_row), whole(dn_norm),
        ],
        out_specs=pl.BlockSpec((None, blk, dn), lambda b, t: (b, t, 0)),
        out_shape=jax.ShapeDtypeStruct((B, S, dn), BF16),
        scratch_shapes=[pltpu.VMEM((n_heads, dk, dk), F32)],
        compiler_params=pltpu.CompilerParams(
            dimension_semantics=("arbitrary", "arbitrary"),
            vmem_limit_bytes=V7X_VMEM_LIMIT),
        name="gdn",
    )(proj3, proj3, proj3, proj3, gate3, alog_row, dtb_row, dn_norm)


def _t5_bucket_starts():
    max_exact = NUM_BUCKETS // 2
    n = np.arange(0, MAX_DISTANCE + 1)
    nf = np.maximum(n, 1).astype(np.float32)
    large = max_exact + (np.log(nf / max_exact) / math.log(MAX_DISTANCE / max_exact)
                         * (NUM_BUCKETS - max_exact)).astype(np.int32)
    bucket = np.where(n < max_exact, n, np.minimum(large, NUM_BUCKETS - 1))
    assert bucket[MAX_DISTANCE] == NUM_BUCKETS - 1 and np.all(np.diff(bucket) >= 0)
    starts = [(0, int(bucket[0]))]
    for d in range(1, MAX_DISTANCE + 1):
        if bucket[d] != bucket[d - 1]:
            starts.append((d, int(bucket[d])))
    return starts


def _diff_attn_kernel(rb_ref, q_ref, k_ref, vt_ref, lq1_ref, lk1_ref, lq2_ref, lk2_ref, dfn_ref,
                      y_ref, bias_ref, s_ref, p_ref, m_ref, l_ref, acc_ref,
                      *, tq, n_heads, lam_init):
    h = pl.program_id(1)
    qi = pl.program_id(2)
    d = DF_HEAD_DIM
    sub = ATTN_SUB
    n_sub = tq // sub
    log2e = math.log2(math.e)

    @pl.when(qi == 0)
    def _():
        keys = lax.broadcasted_iota(jnp.int32, (sub, sub), 0)
        qrys = lax.broadcasted_iota(jnp.int32, (sub, sub), 1)
        starts = _t5_bucket_starts()
        for idx in range(2):
            dist = qrys - keys + idx * sub
            tile = jnp.full((sub, sub), rb_ref[starts[0][1] * n_heads + h], F32)
            for first, bucket in starts[1:]:
                tile = jnp.where(dist >= first, rb_ref[bucket * n_heads + h], tile)
            tile = tile * log2e
            if idx == 0:
                tile = jnp.where(dist >= 0, tile, -jnp.inf)
            bias_ref[idx] = tile
        bias_ref[2] = jnp.full((sub, sub), rb_ref[(NUM_BUCKETS - 1) * n_heads + h] * log2e, F32)
        bias_ref[3] = jnp.full((sub, sub), -jnp.inf, F32)

    m_ref[...] = jnp.full(m_ref.shape, -jnp.inf, F32)
    l_ref[...] = jnp.zeros_like(l_ref)
    acc_ref[...] = jnp.zeros_like(acc_ref)

    chains = [(qb, m) for qb in range(n_sub) for m in range(2)]

    far_shift = rb_ref[(NUM_BUCKETS - 1) * n_heads + h] * log2e

    def produce(j, c):
        qb, m = chains[c]
        k0 = pl.multiple_of(j * tq, tq)
        s_ref[c] = _dot_nt(k_ref[pl.ds(k0, tq), m * d:(m + 1) * d],
                           q_ref[qb * sub:(qb + 1) * sub, m * d:(m + 1) * d])

    def softmax(j, c, far):
        qb, m = chains[c]
        if far:
            s = s_ref[c]
            shift = far_shift
        else:
            parts = []
            for kb in range(n_sub):
                off = (qi - j) * n_sub + (qb - kb)
                idx = jnp.where(off < 0, 3, jnp.minimum(off, 2))
                parts.append(s_ref[c, kb * sub:(kb + 1) * sub, :] + bias_ref[idx])
            s = jnp.concatenate(parts, axis=0)
            shift = 0.0
        cols = slice(qb * sub, (qb + 1) * sub)
        m_prev = m_ref[m, :, cols]
        m_new = jnp.maximum(m_prev, jnp.max(s, axis=0, keepdims=True) + shift)
        alpha = jnp.exp2(m_prev - m_new)
        pr = jnp.exp2(s - (m_new - shift))
        l_ref[m, :, cols] = alpha * l_ref[m, :, cols] + jnp.sum(pr, axis=0, keepdims=True)
        acc_ref[m, :, cols] = alpha * acc_ref[m, :, cols]
        m_ref[m, :, cols] = m_new
        return pr.astype(BF16)

    def add_values(j, c, p):
        qb, m = chains[c]
        cols = slice(qb * sub, (qb + 1) * sub)
        acc_ref[m, :, cols] += _dot(vt_ref[j], p)

    n_chains = len(chains)
    for c in range(n_chains):
        produce(0, c)
    p_ref[...] = softmax(0, 0, False)

    def trip(j, far):
        add_values(j, 0, p_ref[...])
        produce(j + 1, 0)
        for c in range(1, n_chains):
            add_values(j, c, softmax(j, c, far))
            produce(j + 1, c)
        p_ref[...] = softmax(j + 1, 0, far)

    n_far = jnp.maximum(qi - 2, 0)
    lax.fori_loop(0, n_far, lambda j, carry: (trip(j, True), carry)[1], 0)
    lax.fori_loop(n_far, qi, lambda j, carry: (trip(j, False), carry)[1], 0)
    add_values(qi, 0, p_ref[...])
    for c in range(1, n_chains):
        add_values(qi, c, softmax(qi, c, False))

    lam = (jnp.exp(jnp.sum(lq1_ref[...] * lk1_ref[...], axis=-1, keepdims=True))
           - jnp.exp(jnp.sum(lq2_ref[...] * lk2_ref[...], axis=-1, keepdims=True))
           + lam_init)
    o = acc_ref[0] * (1.0 / l_ref[0]) - acc_ref[1] * (lam / l_ref[1])
    o = o * lax.rsqrt(jnp.mean(o * o, axis=0, keepdims=True) + 1e-5) * dfn_ref[...]
    y_ref[...] = (o * (1.0 - lam_init)).T.astype(y_ref.dtype)


def _diff_attn(proj3, v_t, rel_bias, lq1, lk1, lq2, lk2, df_norm_col, *, n_heads, col0, tq,
               lam_init):
    B, S, _ = proj3.shape
    d2 = 2 * DF_HEAD_DIM
    assert tq % ATTN_SUB == 0 and ATTN_SUB >= MAX_DISTANCE and S % tq == 0
    nq = S // tq
    n_chains = 2 * (tq // ATTN_SUB)
    cb = col0 // d2
    vec = lambda n: pl.BlockSpec((1, n), lambda b, h, i: (0, 0))
    return pl.pallas_call(
        functools.partial(_diff_attn_kernel, tq=tq, n_heads=n_heads, lam_init=lam_init),
        grid=(B, n_heads, nq),
        in_specs=[
            pl.BlockSpec(memory_space=pltpu.SMEM),
            pl.BlockSpec((None, tq, d2), lambda b, h, i: (b, i, cb + h)),
            pl.BlockSpec((None, S, d2), lambda b, h, i: (b, 0, cb + n_heads + h)),
            pl.BlockSpec((None, nq, d2, tq), lambda b, h, i: (b, 0, h, 0)),
            vec(DF_HEAD_DIM), vec(DF_HEAD_DIM), vec(DF_HEAD_DIM), vec(DF_HEAD_DIM),
            pl.BlockSpec((d2, 1), lambda b, h, i: (0, 0)),
        ],
        out_specs=pl.BlockSpec((None, tq, d2), lambda b, h, i: (b, i, h)),
        out_shape=jax.ShapeDtypeStruct((B, S, n_heads * d2), BF16),
        scratch_shapes=[
            pltpu.VMEM((4, ATTN_SUB, ATTN_SUB), F32),
            pltpu.VMEM((n_chains, tq, ATTN_SUB), F32),
            pltpu.VMEM((tq, ATTN_SUB), BF16),
            pltpu.VMEM((2, 1, tq), F32),
            pltpu.VMEM((2, 1, tq), F32),
            pltpu.VMEM((2, d2, tq), F32),
        ],
        compiler_params=pltpu.CompilerParams(
            dimension_semantics=("arbitrary", "arbitrary", "arbitrary"),
            vmem_limit_bytes=V7X_VMEM_LIMIT),
        name="diff_attn",
    )(rel_bias, proj3, proj3, v_t, lq1, lk1, lq2, lk2, df_norm_col)


def _out_proj_kernel(x_ref, ya_ref, yb_ref, wa_ref, wb_ref, h_ref):
    h_ref[...] = x_ref[...] + _dot(ya_ref[...], wa_ref[...]) + _dot(yb_ref[...], wb_ref[...])


def _out_proj(x2, y_dn, y_df, w_o, *, tm, tn):
    T, D = x2.shape
    ka = y_dn.shape[1]
    kb = y_df.shape[1]
    assert ka == kb
    return pl.pallas_call(
        _out_proj_kernel,
        grid=(T // tm, D // tn),
        in_specs=[
            pl.BlockSpec((tm, tn), lambda i, j: (i, j)),
            pl.BlockSpec((tm, ka), lambda i, j: (i, 0)),
            pl.BlockSpec((tm, kb), lambda i, j: (i, 0)),
            pl.BlockSpec((ka, tn), lambda i, j: (0, j)),
            pl.BlockSpec((kb, tn), lambda i, j: (1, j)),
        ],
        out_specs=pl.BlockSpec((tm, tn), lambda i, j: (i, j)),
        out_shape=jax.ShapeDtypeStruct((T, D), F32),
        compiler_params=pltpu.CompilerParams(
            dimension_semantics=("arbitrary", "arbitrary"),
            vmem_limit_bytes=V7X_VMEM_LIMIT),
        name="out_proj",
    )(x2, y_dn, y_df, w_o, w_o)


def _mlp_kernel(h_ref, g_ref, wu_ref, wd_ref, gf_ref, o_ref, u_ref, acc_ref):
    f = pl.program_id(1)

    @pl.when(f == 0)
    def _():
        x = h_ref[...]
        ms = jnp.mean(x * x, axis=-1, keepdims=True)
        u_ref[...] = (x * lax.rsqrt(ms + 1e-6) * g_ref[...]).astype(BF16)
        acc_ref[...] = jnp.zeros_like(acc_ref)

    hid = jnp.maximum(_dot(u_ref[...], wu_ref[...]), 0.0)
    acc_ref[...] += _dot((hid * hid).astype(BF16), wd_ref[...])

    @pl.when(f == pl.num_programs(1) - 1)
    def _():
        y = h_ref[...] + acc_ref[...]
        ms = jnp.mean(y * y, axis=-1, keepdims=True)
        o_ref[...] = y * lax.rsqrt(ms + 1e-6) * gf_ref[...]


def _mlp(h1, gain, w_up, w_down, final_gain, *, tm, tf):
    T, D = h1.shape
    Fdim = w_up.shape[1]
    return pl.pallas_call(
        _mlp_kernel,
        grid=(T // tm, Fdim // tf),
        in_specs=[
            pl.BlockSpec((tm, D), lambda i, f: (i, 0)),
            pl.BlockSpec((1, D), lambda i, f: (0, 0)),
            pl.BlockSpec((D, tf), lambda i, f: (0, f)),
            pl.BlockSpec((tf, D), lambda i, f: (f, 0)),
            pl.BlockSpec((1, D), lambda i, f: (0, 0)),
        ],
        out_specs=pl.BlockSpec((tm, D), lambda i, f: (i, 0)),
        out_shape=jax.ShapeDtypeStruct((T, D), F32),
        scratch_shapes=[pltpu.VMEM((tm, D), BF16), pltpu.VMEM((tm, D), F32)],
        compiler_params=pltpu.CompilerParams(
            dimension_semantics=("arbitrary", "arbitrary"),
            vmem_limit_bytes=V7X_VMEM_LIMIT),
        name="mlp",
    )(h1, gain, w_up, w_down, final_gain)


def _tile(n, pref):
    if n <= pref:
        return n
    t = pref - pref % 128
    while t > 128 and n % t:
        t -= 128
    assert n % t == 0
    return t


def kernel(x, attn_norm, w_in, conv_w, a_log, dt_bias, dn_norm, lambda_q1, lambda_k1,
           lambda_q2, lambda_k2, df_norm, rel_bias, w_o, mlp_norm, w_up, w_down, final_norm):
    B, S, D = x.shape
    depth = attn_norm.shape[0]
    n_dn = a_log.shape[1]
    n_df = rel_bias.shape[1]
    dn_dim = n_dn * DN_HEAD_DIM
    df_dim = n_df * 2 * DF_HEAD_DIM
    T = B * S
    gate0 = 4 * dn_dim
    assert w_in.shape[2] == gate0 + 2 * n_dn + 3 * df_dim and 2 * n_dn <= GATE_COLS

    assert depth == 1
    l = 0
    h = x.reshape(T, D)

    wl = w_in[l]
    dfq0 = gate0 + 2 * n_dn
    dfv0 = dfq0 + 2 * df_dim
    w_a = wl[:, :gate0].astype(BF16)
    w_b = wl[:, dfq0:dfv0].astype(BF16)
    w_gate = wl[:, gate0:gate0 + GATE_COLS].astype(BF16)
    w_vt = wl[:, dfv0:].T.astype(BF16)
    w_o_bf = w_o[l].astype(BF16)
    w_up_bf = w_up[l].astype(BF16)
    w_down_bf = w_down[l].astype(BF16)

    proj, gates, v_t = _in_proj(h, attn_norm[l][None, :], w_a, w_b, w_gate, w_vt, conv_w[l],
                                batch=B, dn_dim=dn_dim, df_dim=df_dim, tm=_tile(S, IN_PROJ_ROWS),
                                tn=_tile(math.gcd(dn_dim, df_dim), IN_PROJ_COLS),
                                tv=_tile(S, ATTN_BLOCK),
                                q_scale=DF_HEAD_DIM ** -0.5 * math.log2(math.e))
    proj3 = proj.reshape(B, S, -1)
    gate3 = gates.reshape(B, S, GATE_COLS)

    gate_pad = ((0, 0), (n_dn, GATE_COLS - 2 * n_dn))
    alog_row = jnp.pad(a_log[l][None, :], gate_pad)
    dtb_row = jnp.pad(dt_bias[l][None, :], gate_pad)
    y_dn = _gdn(proj3, gate3, alog_row, dtb_row, dn_norm[l][None, :],
                n_heads=n_dn, blk=_tile(S, GDN_BLOCK))
    lam_init = 0.8 - 0.6 * math.exp(-0.3 * l)
    y_df = _diff_attn(proj3, v_t, rel_bias.reshape(-1),
                      lambda_q1[l][None, :], lambda_k1[l][None, :],
                      lambda_q2[l][None, :], lambda_k2[l][None, :], df_norm[l][:, None],
                      n_heads=n_df, col0=gate0, tq=_tile(S, ATTN_BLOCK), lam_init=lam_init)

    h1 = _out_proj(h, y_dn.reshape(T, dn_dim), y_df.reshape(T, df_dim), w_o_bf,
                   tm=_tile(T, 512), tn=_tile(D, 2048))
    out = _mlp(h1, mlp_norm[l][None, :], w_up_bf, w_down_bf, final_norm[None, :],
               tm=_tile(T, 512), tf=_tile(w_up_bf.shape[1], 1024))
    return out.reshape(B, S, D)
```

```python
import functools
import math

import numpy as np
import jax
import jax.numpy as jnp
from jax import lax
from jax.experimental import pallas as pl
from jax.experimental.pallas import tpu as pltpu

F32 = jnp.float32
BF16 = jnp.bfloat16

DN_HEAD_DIM = 128
DF_HEAD_DIM = 128
CONV_WIDTH = 4
CHUNK = 64
NUM_BUCKETS = 32
MAX_DISTANCE = 128
GATE_COLS = 128
IN_PROJ_ROWS = 1024
IN_PROJ_COLS = 1024
IN_PROJ_SUB = 256
GDN_BLOCK = 512
GDN_GROUP = 4
ATTN_SUB = 256
ATTN_BLOCK = 512

V7X_VMEM_LIMIT = 58 * 1024 * 1024


def _dot(a, b):
    return jnp.dot(a, b, preferred_element_type=F32)


def _dot_nt(a, b):
    return lax.dot_general(a, b, (((1,), (1,)), ((), ())), preferred_element_type=F32)


def _sigmoid(x):
    return 1.0 / (1.0 + jnp.exp(-x))


def _in_proj_kernel(x_ref, g_ref, wa_ref, wb_ref, wg_ref, wvt_ref, cw_ref, proj_ref, gate_ref,
                    vt_ref, u_ref, hist_ref, cbuf_ref, *, bounds, per_seq, q_scale):
    i = pl.program_id(0)
    j = pl.program_id(1)
    tm, tn = proj_ref.shape
    b0, b1, b2, b3, b4, b5 = bounds

    @pl.when(j == 0)
    def _():
        x = x_ref[...]
        ms = jnp.mean(x * x, axis=-1, keepdims=True)
        u = (x * lax.rsqrt(ms + 1e-6) * g_ref[...]).astype(BF16)
        u_ref[...] = u
        gate_ref[...] = _dot(u, wg_ref[...])

    n_col_chunks = tn // IN_PROJ_SUB
    first_of_seq = (i % per_seq) == 0

    def chunk_cols(c):
        return slice(c * IN_PROJ_SUB, (c + 1) * IN_PROJ_SUB)

    def raw_cols(w_ref, c):
        cbuf_ref[8:8 + tm, chunk_cols(c)] = _dot(u_ref[...], w_ref[:, chunk_cols(c)])

    def raw(c):
        return cbuf_ref[8:8 + tm, chunk_cols(c)]

    def conv_silu(c):
        cw = cw_ref[:, chunk_cols(c)]
        y = None
        for s in range(CONV_WIDTH):
            tap = CONV_WIDTH - 1 - s
            term = cbuf_ref[8 - s:8 - s + tm, chunk_cols(c)] * cw[tap:tap + 1, :]
            y = term if y is None else y + term
        return y * _sigmoid(y)

    def l2norm_heads(y, scale):
        outs = []
        for h in range(IN_PROJ_SUB // DN_HEAD_DIM):
            yh = y[:, h * DN_HEAD_DIM:(h + 1) * DN_HEAD_DIM]
            outs.append(yh * (lax.rsqrt(jnp.sum(yh * yh, axis=-1, keepdims=True) + 1e-6) * scale))
        return jnp.concatenate(outs, axis=1)

    def silu_cols(c):
        z = raw(c)
        return z * _sigmoid(z)

    def project(w_ref, epilogue, conv=False):
        if conv:
            cbuf_ref[0:8, :] = jnp.where(first_of_seq, 0.0, hist_ref[j])
        raw_cols(w_ref, 0)
        for c in range(n_col_chunks):
            if c + 1 < n_col_chunks:
                raw_cols(w_ref, c + 1)
            proj_ref[:, chunk_cols(c)] = epilogue(c).astype(BF16)
        if conv:
            hist_ref[j] = cbuf_ref[tm:tm + 8, :]

    @pl.when(j < b0)
    def _():
        project(wa_ref, lambda c: l2norm_heads(conv_silu(c), DN_HEAD_DIM ** -0.5), conv=True)

    @pl.when((j >= b0) & (j < b1))
    def _():
        project(wa_ref, lambda c: l2norm_heads(conv_silu(c), 1.0), conv=True)

    @pl.when((j >= b1) & (j < b2))
    def _():
        project(wa_ref, conv_silu, conv=True)

    @pl.when((j >= b2) & (j < b3))
    def _():
        project(wa_ref, silu_cols)

    @pl.when((j >= b3) & (j < b4))
    def _():
        project(wb_ref, lambda c: raw(c) * q_scale)

    @pl.when((j >= b4) & (j < b5))
    def _():
        project(wb_ref, raw)

    @pl.when(j >= b5)
    def _():
        vt = _dot_nt(wvt_ref[...], u_ref[...]).astype(BF16)
        tv = vt_ref.shape[-1]
        for c in range(vt_ref.shape[0]):
            vt_ref[c] = vt[:, c * tv:(c + 1) * tv]


def _in_proj(x2, gain, w_a, w_b, w_gate, w_vt, conv_w, *, batch, dn_dim, df_dim, tm, tn, tv,
             q_scale):
    T, D = x2.shape
    N = w_a.shape[1] + w_b.shape[1]
    nv = w_vt.shape[0]
    S = T // batch
    assert w_a.shape[1] == 4 * dn_dim and w_b.shape[1] == 2 * df_dim and nv == df_dim
    assert S % tm == 0 and dn_dim % tn == 0 and df_dim % tn == 0 and tm % tv == 0
    assert tn % DN_HEAD_DIM == 0 and conv_w.shape == (CONV_WIDTH, 3 * dn_dim)
    assert tn % IN_PROJ_SUB == 0 and IN_PROJ_SUB % DN_HEAD_DIM == 0
    n_dn, n_df = dn_dim // tn, df_dim // tn
    bounds = (n_dn, 2 * n_dn, 3 * n_dn, 4 * n_dn, 4 * n_dn + n_df, 4 * n_dn + 2 * n_df)
    n_main = bounds[-1]
    n_conv = bounds[2]
    n_a = bounds[3]
    n_b = n_main - n_a
    per_seq = S // tm
    return pl.pallas_call(
        functools.partial(_in_proj_kernel, bounds=bounds, per_seq=per_seq, q_scale=q_scale),
        grid=(T // tm, n_main + nv // tn),
        in_specs=[
            pl.BlockSpec((tm, D), lambda i, j: (i, 0)),
            pl.BlockSpec((1, D), lambda i, j: (0, 0)),
            pl.BlockSpec((D, tn), lambda i, j: (0, jnp.where(j < n_a, j, 0))),
            pl.BlockSpec((D, tn), lambda i, j: (0, jnp.where(j < n_a, n_b - 1,
                                                             jnp.minimum(j - n_a, n_b - 1)))),
            pl.BlockSpec((D, GATE_COLS), lambda i, j: (0, 0), pipeline_mode=pl.Buffered(1)),
            pl.BlockSpec((tn, D), lambda i, j: (jnp.maximum(j - n_main, 0), 0),
                         pipeline_mode=pl.Buffered(1 if nv == tn else 2)),
            pl.BlockSpec((CONV_WIDTH, tn), lambda i, j: (0, jnp.minimum(j, n_conv - 1))),
        ],
        out_specs=[
            pl.BlockSpec((tm, tn), lambda i, j: (i, jnp.minimum(j, n_main - 1))),
            pl.BlockSpec((tm, GATE_COLS), lambda i, j: (i, 0)),
            pl.BlockSpec((None, tm // tv, tn, tv),
                         lambda i, j: (i // per_seq, i % per_seq, jnp.maximum(j - n_main, 0), 0)),
        ],
        out_shape=[
            jax.ShapeDtypeStruct((T, N), BF16),
            jax.ShapeDtypeStruct((T, GATE_COLS), F32),
            jax.ShapeDtypeStruct((batch, S // tv, nv, tv), BF16),
        ],
        scratch_shapes=[
            pltpu.VMEM((tm, D), BF16),
            pltpu.VMEM((n_conv, 8, tn), F32),
            pltpu.VMEM((8 + tm, tn), F32),
        ],
        compiler_params=pltpu.CompilerParams(
            dimension_semantics=("arbitrary", "arbitrary"),
            vmem_limit_bytes=V7X_VMEM_LIMIT),
        name="in_proj",
    )(x2, gain, w_a, w_b, w_gate, w_vt, conv_w)


def _gdn_kernel(q_ref, k_ref, v_ref, z_ref, gate_ref, alog_ref, dtb_ref, dnn_ref,
                y_ref, state_ref, *, blk, n_heads):
    dk = DN_HEAD_DIM
    n_chunks = blk // CHUNK

    @pl.when(pl.program_id(1) == 0)
    def _():
        state_ref[...] = jnp.zeros_like(state_ref)

    gate = gate_ref[...]
    beta_all = _sigmoid(gate)
    xs = gate + dtb_ref[...]
    softplus = jnp.maximum(xs, 0.0) + jnp.log(1.0 + jnp.exp(-jnp.abs(xs)))
    g_all = -jnp.exp(alog_ref[...]) * softplus
    pos = lax.broadcasted_iota(jnp.int32, g_all.shape, 0) & (CHUNK - 1)
    gc_all = g_all
    step = 1
    while step < CHUNK:
        gc_all = gc_all + jnp.where(pos >= step, pltpu.roll(gc_all, step, 0), 0.0)
        step *= 2

    ri = lax.broadcasted_iota(jnp.int32, (CHUNK, CHUNK), 0)
    ci = lax.broadcasted_iota(jnp.int32, (CHUNK, CHUNK), 1)
    tril = ri >= ci
    strict = ri > ci
    eye = ri == ci
    gain = dnn_ref[...]

    def first_stage(c):
        rows = slice(c * CHUNK, (c + 1) * CHUNK)
        items = []
        for h in range(n_heads):
            cols = slice(h * dk, (h + 1) * dk)
            q_bf = q_ref[rows, cols]
            k_bf = k_ref[rows, cols]
            qc = q_bf.astype(F32)
            kc = k_bf.astype(F32)
            vc = v_ref[rows, cols].astype(F32)
            bc = jnp.broadcast_to(beta_all[rows, h:h + 1], (CHUNK, dk))
            gcc = jnp.broadcast_to(gc_all[rows, n_heads + h:n_heads + h + 1], (CHUNK, dk))
            g_last = gcc[CHUNK - 1:CHUNK, :]
            eg = jnp.exp(gcc)
            g_sq = gcc[:, 0:CHUNK]
            g_row = jnp.sum(jnp.where(eye, g_sq, 0.0), axis=0, keepdims=True)
            decay = jnp.exp(jnp.where(tril, g_sq - g_row, -jnp.inf))
            kb = kc * bc
            lhs = jnp.concatenate([kb.astype(BF16), q_bf], axis=0)
            aq = _dot_nt(lhs, k_bf)
            n_mat = jnp.where(strict, -(aq[:CHUNK] * decay), 0.0)
            items.append(dict(
                h=h, qe=qc * eg, g_last=g_last,
                rhs=jnp.concatenate([kb * eg, vc * bc], axis=1).astype(BF16),
                qk=jnp.where(tril, aq[CHUNK:] * decay, 0.0).astype(BF16),
                kd_t=(kc * jnp.exp(g_last - gcc)).T.astype(BF16),
                power=n_mat, inv=jnp.where(eye, 1.0, 0.0) + n_mat))
        return items

    def matrix_stages(items):
        span = 2
        while span < CHUNK:
            for s in items:
                pb = s["power"].astype(BF16)
                s["power"] = _dot(pb, pb)
            for s in items:
                s["inv"] = s["inv"] + _dot(s["inv"].astype(BF16), s["power"].astype(BF16))
            span *= 2
        for s in items:
            s["wu"] = _dot(s["inv"].astype(BF16), s["rhs"]).astype(BF16)
        for s in items:
            s["gr"] = _dot(s["kd_t"], s["wu"])
            qw = _dot(s["qk"], s["wu"])
            s["q_eff"] = (s["qe"] - qw[:, :dk]).astype(BF16)
            s["p_loc"] = qw[:, dk:]

    def state_stage(c, items):
        rows = slice(c * CHUNK, (c + 1) * CHUNK)
        for s in items:
            h = s["h"]
            state = state_ref[h]
            s_bf = state.astype(BF16)
            o = _dot(s["q_eff"], s_bf) + s["p_loc"]
            state_ref[h] = (state * jnp.exp(s["g_last"])
                            - _dot(s["gr"][:, :dk].astype(BF16), s_bf) + s["gr"][:, dk:])
            zs = z_ref[rows, h * dk:(h + 1) * dk].astype(F32)
            o = o * lax.rsqrt(jnp.mean(o * o, axis=-1, keepdims=True) + 1e-6) * gain
            y_ref[rows, h * dk:(h + 1) * dk] = (o * zs).astype(y_ref.dtype)

    groups = [list(range(g, min(g + GDN_GROUP, n_chunks))) for g in range(0, n_chunks, GDN_GROUP)]
    cur = [first_stage(c) for c in groups[0]]
    for gi, chunk_ids in enumerate(groups):
        nxt = [first_stage(c) for c in groups[gi + 1]] if gi + 1 < len(groups) else None
        matrix_stages([s for items in cur for s in items])
        for c, items in zip(chunk_ids, cur):
            state_stage(c, items)
        cur = nxt


def _gdn(proj3, gate3, alog_row, dtb_row, dn_norm, *, n_heads, blk):
    B, S, _ = proj3.shape
    dk = DN_HEAD_DIM
    dn = n_heads * dk

    def group(idx):
        return pl.BlockSpec((None, blk, dn), lambda b, t: (b, t, idx))

    def whole(arr):
        return pl.BlockSpec(arr.shape, lambda b, t: (0,) * arr.ndim)

    return pl.pallas_call(
        functools.partial(_gdn_kernel, blk=blk, n_heads=n_heads),
        grid=(B, S // blk),
        in_specs=[
            group(0), group(1), group(2), group(3),
            pl.BlockSpec((None, blk, GATE_COLS), lambda b, t: (b, t, 0)),
            whole(alog_row), whole(dtb_row), whole(dn_norm),
        ],
        out_specs=pl.BlockSpec((None, blk, dn), lambda b, t: (b, t, 0)),
        out_shape=jax.ShapeDtypeStruct((B, S, dn), BF16),
        scratch_shapes=[pltpu.VMEM((n_heads, dk, dk), F32)],
        compiler_params=pltpu.CompilerParams(
            dimension_semantics=("arbitrary", "arbitrary"),
            vmem_limit_bytes=V7X_VMEM_LIMIT),
        name="gdn",
    )(proj3, proj3, proj3, proj3, gate3, alog_row, dtb_row, dn_norm)


def _t5_bucket_starts():
    max_exact = NUM_BUCKETS // 2
    n = np.arange(0, MAX_DISTANCE + 1)
    nf = np.maximum(n, 1).astype(np.float32)
    large = max_exact + (np.log(nf / max_exact) / math.log(MAX_DISTANCE / max_exact)
                         * (NUM_BUCKETS - max_exact)).astype(np.int32)
    bucket = np.where(n < max_exact, n, np.minimum(large, NUM_BUCKETS - 1))
    assert bucket[MAX_DISTANCE] == NUM_BUCKETS - 1 and np.all(np.diff(bucket) >= 0)
    starts = [(0, int(bucket[0]))]
    for d in range(1, MAX_DISTANCE + 1):
        if bucket[d] != bucket[d - 1]:
            starts.append((d, int(bucket[d])))
    return starts


def _diff_attn_kernel(rb_ref, q_ref, k_ref, vt_ref, lq1_ref, lk1_ref, lq2_ref, lk2_ref, dfn_ref,
                      y_ref, bias_ref, s_ref, p_ref, m_ref, l_ref, acc_ref,
                      *, tq, n_heads, lam_init):
    h = pl.program_id(1)
    qi = pl.program_id(2)
    d = DF_HEAD_DIM
    sub = ATTN_SUB
    n_sub = tq // sub
    log2e = math.log2(math.e)

    @pl.when(qi == 0)
    def _():
        keys = lax.broadcasted_iota(jnp.int32, (sub, sub), 0)
        qrys = lax.broadcasted_iota(jnp.int32, (sub, sub), 1)
        starts = _t5_bucket_starts()
        for idx in range(2):
            dist = qrys - keys + idx * sub
            tile = jnp.full((sub, sub), rb_ref[starts[0][1] * n_heads + h], F32)
            for first, bucket in starts[1:]:
                tile = jnp.where(dist >= first, rb_ref[bucket * n_heads + h], tile)
            tile = tile * log2e
            if idx == 0:
                tile = jnp.where(dist >= 0, tile, -jnp.inf)
            bias_ref[idx] = tile
        bias_ref[2] = jnp.full((sub, sub), rb_ref[(NUM_BUCKETS - 1) * n_heads + h] * log2e, F32)
        bias_ref[3] = jnp.full((sub, sub), -jnp.inf, F32)

    m_ref[...] = jnp.full(m_ref.shape, -jnp.inf, F32)
    l_ref[...] = jnp.zeros_like(l_ref)
    acc_ref[...] = jnp.zeros_like(acc_ref)

    chains = [(qb, m) for qb in range(n_sub) for m in range(2)]

    far_shift = rb_ref[(NUM_BUCKETS - 1) * n_heads + h] * log2e

    def produce(j, c):
        qb, m = chains[c]
        k0 = pl.multiple_of(j * tq, tq)
        s_ref[c] = _dot_nt(k_ref[pl.ds(k0, tq), m * d:(m + 1) * d],
                           q_ref[qb * sub:(qb + 1) * sub, m * d:(m + 1) * d])

    def softmax(j, c, far):
        qb, m = chains[c]
        if far:
            s = s_ref[c]
            shift = far_shift
        else:
            parts = []
            for kb in range(n_sub):
                off = (qi - j) * n_sub + (qb - kb)
                idx = jnp.where(off < 0, 3, jnp.minimum(off, 2))
                parts.append(s_ref[c, kb * sub:(kb + 1) * sub, :] + bias_ref[idx])
            s = jnp.concatenate(parts, axis=0)
            shift = 0.0
        cols = slice(qb * sub, (qb + 1) * sub)
        m_prev = m_ref[m, :, cols]
        m_new = jnp.maximum(m_prev, jnp.max(s, axis=0, keepdims=True) + shift)
        alpha = jnp.exp2(m_prev - m_new)
        pr = jnp.exp2(s - (m_new - shift))
        l_ref[m, :, cols] = alpha * l_ref[m, :, cols] + jnp.sum(pr, axis=0, keepdims=True)
        acc_ref[m, :, cols] = alpha * acc_ref[m, :, cols]
        m_ref[m, :, cols] = m_new
        return pr.astype(BF16)

    def add_values(j, c, p):
        qb, m = chains[c]
        cols = slice(qb * sub, (qb + 1) * sub)
        acc_ref[m, :, cols] += _dot(vt_ref[j], p)

    n_chains = len(chains)
    for c in range(n_chains):
        produce(0, c)
    p_ref[...] = softmax(0, 0, False)

    def trip(j, far):
        add_values(j, 0, p_ref[...])
        produce(j + 1, 0)
        for c in range(1, n_chains):
            add_values(j, c, softmax(j, c, far))
            produce(j + 1, c)
        p_ref[...] = softmax(j + 1, 0, far)

    n_far = jnp.maximum(qi - 2, 0)
    lax.fori_loop(0, n_far, lambda j, carry: (trip(j, True), carry)[1], 0)
    lax.fori_loop(n_far, qi, lambda j, carry: (trip(j, False), carry)[1], 0)
    add_values(qi, 0, p_ref[...])
    for c in range(1, n_chains):
        add_values(qi, c, softmax(qi, c, False))

    lam = (jnp.exp(jnp.sum(lq1_ref[...] * lk1_ref[...], axis=-1, keepdims=True))
           - jnp.exp(jnp.sum(lq2_ref[...] * lk2_ref[...], axis=-1, keepdims=True))
           + lam_init)
    o = acc_ref[0] * (1.0 / l_ref[0]) - acc_ref[1] * (lam / l_ref[1])
    o = o * lax.rsqrt(jnp.mean(o * o, axis=0, keepdims=True) + 1e-5) * dfn_ref[...]
    y_ref[...] = (o * (1.0 - lam_init)).T.astype(y_ref.dtype)


def _diff_attn(proj3, v_t, rel_bias, lq1, lk1, lq2, lk2, df_norm_col, *, n_heads, col0, tq,
               lam_init):
    B, S, _ = proj3.shape
    d2 = 2 * DF_HEAD_DIM
    assert tq % ATTN_SUB == 0 and ATTN_SUB >= MAX_DISTANCE and S % tq == 0
    nq = S // tq
    n_chains = 2 * (tq // ATTN_SUB)
    cb = col0 // d2
    vec = lambda n: pl.BlockSpec((1, n), lambda b, h, i: (0, 0))
    return pl.pallas_call(
        functools.partial(_diff_attn_kernel, tq=tq, n_heads=n_heads, lam_init=lam_init),
        grid=(B, n_heads, nq),
        in_specs=[
            pl.BlockSpec(memory_space=pltpu.SMEM),
            pl.BlockSpec((None, tq, d2), lambda b, h, i: (b, i, cb + h)),
            pl.BlockSpec((None, S, d2), lambda b, h, i: (b, 0, cb + n_heads + h)),
            pl.BlockSpec((None, nq, d2, tq), lambda b, h, i: (b, 0, h, 0)),
            vec(DF_HEAD_DIM), vec(DF_HEAD_DIM), vec(DF_HEAD_DIM), vec(DF_HEAD_DIM),
            pl.BlockSpec((d2, 1), lambda b, h, i: (0, 0)),
        ],
        out_specs=pl.BlockSpec((None, tq, d2), lambda b, h, i: (b, i, h)),
        out_shape=jax.ShapeDtypeStruct((B, S, n_heads * d2), BF16),
        scratch_shapes=[
            pltpu.VMEM((4, ATTN_SUB, ATTN_SUB), F32),
            pltpu.VMEM((n_chains, tq, ATTN_SUB), F32),
            pltpu.VMEM((tq, ATTN_SUB), BF16),
            pltpu.VMEM((2, 1, tq), F32),
            pltpu.VMEM((2, 1, tq), F32),
            pltpu.VMEM((2, d2, tq), F32),
        ],
        compiler_params=pltpu.CompilerParams(
            dimension_semantics=("arbitrary", "arbitrary", "arbitrary"),
            vmem_limit_bytes=V7X_VMEM_LIMIT),
        name="diff_attn",
    )(rel_bias, proj3, proj3, v_t, lq1, lk1, lq2, lk2, df_norm_col)


def _out_proj_kernel(x_ref, ya_ref, yb_ref, wa_ref, wb_ref, h_ref):
    h_ref[...] = x_ref[...] + _dot(ya_ref[...], wa_ref[...]) + _dot(yb_ref[...], wb_ref[...])


def _out_proj(x2, y_dn, y_df, w_o, *, tm, tn):
    T, D = x2.shape
    ka = y_dn.shape[1]
    kb = y_df.shape[1]
    assert ka == kb
    return pl.pallas_call(
        _out_proj_kernel,
        grid=(T // tm, D // tn),
        in_specs=[
            pl.BlockSpec((tm, tn), lambda i, j: (i, j)),
            pl.BlockSpec((tm, ka), lambda i, j: (i, 0)),
            pl.BlockSpec((tm, kb), lambda i, j: (i, 0)),
            pl.BlockSpec((ka, tn), lambda i, j: (0, j)),
            pl.BlockSpec((kb, tn), lambda i, j: (1, j)),
        ],
        out_specs=pl.BlockSpec((tm, tn), lambda i, j: (i, j)),
        out_shape=jax.ShapeDtypeStruct((T, D), F32),
        compiler_params=pltpu.CompilerParams(
            dimension_semantics=("arbitrary", "arbitrary"),
            vmem_limit_bytes=V7X_VMEM_LIMIT),
        name="out_proj",
    )(x2, y_dn, y_df, w_o, w_o)


def _mlp_kernel(h_ref, g_ref, wu_ref, wd_ref, gf_ref, o_ref, u_ref, acc_ref):
    f = pl.program_id(1)

    @pl.when(f == 0)
    def _():
        x = h_ref[...]
        ms = jnp.mean(x * x, axis=-1, keepdims=True)
        u_ref[...] = (x * lax.rsqrt(ms + 1e-6) * g_ref[...]).astype(BF16)
        acc_ref[...] = jnp.zeros_like(acc_ref)

    hid = jnp.maximum(_dot(u_ref[...], wu_ref[...]), 0.0)
    acc_ref[...] += _dot((hid * hid).astype(BF16), wd_ref[...])

    @pl.when(f == pl.num_programs(1) - 1)
    def _():
        y = h_ref[...] + acc_ref[...]
        ms = jnp.mean(y * y, axis=-1, keepdims=True)
        o_ref[...] = y * lax.rsqrt(ms + 1e-6) * gf_ref[...]


def _mlp(h1, gain, w_up, w_down, final_gain, *, tm, tf):
    T, D = h1.shape
    Fdim = w_up.shape[1]
    return pl.pallas_call(
        _mlp_kernel,
        grid=(T // tm, Fdim // tf),
        in_specs=[
            pl.BlockSpec((tm, D), lambda i, f: (i, 0)),
            pl.BlockSpec((1, D), lambda i, f: (0, 0)),
            pl.BlockSpec((D, tf), lambda i, f: (0, f)),
            pl.BlockSpec((tf, D), lambda i, f: (f, 0)),
            pl.BlockSpec((1, D), lambda i, f: (0, 0)),
        ],
        out_specs=pl.BlockSpec((tm, D), lambda i, f: (i, 0)),
        out_shape=jax.ShapeDtypeStruct((T, D), F32),
        scratch_shapes=[pltpu.VMEM((tm, D), BF16), pltpu.VMEM((tm, D), F32)],
        compiler_params=pltpu.CompilerParams(
            dimension_semantics=("arbitrary", "arbitrary"),
            vmem_limit_bytes=V7X_VMEM_LIMIT),
        name="mlp",
    )(h1, gain, w_up, w_down, final_gain)


def _tile(n, pref):
    if n <= pref:
        return n
    t = pref - pref % 128
    while t > 128 and n % t:
        t -= 128
    assert n % t == 0
    return t


def kernel(x, attn_norm, w_in, conv_w, a_log, dt_bias, dn_norm, lambda_q1, lambda_k1,
           lambda_q2, lambda_k2, df_norm, rel_bias, w_o, mlp_norm, w_up, w_down, final_norm):
    B, S, D = x.shape
    depth = attn_norm.shape[0]
    n_dn = a_log.shape[1]
    n_df = rel_bias.shape[1]
    dn_dim = n_dn * DN_HEAD_DIM
    df_dim = n_df * 2 * DF_HEAD_DIM
    T = B * S
    gate0 = 4 * dn_dim
    assert w_in.shape[2] == gate0 + 2 * n_dn + 3 * df_dim and 2 * n_dn <= GATE_COLS

    assert depth == 1
    l = 0
    h = x.reshape(T, D)

    wl = w_in[l]
    dfq0 = gate0 + 2 * n_dn
    dfv0 = dfq0 + 2 * df_dim
    w_a = wl[:, :gate0].astype(BF16)
    w_b = wl[:, dfq0:dfv0].astype(BF16)
    w_gate = wl[:, gate0:gate0 + GATE_COLS].astype(BF16)
    w_vt = wl[:, dfv0:].T.astype(BF16)
    w_o_bf = w_o[l].astype(BF16)
    w_up_bf = w_up[l].astype(BF16)
    w_down_bf = w_down[l].astype(BF16)

    proj, gates, v_t = _in_proj(h, attn_norm[l][None, :], w_a, w_b, w_gate, w_vt, conv_w[l],
                                batch=B, dn_dim=dn_dim, df_dim=df_dim, tm=_tile(S, IN_PROJ_ROWS),
                                tn=_tile(math.gcd(dn_dim, df_dim), IN_PROJ_COLS),
                                tv=_tile(S, ATTN_BLOCK),
                                q_scale=DF_HEAD_DIM ** -0.5 * math.log2(math.e))
    proj3 = proj.reshape(B, S, -1)
    gate3 = gates.reshape(B, S, GATE_COLS)

    gate_pad = ((0, 0), (n_dn, GATE_COLS - 2 * n_dn))
    alog_row = jnp.pad(a_log[l][None, :], gate_pad)
    dtb_row = jnp.pad(dt_bias[l][None, :], gate_pad)
    y_dn = _gdn(proj3, gate3, alog_row, dtb_row, dn_norm[l][None, :],
                n_heads=n_dn, blk=_tile(S, GDN_BLOCK))
    lam_init = 0.8 - 0.6 * math.exp(-0.3 * l)
    y_df = _diff_attn(proj3, v_t, rel_bias.reshape(-1),
                      lambda_q1[l][None, :], lambda_k1[l][None, :],
                      lambda_q2[l][None, :], lambda_k2[l][None, :], df_norm[l][:, None],
                      n_heads=n_df, col0=gate0, tq=_tile(S, ATTN_BLOCK), lam_init=lam_init)

    h1 = _out_proj(h, y_dn.reshape(T, dn_dim), y_df.reshape(T, df_dim), w_o_bf,
                   tm=_tile(T, 512), tn=_tile(D, 2048))
    out = _mlp(h1, mlp_norm[l][None, :], w_up_bf, w_down_bf, final_norm[None, :],
               tm=_tile(T, 512), tf=_tile(w_up_bf.shape[1], 1024))
    return out.reshape(B, S, D)
```

```python
import functools
import math

import numpy as np
import jax
import jax.numpy as jnp
from jax import lax
from jax.experimental import pallas as pl
from jax.experimental.pallas import tpu as pltpu

F32 = jnp.float32
BF16 = jnp.bfloat16

DN_HEAD_DIM = 128
DF_HEAD_DIM = 128
CONV_WIDTH = 4
CHUNK = 64
NUM_BUCKETS = 32
MAX_DISTANCE = 128
GATE_COLS = 128
IN_PROJ_ROWS = 1024
IN_PROJ_COLS = 1024
IN_PROJ_SUB = 256
GDN_BLOCK = 512
GDN_GROUP = 4
ATTN_SUB = 256
ATTN_BLOCK = 512

V7X_VMEM_LIMIT = 58 * 1024 * 1024


def _dot(a, b):
    return jnp.dot(a, b, preferred_element_type=F32)


def _dot_nt(a, b):
    return lax.dot_general(a, b, (((1,), (1,)), ((), ())), preferred_element_type=F32)


def _sigmoid(x):
    return 1.0 / (1.0 + jnp.exp(-x))


def _in_proj_kernel(x_ref, g_ref, wa_ref, wb_ref, wg_ref, wvt_ref, cw_ref, proj_ref, gate_ref,
                    vt_ref, u_ref, hist_ref, cbuf_ref, *, bounds, per_seq, q_scale):
    i = pl.program_id(0)
    j = pl.program_id(1)
    tm, tn = proj_ref.shape
    b0, b1, b2, b3, b4, b5 = bounds

    @pl.when(j == 0)
    def _():
        x = x_ref[...]
        ms = jnp.mean(x * x, axis=-1, keepdims=True)
        u = (x * lax.rsqrt(ms + 1e-6) * g_ref[...]).astype(BF16)
        u_ref[...] = u
        gate_ref[...] = _dot(u, wg_ref[...])

    n_col_chunks = tn // IN_PROJ_SUB
    first_of_seq = (i % per_seq) == 0

    def chunk_cols(c):
        return slice(c * IN_PROJ_SUB, (c + 1) * IN_PROJ_SUB)

    def raw_cols(w_ref, c):
        cbuf_ref[8:8 + tm, chunk_cols(c)] = _dot(u_ref[...], w_ref[:, chunk_cols(c)])

    def raw(c):
        return cbuf_ref[8:8 + tm, chunk_cols(c)]

    def conv_silu(c):
        cw = cw_ref[:, chunk_cols(c)]
        y = None
        for s in range(CONV_WIDTH):
            tap = CONV_WIDTH - 1 - s
            term = cbuf_ref[8 - s:8 - s + tm, chunk_cols(c)] * cw[tap:tap + 1, :]
            y = term if y is None else y + term
        return y * _sigmoid(y)

    def l2norm_heads(y, scale):
        outs = []
        for h in range(IN_PROJ_SUB // DN_HEAD_DIM):
            yh = y[:, h * DN_HEAD_DIM:(h + 1) * DN_HEAD_DIM]
            outs.append(yh * (lax.rsqrt(jnp.sum(yh * yh, axis=-1, keepdims=True) + 1e-6) * scale))
        return jnp.concatenate(outs, axis=1)

    def silu_cols(c):
        z = raw(c)
        return z * _sigmoid(z)

    def project(w_ref, epilogue, conv=False):
        if conv:
            cbuf_ref[0:8, :] = jnp.where(first_of_seq, 0.0, hist_ref[j])
        raw_cols(w_ref, 0)
        for c in range(n_col_chunks):
            if c + 1 < n_col_chunks:
                raw_cols(w_ref, c + 1)
            proj_ref[:, chunk_cols(c)] = epilogue(c).astype(BF16)
        if conv:
            hist_ref[j] = cbuf_ref[tm:tm + 8, :]

    @pl.when(j < b0)
    def _():
        project(wa_ref, lambda c: l2norm_heads(conv_silu(c), DN_HEAD_DIM ** -0.5), conv=True)

    @pl.when((j >= b0) & (j < b1))
    def _():
        project(wa_ref, lambda c: l2norm_heads(conv_silu(c), 1.0), conv=True)

    @pl.when((j >= b1) & (j < b2))
    def _():
        project(wa_ref, conv_silu, conv=True)

    @pl.when((j >= b2) & (j < b3))
    def _():
        project(wa_ref, silu_cols)

    @pl.when((j >= b3) & (j < b4))
    def _():
        project(wb_ref, lambda c: raw(c) * q_scale)

    @pl.when((j >= b4) & (j < b5))
    def _():
        project(wb_ref, raw)

    @pl.when(j >= b5)
    def _():
        vt = _dot_nt(wvt_ref[...], u_ref[...]).astype(BF16)
        tv = vt_ref.shape[-1]
        for c in range(vt_ref.shape[0]):
            vt_ref[c] = vt[:, c * tv:(c + 1) * tv]


def _in_proj(x2, gain, w_all, w_b, w_vt, conv_w, *, batch, dn_dim, df_dim, tm, tn, tv, q_scale):
    T, D = x2.shape
    N = 4 * dn_dim + w_b.shape[1]
    nv = w_vt.shape[0]
    S = T // batch
    assert w_b.shape[1] == 2 * df_dim and nv == df_dim
    assert (4 * dn_dim) % GATE_COLS == 0 and w_all.shape[1] >= 4 * dn_dim + GATE_COLS
    assert S % tm == 0 and dn_dim % tn == 0 and df_dim % tn == 0 and tm % tv == 0
    assert tn % DN_HEAD_DIM == 0 and conv_w.shape == (CONV_WIDTH, 3 * dn_dim)
    assert tn % IN_PROJ_SUB == 0 and IN_PROJ_SUB % DN_HEAD_DIM == 0
    n_dn, n_df = dn_dim // tn, df_dim // tn
    bounds = (n_dn, 2 * n_dn, 3 * n_dn, 4 * n_dn, 4 * n_dn + n_df, 4 * n_dn + 2 * n_df)
    n_main = bounds[-1]
    n_conv = bounds[2]
    n_a = bounds[3]
    n_b = n_main - n_a
    per_seq = S // tm
    return pl.pallas_call(
        functools.partial(_in_proj_kernel, bounds=bounds, per_seq=per_seq, q_scale=q_scale),
        grid=(T // tm, n_main + nv // tn),
        in_specs=[
            pl.BlockSpec((tm, D), lambda i, j: (i, 0)),
            pl.BlockSpec((1, D), lambda i, j: (0, 0)),
            pl.BlockSpec((D, tn), lambda i, j: (0, jnp.where(j < n_a, j, 0))),
            pl.BlockSpec((D, tn), lambda i, j: (0, jnp.where(j < n_a, n_b - 1,
                                                             jnp.minimum(j - n_a, n_b - 1)))),
            pl.BlockSpec((D, GATE_COLS), lambda i, j: (0, 4 * dn_dim // GATE_COLS),
                         pipeline_mode=pl.Buffered(1)),
            pl.BlockSpec((tn, D), lambda i, j: (jnp.maximum(j - n_main, 0), 0),
                         pipeline_mode=pl.Buffered(1 if nv == tn else 2)),
            pl.BlockSpec((CONV_WIDTH, tn), lambda i, j: (0, jnp.minimum(j, n_conv - 1))),
        ],
        out_specs=[
            pl.BlockSpec((tm, tn), lambda i, j: (i, jnp.minimum(j, n_main - 1))),
            pl.BlockSpec((tm, GATE_COLS), lambda i, j: (i, 0)),
            pl.BlockSpec((None, tm // tv, tn, tv),
                         lambda i, j: (i // per_seq, i % per_seq, jnp.maximum(j - n_main, 0), 0)),
        ],
        out_shape=[
            jax.ShapeDtypeStruct((T, N), BF16),
            jax.ShapeDtypeStruct((T, GATE_COLS), F32),
            jax.ShapeDtypeStruct((batch, S // tv, nv, tv), BF16),
        ],
        scratch_shapes=[
            pltpu.VMEM((tm, D), BF16),
            pltpu.VMEM((n_conv, 8, tn), F32),
            pltpu.VMEM((8 + tm, tn), F32),
        ],
        compiler_params=pltpu.CompilerParams(
            dimension_semantics=("arbitrary", "arbitrary"),
            vmem_limit_bytes=V7X_VMEM_LIMIT),
        name="in_proj",
    )(x2, gain, w_all, w_b, w_all, w_vt, conv_w)


def _gdn_kernel(q_ref, k_ref, v_ref, z_ref, gate_ref, alog_ref, dtb_ref, dnn_ref,
                y_ref, state_ref, *, blk, n_heads):
    dk = DN_HEAD_DIM
    n_chunks = blk // CHUNK

    @pl.when(pl.program_id(1) == 0)
    def _():
        state_ref[...] = jnp.zeros_like(state_ref)

    gate = gate_ref[...]
    beta_all = _sigmoid(gate)
    xs = gate + dtb_ref[...]
    softplus = jnp.maximum(xs, 0.0) + jnp.log(1.0 + jnp.exp(-jnp.abs(xs)))
    g_all = -jnp.exp(alog_ref[...]) * softplus
    pos = lax.broadcasted_iota(jnp.int32, g_all.shape, 0) & (CHUNK - 1)
    gc_all = g_all
    step = 1
    while step < CHUNK:
        gc_all = gc_all + jnp.where(pos >= step, pltpu.roll(gc_all, step, 0), 0.0)
        step *= 2

    ri = lax.broadcasted_iota(jnp.int32, (CHUNK, CHUNK), 0)
    ci = lax.broadcasted_iota(jnp.int32, (CHUNK, CHUNK), 1)
    tril = ri >= ci
    strict = ri > ci
    eye = ri == ci
    gain = dnn_ref[...]

    def first_stage(c):
        rows = slice(c * CHUNK, (c + 1) * CHUNK)
        items = []
        for h in range(n_heads):
            cols = slice(h * dk, (h + 1) * dk)
            q_bf = q_ref[rows, cols]
            k_bf = k_ref[rows, cols]
            qc = q_bf.astype(F32)
            kc = k_bf.astype(F32)
            vc = v_ref[rows, cols].astype(F32)
            bc = jnp.broadcast_to(beta_all[rows, h:h + 1], (CHUNK, dk))
            gcc = jnp.broadcast_to(gc_all[rows, n_heads + h:n_heads + h + 1], (CHUNK, dk))
            g_last = gcc[CHUNK - 1:CHUNK, :]
            eg = jnp.exp(gcc)
            g_sq = gcc[:, 0:CHUNK]
            g_row = jnp.sum(jnp.where(eye, g_sq, 0.0), axis=0, keepdims=True)
            decay = jnp.exp(jnp.where(tril, g_sq - g_row, -jnp.inf))
            kb = kc * bc
            lhs = jnp.concatenate([kb.astype(BF16), q_bf], axis=0)
            aq = _dot_nt(lhs, k_bf)
            n_mat = jnp.where(strict, -(aq[:CHUNK] * decay), 0.0)
            items.append(dict(
                h=h, qe=qc * eg, g_last=g_last,
                rhs=jnp.concatenate([kb * eg, vc * bc], axis=1).astype(BF16),
                qk=jnp.where(tril, aq[CHUNK:] * decay, 0.0).astype(BF16),
                kd_t=(kc * jnp.exp(g_last - gcc)).T.astype(BF16),
                power=n_mat, inv=jnp.where(eye, 1.0, 0.0) + n_mat))
        return items

    def matrix_stages(items):
        span = 2
        while span < CHUNK:
            for s in items:
                pb = s["power"].astype(BF16)
                s["power"] = _dot(pb, pb)
            for s in items:
                s["inv"] = s["inv"] + _dot(s["inv"].astype(BF16), s["power"].astype(BF16))
            span *= 2
        for s in items:
            s["wu"] = _dot(s["inv"].astype(BF16), s["rhs"]).astype(BF16)
        for s in items:
            s["gr"] = _dot(s["kd_t"], s["wu"])
            qw = _dot(s["qk"], s["wu"])
            s["q_eff"] = (s["qe"] - qw[:, :dk]).astype(BF16)
            s["p_loc"] = qw[:, dk:]

    def state_stage(c, items):
        rows = slice(c * CHUNK, (c + 1) * CHUNK)
        for s in items:
            h = s["h"]
            state = state_ref[h]
            s_bf = state.astype(BF16)
            o = _dot(s["q_eff"], s_bf) + s["p_loc"]
            state_ref[h] = (state * jnp.exp(s["g_last"])
                            - _dot(s["gr"][:, :dk].astype(BF16), s_bf) + s["gr"][:, dk:])
            zs = z_ref[rows, h * dk:(h + 1) * dk].astype(F32)
            o = o * lax.rsqrt(jnp.mean(o * o, axis=-1, keepdims=True) + 1e-6) * gain
            y_ref[rows, h * dk:(h + 1) * dk] = (o * zs).astype(y_ref.dtype)

    groups = [list(range(g, min(g + GDN_GROUP, n_chunks))) for g in range(0, n_chunks, GDN_GROUP)]
    cur = [first_stage(c) for c in groups[0]]
    for gi, chunk_ids in enumerate(groups):
        nxt = [first_stage(c) for c in groups[gi + 1]] if gi + 1 < len(groups) else None
        matrix_stages([s for items in cur for s in items])
        for c, items in zip(chunk_ids, cur):
            state_stage(c, items)
        cur = nxt


def _gdn(proj3, gate3, alog_row, dtb_row, dn_norm, *, n_heads, blk):
    B, S, _ = proj3.shape
    dk = DN_HEAD_DIM
    dn = n_heads * dk

    def group(idx):
        return pl.BlockSpec((None, blk, dn), lambda b, t: (b, t, idx))

    def whole(arr):
        return pl.BlockSpec(arr.shape, lambda b, t: (0,) * arr.ndim)

    return pl.pallas_call(
        functools.partial(_gdn_kernel, blk=blk, n_heads=n_heads),
        grid=(B, S // blk),
        in_specs=[
            group(0), group(1), group(2), group(3),
            pl.BlockSpec((None, blk, GATE_COLS), lambda b, t: (b, t, 0)),
            whole(alog_row), whole(dtb_row), whole(dn_norm),
        ],
        out_specs=pl.BlockSpec((None, blk, dn), lambda b, t: (b, t, 0)),
        out_shape=jax.ShapeDtypeStruct((B, S, dn), BF16),
        scratch_shapes=[pltpu.VMEM((n_heads, dk, dk), F32)],
        compiler_params=pltpu.CompilerParams(
            dimension_semantics=("arbitrary", "arbitrary"),
            vmem_limit_bytes=V7X_VMEM_LIMIT),
        name="gdn",
    )(proj3, proj3, proj3, proj3, gate3, alog_row, dtb_row, dn_norm)


def _t5_bucket_starts():
    max_exact = NUM_BUCKETS // 2
    n = np.arange(0, MAX_DISTANCE + 1)
    nf = np.maximum(n, 1).astype(np.float32)
    large = max_exact + (np.log(nf / max_exact) / math.log(MAX_DISTANCE / max_exact)
                         * (NUM_BUCKETS - max_exact)).astype(np.int32)
    bucket = np.where(n < max_exact, n, np.minimum(large, NUM_BUCKETS - 1))
    assert bucket[MAX_DISTANCE] == NUM_BUCKETS - 1 and np.all(np.diff(bucket) >= 0)
    starts = [(0, int(bucket[0]))]
    for d in range(1, MAX_DISTANCE + 1):
        if bucket[d] != bucket[d - 1]:
            starts.append((d, int(bucket[d])))
    return starts


def _diff_attn_kernel(rb_ref, q_ref, k_ref, vt_ref, lq1_ref, lk1_ref, lq2_ref, lk2_ref, dfn_ref,
                      y_ref, bias_ref, s_ref, p_ref, m_ref, l_ref, acc_ref,
                      *, tq, n_heads, lam_init):
    h = pl.program_id(1)
    qi = pl.program_id(2)
    d = DF_HEAD_DIM
    sub = ATTN_SUB
    n_sub = tq // sub
    log2e = math.log2(math.e)

    @pl.when(qi == 0)
    def _():
        keys = lax.broadcasted_iota(jnp.int32, (sub, sub), 0)
        qrys = lax.broadcasted_iota(jnp.int32, (sub, sub), 1)
        starts = _t5_bucket_starts()
        for idx in range(2):
            dist = qrys - keys + idx * sub
            tile = jnp.full((sub, sub), rb_ref[starts[0][1] * n_heads + h], F32)
            for first, bucket in starts[1:]:
                tile = jnp.where(dist >= first, rb_ref[bucket * n_heads + h], tile)
            tile = tile * log2e
            if idx == 0:
                tile = jnp.where(dist >= 0, tile, -jnp.inf)
            bias_ref[idx] = tile
        bias_ref[2] = jnp.full((sub, sub), rb_ref[(NUM_BUCKETS - 1) * n_heads + h] * log2e, F32)
        bias_ref[3] = jnp.full((sub, sub), -jnp.inf, F32)

    m_ref[...] = jnp.full(m_ref.shape, -jnp.inf, F32)
    l_ref[...] = jnp.zeros_like(l_ref)
    acc_ref[...] = jnp.zeros_like(acc_ref)

    chains = [(qb, m) for qb in range(n_sub) for m in range(2)]

    far_shift = rb_ref[(NUM_BUCKETS - 1) * n_heads + h] * log2e

    def produce(j, c):
        qb, m = chains[c]
        k0 = pl.multiple_of(j * tq, tq)
        s_ref[c] = _dot_nt(k_ref[pl.ds(k0, tq), m * d:(m + 1) * d],
                           q_ref[qb * sub:(qb + 1) * sub, m * d:(m + 1) * d])

    def softmax(j, c, far):
        qb, m = chains[c]
        if far:
            s = s_ref[c]
            shift = far_shift
        else:
            parts = []
            for kb in range(n_sub):
                off = (qi - j) * n_sub + (qb - kb)
                idx = jnp.where(off < 0, 3, jnp.minimum(off, 2))
                parts.append(s_ref[c, kb * sub:(kb + 1) * sub, :] + bias_ref[idx])
            s = jnp.concatenate(parts, axis=0)
            shift = 0.0
        cols = slice(qb * sub, (qb + 1) * sub)
        m_prev = m_ref[m, :, cols]
        m_new = jnp.maximum(m_prev, jnp.max(s, axis=0, keepdims=True) + shift)
        alpha = jnp.exp2(m_prev - m_new)
        pr = jnp.exp2(s - (m_new - shift))
        l_ref[m, :, cols] = alpha * l_ref[m, :, cols] + jnp.sum(pr, axis=0, keepdims=True)
        acc_ref[m, :, cols] = alpha * acc_ref[m, :, cols]
        m_ref[m, :, cols] = m_new
        return pr.astype(BF16)

    def add_values(j, c, p):
        qb, m = chains[c]
        cols = slice(qb * sub, (qb + 1) * sub)
        acc_ref[m, :, cols] += _dot(vt_ref[j], p)

    n_chains = len(chains)
    for c in range(n_chains):
        produce(0, c)
    p_ref[...] = softmax(0, 0, False)

    def trip(j, far):
        add_values(j, 0, p_ref[...])
        produce(j + 1, 0)
        for c in range(1, n_chains):
            add_values(j, c, softmax(j, c, far))
            produce(j + 1, c)
        p_ref[...] = softmax(j + 1, 0, far)

    n_far = jnp.maximum(qi - 2, 0)
    lax.fori_loop(0, n_far, lambda j, carry: (trip(j, True), carry)[1], 0)
    lax.fori_loop(n_far, qi, lambda j, carry: (trip(j, False), carry)[1], 0)
    add_values(qi, 0, p_ref[...])
    for c in range(1, n_chains):
        add_values(qi, c, softmax(qi, c, False))

    lam = (jnp.exp(jnp.sum(lq1_ref[...] * lk1_ref[...], axis=-1, keepdims=True))
           - jnp.exp(jnp.sum(lq2_ref[...] * lk2_ref[...], axis=-1, keepdims=True))
           + lam_init)
    o = acc_ref[0] * (1.0 / l_ref[0]) - acc_ref[1] * (lam / l_ref[1])
    o = o * lax.rsqrt(jnp.mean(o * o, axis=0, keepdims=True) + 1e-5) * dfn_ref[...]
    y_ref[...] = (o * (1.0 - lam_init)).T.astype(y_ref.dtype)


def _diff_attn(proj3, v_t, rel_bias, lq1, lk1, lq2, lk2, df_norm_col, *, n_heads, col0, tq,
               lam_init):
    B, S, _ = proj3.shape
    d2 = 2 * DF_HEAD_DIM
    assert tq % ATTN_SUB == 0 and ATTN_SUB >= MAX_DISTANCE and S % tq == 0
    nq = S // tq
    n_chains = 2 * (tq // ATTN_SUB)
    cb = col0 // d2
    vec = lambda n: pl.BlockSpec((1, n), lambda b, h, i: (0, 0))
    return pl.pallas_call(
        functools.partial(_diff_attn_kernel, tq=tq, n_heads=n_heads, lam_init=lam_init),
        grid=(B, n_heads, nq),
        in_specs=[
            pl.BlockSpec(memory_space=pltpu.SMEM),
            pl.BlockSpec((None, tq, d2), lambda b, h, i: (b, i, cb + h)),
            pl.BlockSpec((None, S, d2), lambda b, h, i: (b, 0, cb + n_heads + h)),
            pl.BlockSpec((None, nq, d2, tq), lambda b, h, i: (b, 0, h, 0)),
            vec(DF_HEAD_DIM), vec(DF_HEAD_DIM), vec(DF_HEAD_DIM), vec(DF_HEAD_DIM),
            pl.BlockSpec((d2, 1), lambda b, h, i: (0, 0)),
        ],
        out_specs=pl.BlockSpec((None, tq, d2), lambda b, h, i: (b, i, h)),
        out_shape=jax.ShapeDtypeStruct((B, S, n_heads * d2), BF16),
        scratch_shapes=[
            pltpu.VMEM((4, ATTN_SUB, ATTN_SUB), F32),
            pltpu.VMEM((n_chains, tq, ATTN_SUB), F32),
            pltpu.VMEM((tq, ATTN_SUB), BF16),
            pltpu.VMEM((2, 1, tq), F32),
            pltpu.VMEM((2, 1, tq), F32),
            pltpu.VMEM((2, d2, tq), F32),
        ],
        compiler_params=pltpu.CompilerParams(
            dimension_semantics=("arbitrary", "arbitrary", "arbitrary"),
            vmem_limit_bytes=V7X_VMEM_LIMIT),
        name="diff_attn",
    )(rel_bias, proj3, proj3, v_t, lq1, lk1, lq2, lk2, df_norm_col)


def _out_proj_kernel(x_ref, ya_ref, yb_ref, wa_ref, wb_ref, h_ref):
    h_ref[...] = x_ref[...] + _dot(ya_ref[...], wa_ref[...]) + _dot(yb_ref[...], wb_ref[...])


def _out_proj(x2, y_dn, y_df, w_o, *, tm, tn):
    T, D = x2.shape
    ka = y_dn.shape[1]
    kb = y_df.shape[1]
    assert ka == kb
    return pl.pallas_call(
        _out_proj_kernel,
        grid=(T // tm, D // tn),
        in_specs=[
            pl.BlockSpec((tm, tn), lambda i, j: (i, j)),
            pl.BlockSpec((tm, ka), lambda i, j: (i, 0)),
            pl.BlockSpec((tm, kb), lambda i, j: (i, 0)),
            pl.BlockSpec((ka, tn), lambda i, j: (0, j)),
            pl.BlockSpec((kb, tn), lambda i, j: (1, j)),
        ],
        out_specs=pl.BlockSpec((tm, tn), lambda i, j: (i, j)),
        out_shape=jax.ShapeDtypeStruct((T, D), F32),
        compiler_params=pltpu.CompilerParams(
            dimension_semantics=("arbitrary", "arbitrary"),
            vmem_limit_bytes=V7X_VMEM_LIMIT),
        name="out_proj",
    )(x2, y_dn, y_df, w_o, w_o)


def _mlp_kernel(h_ref, g_ref, wu_ref, wd_ref, gf_ref, o_ref, u_ref, acc_ref):
    f = pl.program_id(1)

    @pl.when(f == 0)
    def _():
        x = h_ref[...]
        ms = jnp.mean(x * x, axis=-1, keepdims=True)
        u_ref[...] = (x * lax.rsqrt(ms + 1e-6) * g_ref[...]).astype(BF16)
        acc_ref[...] = jnp.zeros_like(acc_ref)

    hid = jnp.maximum(_dot(u_ref[...], wu_ref[...]), 0.0)
    acc_ref[...] += _dot((hid * hid).astype(BF16), wd_ref[...])

    @pl.when(f == pl.num_programs(1) - 1)
    def _():
        y = h_ref[...] + acc_ref[...]
        ms = jnp.mean(y * y, axis=-1, keepdims=True)
        o_ref[...] = y * lax.rsqrt(ms + 1e-6) * gf_ref[...]


def _mlp(h1, gain, w_up, w_down, final_gain, *, tm, tf):
    T, D = h1.shape
    Fdim = w_up.shape[1]
    return pl.pallas_call(
        _mlp_kernel,
        grid=(T // tm, Fdim // tf),
        in_specs=[
            pl.BlockSpec((tm, D), lambda i, f: (i, 0)),
            pl.BlockSpec((1, D), lambda i, f: (0, 0)),
            pl.BlockSpec((D, tf), lambda i, f: (0, f)),
            pl.BlockSpec((tf, D), lambda i, f: (f, 0)),
            pl.BlockSpec((1, D), lambda i, f: (0, 0)),
        ],
        out_specs=pl.BlockSpec((tm, D), lambda i, f: (i, 0)),
        out_shape=jax.ShapeDtypeStruct((T, D), F32),
        scratch_shapes=[pltpu.VMEM((tm, D), BF16), pltpu.VMEM((tm, D), F32)],
        compiler_params=pltpu.CompilerParams(
            dimension_semantics=("arbitrary", "arbitrary"),
            vmem_limit_bytes=V7X_VMEM_LIMIT),
        name="mlp",
    )(h1, gain, w_up, w_down, final_gain)


def _tile(n, pref):
    if n <= pref:
        return n
    t = pref - pref % 128
    while t > 128 and n % t:
        t -= 128
    assert n % t == 0
    return t


def kernel(x, attn_norm, w_in, conv_w, a_log, dt_bias, dn_norm, lambda_q1, lambda_k1,
           lambda_q2, lambda_k2, df_norm, rel_bias, w_o, mlp_norm, w_up, w_down, final_norm):
    B, S, D = x.shape
    depth = attn_norm.shape[0]
    n_dn = a_log.shape[1]
    n_df = rel_bias.shape[1]
    dn_dim = n_dn * DN_HEAD_DIM
    df_dim = n_df * 2 * DF_HEAD_DIM
    T = B * S
    gate0 = 4 * dn_dim
    assert w_in.shape[2] == gate0 + 2 * n_dn + 3 * df_dim and 2 * n_dn <= GATE_COLS

    assert depth == 1
    l = 0
    h = x.reshape(T, D)

    wl = w_in[l]
    dfq0 = gate0 + 2 * n_dn
    dfv0 = dfq0 + 2 * df_dim
    w_all = wl.astype(BF16)
    w_b = w_all[:, dfq0:dfv0]
    w_vt = w_all[:, dfv0:].T
    w_o_bf = w_o[l].astype(BF16)
    w_up_bf = w_up[l].astype(BF16)
    w_down_bf = w_down[l].astype(BF16)

    proj, gates, v_t = _in_proj(h, attn_norm[l][None, :], w_all, w_b, w_vt, conv_w[l],
                                batch=B, dn_dim=dn_dim, df_dim=df_dim, tm=_tile(S, IN_PROJ_ROWS),
                                tn=_tile(math.gcd(dn_dim, df_dim), IN_PROJ_COLS),
                                tv=_tile(S, ATTN_BLOCK),
                                q_scale=DF_HEAD_DIM ** -0.5 * math.log2(math.e))
    proj3 = proj.reshape(B, S, -1)
    gate3 = gates.reshape(B, S, GATE_COLS)

    gate_pad = ((0, 0), (n_dn, GATE_COLS - 2 * n_dn))
    alog_row = jnp.pad(a_log[l][None, :], gate_pad)
    dtb_row = jnp.pad(dt_bias[l][None, :], gate_pad)
    y_dn = _gdn(proj3, gate3, alog_row, dtb_row, dn_norm[l][None, :],
                n_heads=n_dn, blk=_tile(S, GDN_BLOCK))
    lam_init = 0.8 - 0.6 * math.exp(-0.3 * l)
    y_df = _diff_attn(proj3, v_t, rel_bias.reshape(-1),
                      lambda_q1[l][None, :], lambda_k1[l][None, :],
                      lambda_q2[l][None, :], lambda_k2[l][None, :], df_norm[l][:, None],
                      n_heads=n_df, col0=gate0, tq=_tile(S, ATTN_BLOCK), lam_init=lam_init)

    h1 = _out_proj(h, y_dn.reshape(T, dn_dim), y_df.reshape(T, df_dim), w_o_bf,
                   tm=_tile(T, 512), tn=_tile(D, 2048))
    out = _mlp(h1, mlp_norm[l][None, :], w_up_bf, w_down_bf, final_norm[None, :],
               tm=_tile(T, 512), tf=_tile(w_up_bf.shape[1], 1024))
    return out.reshape(B, S, D)
```

```python
import functools
import math

import numpy as np
import jax
import jax.numpy as jnp
from jax import lax
from jax.experimental import pallas as pl
from jax.experimental.pallas import tpu as pltpu

F32 = jnp.float32
BF16 = jnp.bfloat16

DN_HEAD_DIM = 128
DF_HEAD_DIM = 128
CONV_WIDTH = 4
CHUNK = 64
NUM_BUCKETS = 32
MAX_DISTANCE = 128
GATE_COLS = 128
IN_PROJ_ROWS = 1024
IN_PROJ_COLS = 1024
IN_PROJ_SUB = 256
GDN_BLOCK = 512
GDN_GROUP = 4
ATTN_SUB = 256
ATTN_BLOCK = 512

V7X_VMEM_LIMIT = 58 * 1024 * 1024


def _dot(a, b):
    return jnp.dot(a, b, preferred_element_type=F32)


def _dot_nt(a, b):
    return lax.dot_general(a, b, (((1,), (1,)), ((), ())), preferred_element_type=F32)


def _sigmoid(x):
    return 1.0 / (1.0 + jnp.exp(-x))


def _in_proj_kernel(x_ref, g_ref, wa_ref, wb_ref, wg_ref, wvt_ref, cw_ref, proj_ref, gate_ref,
                    vt_ref, u_ref, hist_ref, cbuf_ref, *, bounds, per_seq, q_scale):
    i = pl.program_id(0)
    j = pl.program_id(1)
    tm, tn = proj_ref.shape
    b0, b1, b2, b3, b4, b5 = bounds

    @pl.when(j == 0)
    def _():
        x = x_ref[...]
        ms = jnp.mean(x * x, axis=-1, keepdims=True)
        u = (x * lax.rsqrt(ms + 1e-6) * g_ref[...]).astype(BF16)
        u_ref[...] = u
        gate_ref[...] = _dot(u, wg_ref[...])

    n_col_chunks = tn // IN_PROJ_SUB
    first_of_seq = (i % per_seq) == 0

    def chunk_cols(c):
        return slice(c * IN_PROJ_SUB, (c + 1) * IN_PROJ_SUB)

    def raw_cols(w_ref, c):
        cbuf_ref[8:8 + tm, chunk_cols(c)] = _dot(u_ref[...], w_ref[:, chunk_cols(c)])

    def raw(c):
        return cbuf_ref[8:8 + tm, chunk_cols(c)]

    def conv_silu(c):
        cw = cw_ref[:, chunk_cols(c)]
        y = None
        for s in range(CONV_WIDTH):
            tap = CONV_WIDTH - 1 - s
            term = cbuf_ref[8 - s:8 - s + tm, chunk_cols(c)] * cw[tap:tap + 1, :]
            y = term if y is None else y + term
        return y * _sigmoid(y)

    def l2norm_heads(y, scale):
        outs = []
        for h in range(IN_PROJ_SUB // DN_HEAD_DIM):
            yh = y[:, h * DN_HEAD_DIM:(h + 1) * DN_HEAD_DIM]
            outs.append(yh * (lax.rsqrt(jnp.sum(yh * yh, axis=-1, keepdims=True) + 1e-6) * scale))
        return jnp.concatenate(outs, axis=1)

    def silu_cols(c):
        z = raw(c)
        return z * _sigmoid(z)

    def project(w_ref, epilogue, conv=False):
        if conv:
            cbuf_ref[0:8, :] = jnp.where(first_of_seq, 0.0, hist_ref[j])
        raw_cols(w_ref, 0)
        for c in range(n_col_chunks):
            if c + 1 < n_col_chunks:
                raw_cols(w_ref, c + 1)
            proj_ref[:, chunk_cols(c)] = epilogue(c).astype(BF16)
        if conv:
            hist_ref[j] = cbuf_ref[tm:tm + 8, :]

    @pl.when(j < b0)
    def _():
        project(wa_ref, lambda c: l2norm_heads(conv_silu(c), DN_HEAD_DIM ** -0.5), conv=True)

    @pl.when((j >= b0) & (j < b1))
    def _():
        project(wa_ref, lambda c: l2norm_heads(conv_silu(c), 1.0), conv=True)

    @pl.when((j >= b1) & (j < b2))
    def _():
        project(wa_ref, conv_silu, conv=True)

    @pl.when((j >= b2) & (j < b3))
    def _():
        project(wa_ref, silu_cols)

    @pl.when((j >= b3) & (j < b4))
    def _():
        project(wb_ref, lambda c: raw(c) * q_scale)

    @pl.when((j >= b4) & (j < b5))
    def _():
        project(wb_ref, raw)

    @pl.when(j >= b5)
    def _():
        vt = _dot_nt(wvt_ref[...], u_ref[...]).astype(BF16)
        tv = vt_ref.shape[-1]
        for c in range(vt_ref.shape[0]):
            vt_ref[c] = vt[:, c * tv:(c + 1) * tv]


def _in_proj(x2, gain, w_all, w_b, w_vt, conv_w, *, batch, dn_dim, df_dim, tm, tn, tv, q_scale):
    T, D = x2.shape
    N = 4 * dn_dim + w_b.shape[1]
    nv = w_vt.shape[0]
    S = T // batch
    assert w_b.shape[1] == 2 * df_dim and nv == df_dim
    assert (4 * dn_dim) % GATE_COLS == 0 and w_all.shape[1] >= 4 * dn_dim + GATE_COLS
    assert S % tm == 0 and dn_dim % tn == 0 and df_dim % tn == 0 and tm % tv == 0
    assert tn % DN_HEAD_DIM == 0 and conv_w.shape == (CONV_WIDTH, 3 * dn_dim)
    assert tn % IN_PROJ_SUB == 0 and IN_PROJ_SUB % DN_HEAD_DIM == 0
    n_dn, n_df = dn_dim // tn, df_dim // tn
    bounds = (n_dn, 2 * n_dn, 3 * n_dn, 4 * n_dn, 4 * n_dn + n_df, 4 * n_dn + 2 * n_df)
    n_main = bounds[-1]
    n_conv = bounds[2]
    n_a = bounds[3]
    n_b = n_main - n_a
    per_seq = S // tm
    return pl.pallas_call(
        functools.partial(_in_proj_kernel, bounds=bounds, per_seq=per_seq, q_scale=q_scale),
        grid=(T // tm, n_main + nv // tn),
        in_specs=[
            pl.BlockSpec((tm, D), lambda i, j: (i, 0)),
            pl.BlockSpec((1, D), lambda i, j: (0, 0)),
            pl.BlockSpec((D, tn), lambda i, j: (0, jnp.where(j < n_a, j, 0))),
            pl.BlockSpec((D, tn), lambda i, j: (0, jnp.where(j < n_a, n_b - 1,
                                                             jnp.minimum(j - n_a, n_b - 1)))),
            pl.BlockSpec((D, GATE_COLS), lambda i, j: (0, 4 * dn_dim // GATE_COLS),
                         pipeline_mode=pl.Buffered(1)),
            pl.BlockSpec((tn, D), lambda i, j: (jnp.maximum(j - n_main, 0), 0),
                         pipeline_mode=pl.Buffered(1 if nv == tn else 2)),
            pl.BlockSpec((CONV_WIDTH, tn), lambda i, j: (0, jnp.minimum(j, n_conv - 1))),
        ],
        out_specs=[
            pl.BlockSpec((tm, tn), lambda i, j: (i, jnp.minimum(j, n_main - 1))),
            pl.BlockSpec((tm, GATE_COLS), lambda i, j: (i, 0)),
            pl.BlockSpec((None, tm // tv, tn, tv),
                         lambda i, j: (i // per_seq, i % per_seq, jnp.maximum(j - n_main, 0), 0)),
        ],
        out_shape=[
            jax.ShapeDtypeStruct((T, N), BF16),
            jax.ShapeDtypeStruct((T, GATE_COLS), F32),
            jax.ShapeDtypeStruct((batch, S // tv, nv, tv), BF16),
        ],
        scratch_shapes=[
            pltpu.VMEM((tm, D), BF16),
            pltpu.VMEM((n_conv, 8, tn), F32),
            pltpu.VMEM((8 + tm, tn), F32),
        ],
        compiler_params=pltpu.CompilerParams(
            dimension_semantics=("arbitrary", "arbitrary"),
            vmem_limit_bytes=V7X_VMEM_LIMIT),
        name="in_proj",
    )(x2, gain, w_all, w_b, w_all, w_vt, conv_w)


def _gdn_kernel(q_ref, k_ref, v_ref, z_ref, gate_ref, alog_ref, dtb_ref, dnn_ref,
                y_ref, state_ref, *, blk, n_heads):
    dk = DN_HEAD_DIM
    n_chunks = blk // CHUNK

    @pl.when(pl.program_id(1) == 0)
    def _():
        state_ref[...] = jnp.zeros_like(state_ref)

    gate = gate_ref[...]
    beta_all = _sigmoid(gate)
    xs = gate + dtb_ref[...]
    softplus = jnp.maximum(xs, 0.0) + jnp.log(1.0 + jnp.exp(-jnp.abs(xs)))
    g_all = -jnp.exp(alog_ref[...]) * softplus
    pos = lax.broadcasted_iota(jnp.int32, g_all.shape, 0) & (CHUNK - 1)
    gc_all = g_all
    step = 1
    while step < CHUNK:
        gc_all = gc_all + jnp.where(pos >= step, pltpu.roll(gc_all, step, 0), 0.0)
        step *= 2

    ri = lax.broadcasted_iota(jnp.int32, (CHUNK, CHUNK), 0)
    ci = lax.broadcasted_iota(jnp.int32, (CHUNK, CHUNK), 1)
    tril = ri >= ci
    strict = ri > ci
    eye = ri == ci
    gain = dnn_ref[...]

    def first_stage(c):
        rows = slice(c * CHUNK, (c + 1) * CHUNK)
        items = []
        for h in range(n_heads):
            cols = slice(h * dk, (h + 1) * dk)
            q_bf = q_ref[rows, cols]
            k_bf = k_ref[rows, cols]
            qc = q_bf.astype(F32)
            kc = k_bf.astype(F32)
            vc = v_ref[rows, cols].astype(F32)
            bc = jnp.broadcast_to(beta_all[rows, h:h + 1], (CHUNK, dk))
            gcc = jnp.broadcast_to(gc_all[rows, n_heads + h:n_heads + h + 1], (CHUNK, dk))
            g_last = gcc[CHUNK - 1:CHUNK, :]
            eg = jnp.exp(gcc)
            g_sq = gcc[:, 0:CHUNK]
            g_row = jnp.sum(jnp.where(eye, g_sq, 0.0), axis=0, keepdims=True)
            decay = jnp.exp(jnp.where(tril, g_sq - g_row, -jnp.inf))
            kb = kc * bc
            lhs = jnp.concatenate([kb.astype(BF16), q_bf], axis=0)
            aq = _dot_nt(lhs, k_bf)
            n_mat = jnp.where(strict, -(aq[:CHUNK] * decay), 0.0)
            items.append(dict(
                h=h, qe=qc * eg, g_last=g_last,
                rhs=jnp.concatenate([kb * eg, vc * bc], axis=1).astype(BF16),
                qk=jnp.where(tril, aq[CHUNK:] * decay, 0.0).astype(BF16),
                kd_t=(kc * jnp.exp(g_last - gcc)).T.astype(BF16),
                power=n_mat, inv=jnp.where(eye, 1.0, 0.0) + n_mat))
        return items

    def matrix_stages(items):
        span = 2
        while span < CHUNK:
            for s in items:
                pb = s["power"].astype(BF16)
                s["power"] = _dot(pb, pb)
            for s in items:
                s["inv"] = s["inv"] + _dot(s["inv"].astype(BF16), s["power"].astype(BF16))
            span *= 2
        for s in items:
            s["wu"] = _dot(s["inv"].astype(BF16), s["rhs"]).astype(BF16)
        for s in items:
            s["gr"] = _dot(s["kd_t"], s["wu"])
            qw = _dot(s["qk"], s["wu"])
            s["q_eff"] = (s["qe"] - qw[:, :dk]).astype(BF16)
            s["p_loc"] = qw[:, dk:]

    def state_stage(c, items):
        rows = slice(c * CHUNK, (c + 1) * CHUNK)
        for s in items:
            h = s["h"]
            state = state_ref[h]
            s_bf = state.astype(BF16)
            o = _dot(s["q_eff"], s_bf) + s["p_loc"]
            state_ref[h] = (state * jnp.exp(s["g_last"])
                            - _dot(s["gr"][:, :dk].astype(BF16), s_bf) + s["gr"][:, dk:])
            zs = z_ref[rows, h * dk:(h + 1) * dk].astype(F32)
            o = o * lax.rsqrt(jnp.mean(o * o, axis=-1, keepdims=True) + 1e-6) * gain
            y_ref[rows, h * dk:(h + 1) * dk] = (o * zs).astype(y_ref.dtype)

    groups = [list(range(g, min(g + GDN_GROUP, n_chunks))) for g in range(0, n_chunks, GDN_GROUP)]
    cur = [first_stage(c) for c in groups[0]]
    for gi, chunk_ids in enumerate(groups):
        nxt = [first_stage(c) for c in groups[gi + 1]] if gi + 1 < len(groups) else None
        matrix_stages([s for items in cur for s in items])
        for c, items in zip(chunk_ids, cur):
            state_stage(c, items)
        cur = nxt


def _gdn(proj3, gate3, alog_row, dtb_row, dn_norm, *, n_heads, blk):
    B, S, _ = proj3.shape
    dk = DN_HEAD_DIM
    dn = n_heads * dk

    def group(idx):
        return pl.BlockSpec((None, blk, dn), lambda b, t: (b, t, idx))

    def whole(arr):
        return pl.BlockSpec(arr.shape, lambda b, t: (0,) * arr.ndim)

    return pl.pallas_call(
        functools.partial(_gdn_kernel, blk=blk, n_heads=n_heads),
        grid=(B, S // blk),
        in_specs=[
            group(0), group(1), group(2), group(3),
            pl.BlockSpec((None, blk, GATE_COLS), lambda b, t: (b, t, 0)),
            whole(alog_row), whole(dtb_row), whole(dn_norm),
        ],
        out_specs=pl.BlockSpec((None, blk, dn), lambda b, t: (b, t, 0)),
        out_shape=jax.ShapeDtypeStruct((B, S, dn), BF16),
        scratch_shapes=[pltpu.VMEM((n_heads, dk, dk), F32)],
        compiler_params=pltpu.CompilerParams(
            dimension_semantics=("arbitrary", "arbitrary"),
            vmem_limit_bytes=V7X_VMEM_LIMIT),
        name="gdn",
    )(proj3, proj3, proj3, proj3, gate3, alog_row, dtb_row, dn_norm)


def _t5_bucket_starts():
    max_exact = NUM_BUCKETS // 2
    n = np.arange(0, MAX_DISTANCE + 1)
    nf = np.maximum(n, 1).astype(np.float32)
    large = max_exact + (np.log(nf / max_exact) / math.log(MAX_DISTANCE / max_exact)
                         * (NUM_BUCKETS - max_exact)).astype(np.int32)
    bucket = np.where(n < max_exact, n, np.minimum(large, NUM_BUCKETS - 1))
    assert bucket[MAX_DISTANCE] == NUM_BUCKETS - 1 and np.all(np.diff(bucket) >= 0)
    starts = [(0, int(bucket[0]))]
    for d in range(1, MAX_DISTANCE + 1):
        if bucket[d] != bucket[d - 1]:
            starts.append((d, int(bucket[d])))
    return starts


def _diff_attn_kernel(rb_ref, q_ref, k_ref, vt_ref, lq1_ref, lk1_ref, lq2_ref, lk2_ref, dfn_ref,
                      y_ref, bias_ref, s_ref, p_ref, m_ref, l_ref, acc_ref,
                      *, tq, n_heads, lam_init):
    h = pl.program_id(1)
    qi = pl.program_id(2)
    d = DF_HEAD_DIM
    sub = ATTN_SUB
    n_sub = tq // sub
    log2e = math.log2(math.e)

    @pl.when(qi == 0)
    def _():
        keys = lax.broadcasted_iota(jnp.int32, (sub, sub), 0)
        qrys = lax.broadcasted_iota(jnp.int32, (sub, sub), 1)
        starts = _t5_bucket_starts()
        for idx in range(2):
            dist = qrys - keys + idx * sub
            tile = jnp.full((sub, sub), rb_ref[starts[0][1] * n_heads + h], F32)
            for first, bucket in starts[1:]:
                tile = jnp.where(dist >= first, rb_ref[bucket * n_heads + h], tile)
            tile = tile * log2e
            if idx == 0:
                tile = jnp.where(dist >= 0, tile, -jnp.inf)
            bias_ref[idx] = tile
        bias_ref[2] = jnp.full((sub, sub), rb_ref[(NUM_BUCKETS - 1) * n_heads + h] * log2e, F32)
        bias_ref[3] = jnp.full((sub, sub), -jnp.inf, F32)

    m_ref[...] = jnp.full(m_ref.shape, -jnp.inf, F32)
    l_ref[...] = jnp.zeros_like(l_ref)
    acc_ref[...] = jnp.zeros_like(acc_ref)

    chains = [(qb, m) for qb in range(n_sub) for m in range(2)]

    far_shift = rb_ref[(NUM_BUCKETS - 1) * n_heads + h] * log2e

    def produce(j, c):
        qb, m = chains[c]
        k0 = pl.multiple_of(j * tq, tq)
        s_ref[c] = _dot_nt(k_ref[pl.ds(k0, tq), m * d:(m + 1) * d],
                           q_ref[qb * sub:(qb + 1) * sub, m * d:(m + 1) * d])

    def softmax(j, c, far):
        qb, m = chains[c]
        if far:
            s = s_ref[c]
            shift = far_shift
        else:
            parts = []
            for kb in range(n_sub):
                off = (qi - j) * n_sub + (qb - kb)
                idx = jnp.where(off < 0, 3, jnp.minimum(off, 2))
                parts.append(s_ref[c, kb * sub:(kb + 1) * sub, :] + bias_ref[idx])
            s = jnp.concatenate(parts, axis=0)
            shift = 0.0
        cols = slice(qb * sub, (qb + 1) * sub)
        m_prev = m_ref[m, :, cols]
        m_new = jnp.maximum(m_prev, jnp.max(s, axis=0, keepdims=True) + shift)
        alpha = jnp.exp2(m_prev - m_new)
        pr = jnp.exp2(s - (m_new - shift))
        l_ref[m, :, cols] = alpha * l_ref[m, :, cols] + jnp.sum(pr, axis=0, keepdims=True)
        acc_ref[m, :, cols] = alpha * acc_ref[m, :, cols]
        m_ref[m, :, cols] = m_new
        return pr.astype(BF16)

    def add_values(j, c, p):
        qb, m = chains[c]
        cols = slice(qb * sub, (qb + 1) * sub)
        acc_ref[m, :, cols] += _dot(vt_ref[j], p)

    n_chains = len(chains)
    for c in range(n_chains):
        produce(0, c)
    p_ref[...] = softmax(0, 0, False)

    def trip(j, far):
        add_values(j, 0, p_ref[...])
        produce(j + 1, 0)
        for c in range(1, n_chains):
            add_values(j, c, softmax(j, c, far))
            produce(j + 1, c)
        p_ref[...] = softmax(j + 1, 0, far)

    n_far = jnp.maximum(qi - 2, 0)
    lax.fori_loop(0, n_far, lambda j, carry: (trip(j, True), carry)[1], 0)
    lax.fori_loop(n_far, qi, lambda j, carry: (trip(j, False), carry)[1], 0)
    add_values(qi, 0, p_ref[...])
    for c in range(1, n_chains):
        add_values(qi, c, softmax(qi, c, False))

    lam = (jnp.exp(jnp.sum(lq1_ref[...] * lk1_ref[...], axis=-1, keepdims=True))
           - jnp.exp(jnp.sum(lq2_ref[...] * lk2_ref[...], axis=-1, keepdims=True))
           + lam_init)
    o = acc_ref[0] * (1.0 / l_ref[0]) - acc_ref[1] * (lam / l_ref[1])
    o = o * lax.rsqrt(jnp.mean(o * o, axis=0, keepdims=True) + 1e-5) * dfn_ref[...]
    y_ref[...] = (o * (1.0 - lam_init)).T.astype(y_ref.dtype)


def _diff_attn(proj3, v_t, rel_bias, lq1, lk1, lq2, lk2, df_norm_col, *, n_heads, col0, tq,
               lam_init):
    B, S, _ = proj3.shape
    d2 = 2 * DF_HEAD_DIM
    assert tq % ATTN_SUB == 0 and ATTN_SUB >= MAX_DISTANCE and S % tq == 0
    nq = S // tq
    n_chains = 2 * (tq // ATTN_SUB)
    cb = col0 // d2
    vec = lambda n: pl.BlockSpec((1, n), lambda b, h, i: (0, 0))
    return pl.pallas_call(
        functools.partial(_diff_attn_kernel, tq=tq, n_heads=n_heads, lam_init=lam_init),
        grid=(B, n_heads, nq),
        in_specs=[
            pl.BlockSpec(memory_space=pltpu.SMEM),
            pl.BlockSpec((None, tq, d2), lambda b, h, i: (b, i, cb + h)),
            pl.BlockSpec((None, S, d2), lambda b, h, i: (b, 0, cb + n_heads + h)),
            pl.BlockSpec((None, nq, d2, tq), lambda b, h, i: (b, 0, h, 0)),
            vec(DF_HEAD_DIM), vec(DF_HEAD_DIM), vec(DF_HEAD_DIM), vec(DF_HEAD_DIM),
            pl.BlockSpec((d2, 1), lambda b, h, i: (0, 0)),
        ],
        out_specs=pl.BlockSpec((None, tq, d2), lambda b, h, i: (b, i, h)),
        out_shape=jax.ShapeDtypeStruct((B, S, n_heads * d2), BF16),
        scratch_shapes=[
            pltpu.VMEM((4, ATTN_SUB, ATTN_SUB), F32),
            pltpu.VMEM((n_chains, tq, ATTN_SUB), F32),
            pltpu.VMEM((tq, ATTN_SUB), BF16),
            pltpu.VMEM((2, 1, tq), F32),
            pltpu.VMEM((2, 1, tq), F32),
            pltpu.VMEM((2, d2, tq), F32),
        ],
        compiler_params=pltpu.CompilerParams(
            dimension_semantics=("arbitrary", "arbitrary", "arbitrary"),
            vmem_limit_bytes=V7X_VMEM_LIMIT),
        name="diff_attn",
    )(rel_bias, proj3, proj3, v_t, lq1, lk1, lq2, lk2, df_norm_col)


def _out_proj_kernel(x_ref, ya_ref, yb_ref, wa_ref, wb_ref, h_ref):
    h_ref[...] = x_ref[...] + _dot(ya_ref[...], wa_ref[...]) + _dot(yb_ref[...], wb_ref[...])


def _out_proj(x2, y_dn, y_df, w_o, *, tm, tn):
    T, D = x2.shape
    ka = y_dn.shape[1]
    kb = y_df.shape[1]
    assert ka == kb
    return pl.pallas_call(
        _out_proj_kernel,
        grid=(T // tm, D // tn),
        in_specs=[
            pl.BlockSpec((tm, tn), lambda i, j: (i, j)),
            pl.BlockSpec((tm, ka), lambda i, j: (i, 0)),
            pl.BlockSpec((tm, kb), lambda i, j: (i, 0)),
            pl.BlockSpec((ka, tn), lambda i, j: (0, j)),
            pl.BlockSpec((kb, tn), lambda i, j: (1, j)),
        ],
        out_specs=pl.BlockSpec((tm, tn), lambda i, j: (i, j)),
        out_shape=jax.ShapeDtypeStruct((T, D), F32),
        compiler_params=pltpu.CompilerParams(
            dimension_semantics=("arbitrary", "arbitrary"),
            vmem_limit_bytes=V7X_VMEM_LIMIT),
        name="out_proj",
    )(x2, y_dn, y_df, w_o, w_o)


def _mlp_kernel(h_ref, g_ref, wu_ref, wd_ref, gf_ref, o_ref, u_ref, acc_ref):
    f = pl.program_id(1)

    @pl.when(f == 0)
    def _():
        x = h_ref[...]
        ms = jnp.mean(x * x, axis=-1, keepdims=True)
        u_ref[...] = (x * lax.rsqrt(ms + 1e-6) * g_ref[...]).astype(BF16)
        acc_ref[...] = jnp.zeros_like(acc_ref)

    hid = jnp.maximum(_dot(u_ref[...], wu_ref[...]), 0.0)
    acc_ref[...] += _dot((hid * hid).astype(BF16), wd_ref[...])

    @pl.when(f == pl.num_programs(1) - 1)
    def _():
        y = h_ref[...] + acc_ref[...]
        ms = jnp.mean(y * y, axis=-1, keepdims=True)
        o_ref[...] = y * lax.rsqrt(ms + 1e-6) * gf_ref[...]


def _mlp(h1, gain, w_up, w_down, final_gain, *, tm, tf):
    T, D = h1.shape
    Fdim = w_up.shape[1]
    return pl.pallas_call(
        _mlp_kernel,
        grid=(T // tm, Fdim // tf),
        in_specs=[
            pl.BlockSpec((tm, D), lambda i, f: (i, 0)),
            pl.BlockSpec((1, D), lambda i, f: (0, 0)),
            pl.BlockSpec((D, tf), lambda i, f: (0, f)),
            pl.BlockSpec((tf, D), lambda i, f: (f, 0)),
            pl.BlockSpec((1, D), lambda i, f: (0, 0)),
        ],
        out_specs=pl.BlockSpec((tm, D), lambda i, f: (i, 0)),
        out_shape=jax.ShapeDtypeStruct((T, D), F32),
        scratch_shapes=[pltpu.VMEM((tm, D), BF16), pltpu.VMEM((tm, D), F32)],
        compiler_params=pltpu.CompilerParams(
            dimension_semantics=("arbitrary", "arbitrary"),
            vmem_limit_bytes=V7X_VMEM_LIMIT),
        name="mlp",
    )(h1, gain, w_up, w_down, final_gain)


def _tile(n, pref):
    if n <= pref:
        return n
    t = pref - pref % 128
    while t > 128 and n % t:
        t -= 128
    assert n % t == 0
    return t


def kernel(x, attn_norm, w_in, conv_w, a_log, dt_bias, dn_norm, lambda_q1, lambda_k1,
           lambda_q2, lambda_k2, df_norm, rel_bias, w_o, mlp_norm, w_up, w_down, final_norm):
    B, S, D = x.shape
    depth = attn_norm.shape[0]
    n_dn = a_log.shape[1]
    n_df = rel_bias.shape[1]
    dn_dim = n_dn * DN_HEAD_DIM
    df_dim = n_df * 2 * DF_HEAD_DIM
    T = B * S
    gate0 = 4 * dn_dim
    assert w_in.shape[2] == gate0 + 2 * n_dn + 3 * df_dim and 2 * n_dn <= GATE_COLS

    assert depth == 1
    l = 0
    h = x.reshape(T, D)

    wl = w_in[l]
    dfq0 = gate0 + 2 * n_dn
    dfv0 = dfq0 + 2 * df_dim
    w_all = wl[:, :gate0 + GATE_COLS].astype(BF16)
    w_b = wl[:, dfq0:dfv0].astype(BF16)
    w_vt = wl[:, dfv0:].T.astype(BF16)
    w_o_bf = w_o[l].astype(BF16)
    w_up_bf = w_up[l].astype(BF16)
    w_down_bf = w_down[l].astype(BF16)

    proj, gates, v_t = _in_proj(h, attn_norm[l][None, :], w_all, w_b, w_vt, conv_w[l],
                                batch=B, dn_dim=dn_dim, df_dim=df_dim, tm=_tile(S, IN_PROJ_ROWS),
                                tn=_tile(math.gcd(dn_dim, df_dim), IN_PROJ_COLS),
                                tv=_tile(S, ATTN_BLOCK),
                                q_scale=DF_HEAD_DIM ** -0.5 * math.log2(math.e))
    proj3 = proj.reshape(B, S, -1)
    gate3 = gates.reshape(B, S, GATE_COLS)

    gate_pad = ((0, 0), (n_dn, GATE_COLS - 2 * n_dn))
    alog_row = jnp.pad(a_log[l][None, :], gate_pad)
    dtb_row = jnp.pad(dt_bias[l][None, :], gate_pad)
    y_dn = _gdn(proj3, gate3, alog_row, dtb_row, dn_norm[l][None, :],
                n_heads=n_dn, blk=_tile(S, GDN_BLOCK))
    lam_init = 0.8 - 0.6 * math.exp(-0.3 * l)
    y_df = _diff_attn(proj3, v_t, rel_bias.reshape(-1),
                      lambda_q1[l][None, :], lambda_k1[l][None, :],
                      lambda_q2[l][None, :], lambda_k2[l][None, :], df_norm[l][:, None],
                      n_heads=n_df, col0=gate0, tq=_tile(S, ATTN_BLOCK), lam_init=lam_init)

    h1 = _out_proj(h, y_dn.reshape(T, dn_dim), y_df.reshape(T, df_dim), w_o_bf,
                   tm=_tile(T, 512), tn=_tile(D, 2048))
    out = _mlp(h1, mlp_norm[l][None, :], w_up_bf, w_down_bf, final_norm[None, :],
               tm=_tile(T, 512), tf=_tile(w_up_bf.shape[1], 1024))
    return out.reshape(B, S, D)
```

```python
import functools
import math

import numpy as np
import jax
import jax.numpy as jnp
from jax import lax
from jax.experimental import pallas as pl
from jax.experimental.pallas import tpu as pltpu

F32 = jnp.float32
BF16 = jnp.bfloat16

DN_HEAD_DIM = 128
DF_HEAD_DIM = 128
CONV_WIDTH = 4
CHUNK = 64
NUM_BUCKETS = 32
MAX_DISTANCE = 128
GATE_COLS = 128
IN_PROJ_ROWS = 1024
IN_PROJ_COLS = 1024
IN_PROJ_SUB = 256
GDN_BLOCK = 512
GDN_GROUP = 4
ATTN_SUB = 256
ATTN_BLOCK = 512
ATTN_QBLOCK = 1024

V7X_VMEM_LIMIT = 58 * 1024 * 1024


def _dot(a, b):
    return jnp.dot(a, b, preferred_element_type=F32)


def _dot_nt(a, b):
    return lax.dot_general(a, b, (((1,), (1,)), ((), ())), preferred_element_type=F32)


def _sigmoid(x):
    return 1.0 / (1.0 + jnp.exp(-x))


def _in_proj_kernel(x_ref, g_ref, wa_ref, wb_ref, wg_ref, wvt_ref, cw_ref, proj_ref, gate_ref,
                    vt_ref, u_ref, hist_ref, cbuf_ref, *, bounds, per_seq, q_scale):
    i = pl.program_id(0)
    j = pl.program_id(1)
    tm, tn = proj_ref.shape
    b0, b1, b2, b3, b4, b5 = bounds

    @pl.when(j == 0)
    def _():
        x = x_ref[...]
        ms = jnp.mean(x * x, axis=-1, keepdims=True)
        u = (x * lax.rsqrt(ms + 1e-6) * g_ref[...]).astype(BF16)
        u_ref[...] = u
        gate_ref[...] = _dot(u, wg_ref[...])

    n_col_chunks = tn // IN_PROJ_SUB
    first_of_seq = (i % per_seq) == 0

    def chunk_cols(c):
        return slice(c * IN_PROJ_SUB, (c + 1) * IN_PROJ_SUB)

    def raw_cols(w_ref, c):
        cbuf_ref[8:8 + tm, chunk_cols(c)] = _dot(u_ref[...], w_ref[:, chunk_cols(c)])

    def raw(c):
        return cbuf_ref[8:8 + tm, chunk_cols(c)]

    def conv_silu(c):
        cw = cw_ref[:, chunk_cols(c)]
        y = None
        for s in range(CONV_WIDTH):
            tap = CONV_WIDTH - 1 - s
            term = cbuf_ref[8 - s:8 - s + tm, chunk_cols(c)] * cw[tap:tap + 1, :]
            y = term if y is None else y + term
        return y * _sigmoid(y)

    def l2norm_heads(y, scale):
        outs = []
        for h in range(IN_PROJ_SUB // DN_HEAD_DIM):
            yh = y[:, h * DN_HEAD_DIM:(h + 1) * DN_HEAD_DIM]
            outs.append(yh * (lax.rsqrt(jnp.sum(yh * yh, axis=-1, keepdims=True) + 1e-6) * scale))
        return jnp.concatenate(outs, axis=1)

    def silu_cols(c):
        z = raw(c)
        return z * _sigmoid(z)

    def project(w_ref, epilogue, conv=False):
        if conv:
            cbuf_ref[0:8, :] = jnp.where(first_of_seq, 0.0, hist_ref[j])
        raw_cols(w_ref, 0)
        for c in range(n_col_chunks):
            if c + 1 < n_col_chunks:
                raw_cols(w_ref, c + 1)
            proj_ref[:, chunk_cols(c)] = epilogue(c).astype(BF16)
        if conv:
            hist_ref[j] = cbuf_ref[tm:tm + 8, :]

    @pl.when(j < b0)
    def _():
        project(wa_ref, lambda c: l2norm_heads(conv_silu(c), DN_HEAD_DIM ** -0.5), conv=True)

    @pl.when((j >= b0) & (j < b1))
    def _():
        project(wa_ref, lambda c: l2norm_heads(conv_silu(c), 1.0), conv=True)

    @pl.when((j >= b1) & (j < b2))
    def _():
        project(wa_ref, conv_silu, conv=True)

    @pl.when((j >= b2) & (j < b3))
    def _():
        project(wa_ref, silu_cols)

    @pl.when((j >= b3) & (j < b4))
    def _():
        project(wb_ref, lambda c: raw(c) * q_scale)

    @pl.when((j >= b4) & (j < b5))
    def _():
        project(wb_ref, raw)

    @pl.when(j >= b5)
    def _():
        vt = _dot_nt(wvt_ref[...], u_ref[...]).astype(BF16)
        tv = vt_ref.shape[-1]
        for c in range(vt_ref.shape[0]):
            vt_ref[c] = vt[:, c * tv:(c + 1) * tv]


def _in_proj(x2, gain, w_all, w_b, w_vt, conv_w, *, batch, dn_dim, df_dim, tm, tn, tv, q_scale):
    T, D = x2.shape
    N = 4 * dn_dim + w_b.shape[1]
    nv = w_vt.shape[0]
    S = T // batch
    assert w_b.shape[1] == 2 * df_dim and nv == df_dim
    assert (4 * dn_dim) % GATE_COLS == 0 and w_all.shape[1] >= 4 * dn_dim + GATE_COLS
    assert S % tm == 0 and dn_dim % tn == 0 and df_dim % tn == 0 and tm % tv == 0
    assert tn % DN_HEAD_DIM == 0 and conv_w.shape == (CONV_WIDTH, 3 * dn_dim)
    assert tn % IN_PROJ_SUB == 0 and IN_PROJ_SUB % DN_HEAD_DIM == 0
    n_dn, n_df = dn_dim // tn, df_dim // tn
    bounds = (n_dn, 2 * n_dn, 3 * n_dn, 4 * n_dn, 4 * n_dn + n_df, 4 * n_dn + 2 * n_df)
    n_main = bounds[-1]
    n_conv = bounds[2]
    n_a = bounds[3]
    n_b = n_main - n_a
    per_seq = S // tm
    return pl.pallas_call(
        functools.partial(_in_proj_kernel, bounds=bounds, per_seq=per_seq, q_scale=q_scale),
        grid=(T // tm, n_main + nv // tn),
        in_specs=[
            pl.BlockSpec((tm, D), lambda i, j: (i, 0)),
            pl.BlockSpec((1, D), lambda i, j: (0, 0)),
            pl.BlockSpec((D, tn), lambda i, j: (0, jnp.where(j < n_a, j, 0))),
            pl.BlockSpec((D, tn), lambda i, j: (0, jnp.where(j < n_a, n_b - 1,
                                                             jnp.minimum(j - n_a, n_b - 1)))),
            pl.BlockSpec((D, GATE_COLS), lambda i, j: (0, 4 * dn_dim // GATE_COLS),
                         pipeline_mode=pl.Buffered(1)),
            pl.BlockSpec((tn, D), lambda i, j: (jnp.maximum(j - n_main, 0), 0),
                         pipeline_mode=pl.Buffered(1 if nv == tn else 2)),
            pl.BlockSpec((CONV_WIDTH, tn), lambda i, j: (0, jnp.minimum(j, n_conv - 1))),
        ],
        out_specs=[
            pl.BlockSpec((tm, tn), lambda i, j: (i, jnp.minimum(j, n_main - 1))),
            pl.BlockSpec((tm, GATE_COLS), lambda i, j: (i, 0)),
            pl.BlockSpec((None, tm // tv, tn, tv),
                         lambda i, j: (i // per_seq, i % per_seq, jnp.maximum(j - n_main, 0), 0)),
        ],
        out_shape=[
            jax.ShapeDtypeStruct((T, N), BF16),
            jax.ShapeDtypeStruct((T, GATE_COLS), F32),
            jax.ShapeDtypeStruct((batch, S // tv, nv, tv), BF16),
        ],
        scratch_shapes=[
            pltpu.VMEM((tm, D), BF16),
            pltpu.VMEM((n_conv, 8, tn), F32),
            pltpu.VMEM((8 + tm, tn), F32),
        ],
        compiler_params=pltpu.CompilerParams(
            dimension_semantics=("arbitrary", "arbitrary"),
            vmem_limit_bytes=V7X_VMEM_LIMIT),
        name="in_proj",
    )(x2, gain, w_all, w_b, w_all, w_vt, conv_w)


def _gdn_kernel(q_ref, k_ref, v_ref, z_ref, gate_ref, alog_ref, dtb_ref, dnn_ref,
                y_ref, state_ref, *, blk, n_heads):
    dk = DN_HEAD_DIM
    n_chunks = blk // CHUNK

    @pl.when(pl.program_id(1) == 0)
    def _():
        state_ref[...] = jnp.zeros_like(state_ref)

    gate = gate_ref[...]
    beta_all = _sigmoid(gate)
    xs = gate + dtb_ref[...]
    softplus = jnp.maximum(xs, 0.0) + jnp.log(1.0 + jnp.exp(-jnp.abs(xs)))
    g_all = -jnp.exp(alog_ref[...]) * softplus
    pos = lax.broadcasted_iota(jnp.int32, g_all.shape, 0) & (CHUNK - 1)
    gc_all = g_all
    step = 1
    while step < CHUNK:
        gc_all = gc_all + jnp.where(pos >= step, pltpu.roll(gc_all, step, 0), 0.0)
        step *= 2

    ri = lax.broadcasted_iota(jnp.int32, (CHUNK, CHUNK), 0)
    ci = lax.broadcasted_iota(jnp.int32, (CHUNK, CHUNK), 1)
    tril = ri >= ci
    strict = ri > ci
    eye = ri == ci
    gain = dnn_ref[...]

    def first_stage(c):
        rows = slice(c * CHUNK, (c + 1) * CHUNK)
        items = []
        for h in range(n_heads):
            cols = slice(h * dk, (h + 1) * dk)
            q_bf = q_ref[rows, cols]
            k_bf = k_ref[rows, cols]
            qc = q_bf.astype(F32)
            kc = k_bf.astype(F32)
            vc = v_ref[rows, cols].astype(F32)
            bc = jnp.broadcast_to(beta_all[rows, h:h + 1], (CHUNK, dk))
            gcc = jnp.broadcast_to(gc_all[rows, n_heads + h:n_heads + h + 1], (CHUNK, dk))
            g_last = gcc[CHUNK - 1:CHUNK, :]
            eg = jnp.exp(gcc)
            g_sq = gcc[:, 0:CHUNK]
            g_row = jnp.sum(jnp.where(eye, g_sq, 0.0), axis=0, keepdims=True)
            decay = jnp.exp(jnp.where(tril, g_sq - g_row, -jnp.inf))
            kb = kc * bc
            lhs = jnp.concatenate([kb.astype(BF16), q_bf], axis=0)
            aq = _dot_nt(lhs, k_bf)
            n_mat = jnp.where(strict, -(aq[:CHUNK] * decay), 0.0)
            items.append(dict(
                h=h, qe=qc * eg, g_last=g_last,
                rhs=jnp.concatenate([kb * eg, vc * bc], axis=1).astype(BF16),
                qk=jnp.where(tril, aq[CHUNK:] * decay, 0.0).astype(BF16),
                kd_t=(kc * jnp.exp(g_last - gcc)).T.astype(BF16),
                power=n_mat, inv=jnp.where(eye, 1.0, 0.0) + n_mat))
        return items

    def matrix_stages(items):
        span = 2
        while span < CHUNK:
            for s in items:
                pb = s["power"].astype(BF16)
                s["power"] = _dot(pb, pb)
            for s in items:
                s["inv"] = s["inv"] + _dot(s["inv"].astype(BF16), s["power"].astype(BF16))
            span *= 2
        for s in items:
            s["wu"] = _dot(s["inv"].astype(BF16), s["rhs"]).astype(BF16)
        for s in items:
            s["gr"] = _dot(s["kd_t"], s["wu"])
            qw = _dot(s["qk"], s["wu"])
            s["q_eff"] = (s["qe"] - qw[:, :dk]).astype(BF16)
            s["p_loc"] = qw[:, dk:]

    def state_stage(c, items):
        rows = slice(c * CHUNK, (c + 1) * CHUNK)
        for s in items:
            h = s["h"]
            state = state_ref[h]
            s_bf = state.astype(BF16)
            o = _dot(s["q_eff"], s_bf) + s["p_loc"]
            state_ref[h] = (state * jnp.exp(s["g_last"])
                            - _dot(s["gr"][:, :dk].astype(BF16), s_bf) + s["gr"][:, dk:])
            zs = z_ref[rows, h * dk:(h + 1) * dk].astype(F32)
            o = o * lax.rsqrt(jnp.mean(o * o, axis=-1, keepdims=True) + 1e-6) * gain
            y_ref[rows, h * dk:(h + 1) * dk] = (o * zs).astype(y_ref.dtype)

    groups = [list(range(g, min(g + GDN_GROUP, n_chunks))) for g in range(0, n_chunks, GDN_GROUP)]
    cur = [first_stage(c) for c in groups[0]]
    for gi, chunk_ids in enumerate(groups):
        nxt = [first_stage(c) for c in groups[gi + 1]] if gi + 1 < len(groups) else None
        matrix_stages([s for items in cur for s in items])
        for c, items in zip(chunk_ids, cur):
            state_stage(c, items)
        cur = nxt


def _gdn(proj3, gate3, alog_row, dtb_row, dn_norm, *, n_heads, blk):
    B, S, _ = proj3.shape
    dk = DN_HEAD_DIM
    dn = n_heads * dk

    def group(idx):
        return pl.BlockSpec((None, blk, dn), lambda b, t: (b, t, idx))

    def whole(arr):
        return pl.BlockSpec(arr.shape, lambda b, t: (0,) * arr.ndim)

    return pl.pallas_call(
        functools.partial(_gdn_kernel, blk=blk, n_heads=n_heads),
        grid=(B, S // blk),
        in_specs=[
            group(0), group(1), group(2), group(3),
            pl.BlockSpec((None, blk, GATE_COLS), lambda b, t: (b, t, 0)),
            whole(alog_row), whole(dtb_row), whole(dn_norm),
        ],
        out_specs=pl.BlockSpec((None, blk, dn), lambda b, t: (b, t, 0)),
        out_shape=jax.ShapeDtypeStruct((B, S, dn), BF16),
        scratch_shapes=[pltpu.VMEM((n_heads, dk, dk), F32)],
        compiler_params=pltpu.CompilerParams(
            dimension_semantics=("arbitrary", "arbitrary"),
            vmem_limit_bytes=V7X_VMEM_LIMIT),
        name="gdn",
    )(proj3, proj3, proj3, proj3, gate3, alog_row, dtb_row, dn_norm)


def _t5_bucket_starts():
    max_exact = NUM_BUCKETS // 2
    n = np.arange(0, MAX_DISTANCE + 1)
    nf = np.maximum(n, 1).astype(np.float32)
    large = max_exact + (np.log(nf / max_exact) / math.log(MAX_DISTANCE / max_exact)
                         * (NUM_BUCKETS - max_exact)).astype(np.int32)
    bucket = np.where(n < max_exact, n, np.minimum(large, NUM_BUCKETS - 1))
    assert bucket[MAX_DISTANCE] == NUM_BUCKETS - 1 and np.all(np.diff(bucket) >= 0)
    starts = [(0, int(bucket[0]))]
    for d in range(1, MAX_DISTANCE + 1):
        if bucket[d] != bucket[d - 1]:
            starts.append((d, int(bucket[d])))
    return starts


def _diff_attn_kernel(rb_ref, q_ref, k_ref, vt_ref, lq1_ref, lk1_ref, lq2_ref, lk2_ref, dfn_ref,
                      y_ref, bias_ref, s_ref, p_ref, m_ref, l_ref, acc_ref,
                      *, tq, tk, n_heads, lam_init):
    h = pl.program_id(1)
    qi = pl.program_id(2)
    d = DF_HEAD_DIM
    sub = ATTN_SUB
    nsq = tq // sub
    nsk = tk // sub
    log2e = math.log2(math.e)

    @pl.when(qi == 0)
    def _():
        keys = lax.broadcasted_iota(jnp.int32, (sub, sub), 0)
        qrys = lax.broadcasted_iota(jnp.int32, (sub, sub), 1)
        starts = _t5_bucket_starts()
        for idx in range(2):
            dist = qrys - keys + idx * sub
            tile = jnp.full((sub, sub), rb_ref[starts[0][1] * n_heads + h], F32)
            for first, bucket in starts[1:]:
                tile = jnp.where(dist >= first, rb_ref[bucket * n_heads + h], tile)
            tile = tile * log2e
            if idx == 0:
                tile = jnp.where(dist >= 0, tile, -jnp.inf)
            bias_ref[idx] = tile
        bias_ref[2] = jnp.full((sub, sub), rb_ref[(NUM_BUCKETS - 1) * n_heads + h] * log2e, F32)
        bias_ref[3] = jnp.full((sub, sub), -jnp.inf, F32)

    m_ref[...] = jnp.full(m_ref.shape, -jnp.inf, F32)
    l_ref[...] = jnp.zeros_like(l_ref)
    acc_ref[...] = jnp.zeros_like(acc_ref)

    chains = [(qb, m) for qb in range(nsq) for m in range(2)]

    far_shift = rb_ref[(NUM_BUCKETS - 1) * n_heads + h] * log2e

    def produce(j, c):
        qb, m = chains[c]
        k0 = pl.multiple_of(j * tk, tk)
        s_ref[c] = _dot_nt(k_ref[pl.ds(k0, tk), m * d:(m + 1) * d],
                           q_ref[qb * sub:(qb + 1) * sub, m * d:(m + 1) * d])

    def softmax(j, c, far):
        qb, m = chains[c]
        if far:
            s = s_ref[c]
            shift = far_shift
        else:
            parts = []
            for kb in range(nsk):
                off = (qi * nsq + qb) - (j * nsk + kb)
                idx = jnp.where(off < 0, 3, jnp.minimum(off, 2))
                parts.append(s_ref[c, kb * sub:(kb + 1) * sub, :] + bias_ref[idx])
            s = jnp.concatenate(parts, axis=0)
            shift = 0.0
        cols = slice(qb * sub, (qb + 1) * sub)
        m_prev = m_ref[m, :, cols]
        m_new = jnp.maximum(m_prev, jnp.max(s, axis=0, keepdims=True) + shift)
        alpha = jnp.exp2(m_prev - m_new)
        pr = jnp.exp2(s - (m_new - shift))
        l_ref[m, :, cols] = alpha * l_ref[m, :, cols] + jnp.sum(pr, axis=0, keepdims=True)
        acc_ref[m, :, cols] = alpha * acc_ref[m, :, cols]
        m_ref[m, :, cols] = m_new
        return pr.astype(BF16)

    def add_values(j, c, p):
        qb, m = chains[c]
        cols = slice(qb * sub, (qb + 1) * sub)
        acc_ref[m, :, cols] += _dot(vt_ref[j], p)

    n_chains = len(chains)
    for c in range(n_chains):
        produce(0, c)
    p_ref[...] = softmax(0, 0, False)

    def trip(j, far):
        add_values(j, 0, p_ref[...])
        produce(j + 1, 0)
        for c in range(1, n_chains):
            add_values(j, c, softmax(j, c, far))
            produce(j + 1, c)
        p_ref[...] = softmax(j + 1, 0, far)

    last = (qi + 1) * (tq // tk) - 1
    n_far = jnp.maximum((qi * nsq - 1) // nsk - 1, 0)
    lax.fori_loop(0, n_far, lambda j, carry: (trip(j, True), carry)[1], 0)
    lax.fori_loop(n_far, last, lambda j, carry: (trip(j, False), carry)[1], 0)
    live = [c for c in range(n_chains) if chains[c][0] >= nsq - nsk]
    for c in live:
        add_values(last, c, p_ref[...] if c == 0 else softmax(last, c, False))

    lam = (jnp.exp(jnp.sum(lq1_ref[...] * lk1_ref[...], axis=-1, keepdims=True))
           - jnp.exp(jnp.sum(lq2_ref[...] * lk2_ref[...], axis=-1, keepdims=True))
           + lam_init)
    o = acc_ref[0] * (1.0 / l_ref[0]) - acc_ref[1] * (lam / l_ref[1])
    o = o * lax.rsqrt(jnp.mean(o * o, axis=0, keepdims=True) + 1e-5) * dfn_ref[...]
    y_ref[...] = (o * (1.0 - lam_init)).T.astype(y_ref.dtype)


def _diff_attn(proj3, v_t, rel_bias, lq1, lk1, lq2, lk2, df_norm_col, *, n_heads, col0, tq, tk,
               lam_init):
    B, S, _ = proj3.shape
    d2 = 2 * DF_HEAD_DIM
    assert tq % tk == 0 and tk % ATTN_SUB == 0 and ATTN_SUB >= MAX_DISTANCE and S % tq == 0
    nq = S // tq
    n_chains = 2 * (tq // ATTN_SUB)
    cb = col0 // d2
    vec = lambda n: pl.BlockSpec((1, n), lambda b, h, i: (0, 0))
    return pl.pallas_call(
        functools.partial(_diff_attn_kernel, tq=tq, tk=tk, n_heads=n_heads, lam_init=lam_init),
        grid=(B, n_heads, nq),
        in_specs=[
            pl.BlockSpec(memory_space=pltpu.SMEM),
            pl.BlockSpec((None, tq, d2), lambda b, h, i: (b, i, cb + h)),
            pl.BlockSpec((None, S, d2), lambda b, h, i: (b, 0, cb + n_heads + h)),
            pl.BlockSpec((None, S // tk, d2, tk), lambda b, h, i: (b, 0, h, 0)),
            vec(DF_HEAD_DIM), vec(DF_HEAD_DIM), vec(DF_HEAD_DIM), vec(DF_HEAD_DIM),
            pl.BlockSpec((d2, 1), lambda b, h, i: (0, 0)),
        ],
        out_specs=pl.BlockSpec((None, tq, d2), lambda b, h, i: (b, i, h)),
        out_shape=jax.ShapeDtypeStruct((B, S, n_heads * d2), BF16),
        scratch_shapes=[
            pltpu.VMEM((4, ATTN_SUB, ATTN_SUB), F32),
            pltpu.VMEM((n_chains, tk, ATTN_SUB), F32),
            pltpu.VMEM((tk, ATTN_SUB), BF16),
            pltpu.VMEM((2, 1, tq), F32),
            pltpu.VMEM((2, 1, tq), F32),
            pltpu.VMEM((2, d2, tq), F32),
        ],
        compiler_params=pltpu.CompilerParams(
            dimension_semantics=("arbitrary", "arbitrary", "arbitrary"),
            vmem_limit_bytes=V7X_VMEM_LIMIT),
        name="diff_attn",
    )(rel_bias, proj3, proj3, v_t, lq1, lk1, lq2, lk2, df_norm_col)


def _out_proj_kernel(x_ref, ya_ref, yb_ref, wa_ref, wb_ref, h_ref):
    h_ref[...] = x_ref[...] + _dot(ya_ref[...], wa_ref[...]) + _dot(yb_ref[...], wb_ref[...])


def _out_proj(x2, y_dn, y_df, w_o, *, tm, tn):
    T, D = x2.shape
    ka = y_dn.shape[1]
    kb = y_df.shape[1]
    assert ka == kb
    return pl.pallas_call(
        _out_proj_kernel,
        grid=(T // tm, D // tn),
        in_specs=[
            pl.BlockSpec((tm, tn), lambda i, j: (i, j)),
            pl.BlockSpec((tm, ka), lambda i, j: (i, 0)),
            pl.BlockSpec((tm, kb), lambda i, j: (i, 0)),
            pl.BlockSpec((ka, tn), lambda i, j: (0, j)),
            pl.BlockSpec((kb, tn), lambda i, j: (1, j)),
        ],
        out_specs=pl.BlockSpec((tm, tn), lambda i, j: (i, j)),
        out_shape=jax.ShapeDtypeStruct((T, D), F32),
        compiler_params=pltpu.CompilerParams(
            dimension_semantics=("arbitrary", "arbitrary"),
            vmem_limit_bytes=V7X_VMEM_LIMIT),
        name="out_proj",
    )(x2, y_dn, y_df, w_o, w_o)


def _mlp_kernel(h_ref, g_ref, wu_ref, wd_ref, gf_ref, o_ref, u_ref, acc_ref):
    f = pl.program_id(1)

    @pl.when(f == 0)
    def _():
        x = h_ref[...]
        ms = jnp.mean(x * x, axis=-1, keepdims=True)
        u_ref[...] = (x * lax.rsqrt(ms + 1e-6) * g_ref[...]).astype(BF16)
        acc_ref[...] = jnp.zeros_like(acc_ref)

    hid = jnp.maximum(_dot(u_ref[...], wu_ref[...]), 0.0)
    acc_ref[...] += _dot((hid * hid).astype(BF16), wd_ref[...])

    @pl.when(f == pl.num_programs(1) - 1)
    def _():
        y = h_ref[...] + acc_ref[...]
        ms = jnp.mean(y * y, axis=-1, keepdims=True)
        o_ref[...] = y * lax.rsqrt(ms + 1e-6) * gf_ref[...]


def _mlp(h1, gain, w_up, w_down, final_gain, *, tm, tf):
    T, D = h1.shape
    Fdim = w_up.shape[1]
    return pl.pallas_call(
        _mlp_kernel,
        grid=(T // tm, Fdim // tf),
        in_specs=[
            pl.BlockSpec((tm, D), lambda i, f: (i, 0)),
            pl.BlockSpec((1, D), lambda i, f: (0, 0)),
            pl.BlockSpec((D, tf), lambda i, f: (0, f)),
            pl.BlockSpec((tf, D), lambda i, f: (f, 0)),
            pl.BlockSpec((1, D), lambda i, f: (0, 0)),
        ],
        out_specs=pl.BlockSpec((tm, D), lambda i, f: (i, 0)),
        out_shape=jax.ShapeDtypeStruct((T, D), F32),
        scratch_shapes=[pltpu.VMEM((tm, D), BF16), pltpu.VMEM((tm, D), F32)],
        compiler_params=pltpu.CompilerParams(
            dimension_semantics=("arbitrary", "arbitrary"),
            vmem_limit_bytes=V7X_VMEM_LIMIT),
        name="mlp",
    )(h1, gain, w_up, w_down, final_gain)


def _tile(n, pref):
    if n <= pref:
        return n
    t = pref - pref % 128
    while t > 128 and n % t:
        t -= 128
    assert n % t == 0
    return t


def kernel(x, attn_norm, w_in, conv_w, a_log, dt_bias, dn_norm, lambda_q1, lambda_k1,
           lambda_q2, lambda_k2, df_norm, rel_bias, w_o, mlp_norm, w_up, w_down, final_norm):
    B, S, D = x.shape
    depth = attn_norm.shape[0]
    n_dn = a_log.shape[1]
    n_df = rel_bias.shape[1]
    dn_dim = n_dn * DN_HEAD_DIM
    df_dim = n_df * 2 * DF_HEAD_DIM
    T = B * S
    gate0 = 4 * dn_dim
    assert w_in.shape[2] == gate0 + 2 * n_dn + 3 * df_dim and 2 * n_dn <= GATE_COLS

    assert depth == 1
    l = 0
    h = x.reshape(T, D)

    wl = w_in[l]
    dfq0 = gate0 + 2 * n_dn
    dfv0 = dfq0 + 2 * df_dim
    w_all = wl.astype(BF16)
    w_b = w_all[:, dfq0:dfv0]
    w_vt = w_all[:, dfv0:].T
    w_o_bf = w_o[l].astype(BF16)
    w_up_bf = w_up[l].astype(BF16)
    w_down_bf = w_down[l].astype(BF16)

    proj, gates, v_t = _in_proj(h, attn_norm[l][None, :], w_all, w_b, w_vt, conv_w[l],
                                batch=B, dn_dim=dn_dim, df_dim=df_dim, tm=_tile(S, IN_PROJ_ROWS),
                                tn=_tile(math.gcd(dn_dim, df_dim), IN_PROJ_COLS),
                                tv=_tile(S, ATTN_BLOCK),
                                q_scale=DF_HEAD_DIM ** -0.5 * math.log2(math.e))
    proj3 = proj.reshape(B, S, -1)
    gate3 = gates.reshape(B, S, GATE_COLS)

    gate_pad = ((0, 0), (n_dn, GATE_COLS - 2 * n_dn))
    alog_row = jnp.pad(a_log[l][None, :], gate_pad)
    dtb_row = jnp.pad(dt_bias[l][None, :], gate_pad)
    y_dn = _gdn(proj3, gate3, alog_row, dtb_row, dn_norm[l][None, :],
                n_heads=n_dn, blk=_tile(S, GDN_BLOCK))
    lam_init = 0.8 - 0.6 * math.exp(-0.3 * l)
    y_df = _diff_attn(proj3, v_t, rel_bias.reshape(-1),
                      lambda_q1[l][None, :], lambda_k1[l][None, :],
                      lambda_q2[l][None, :], lambda_k2[l][None, :], df_norm[l][:, None],
                      n_heads=n_df, col0=gate0, tq=_tile(S, ATTN_QBLOCK), tk=_tile(S, ATTN_BLOCK),
                      lam_init=lam_init)

    h1 = _out_proj(h, y_dn.reshape(T, dn_dim), y_df.reshape(T, df_dim), w_o_bf,
                   tm=_tile(T, 512), tn=_tile(D, 2048))
    out = _mlp(h1, mlp_norm[l][None, :], w_up_bf, w_down_bf, final_norm[None, :],
               tm=_tile(T, 512), tf=_tile(w_up_bf.shape[1], 1024))
    return out.reshape(B, S, D)
```

```python
import functools
import math

import numpy as np
import jax
import jax.numpy as jnp
from jax import lax
from jax.experimental import pallas as pl
from jax.experimental.pallas import tpu as pltpu

F32 = jnp.float32
BF16 = jnp.bfloat16

DN_HEAD_DIM = 128
DF_HEAD_DIM = 128
CONV_WIDTH = 4
CHUNK = 64
NUM_BUCKETS = 32
MAX_DISTANCE = 128
GATE_COLS = 128
IN_PROJ_ROWS = 1024
IN_PROJ_COLS = 1024
IN_PROJ_SUB = 256
GDN_BLOCK = 512
GDN_GROUP = 4
ATTN_SUB = 256
ATTN_BLOCK = 512
ATTN_QBLOCK = 1024

V7X_VMEM_LIMIT = 58 * 1024 * 1024


def _dot(a, b):
    return jnp.dot(a, b, preferred_element_type=F32)


def _dot_nt(a, b):
    return lax.dot_general(a, b, (((1,), (1,)), ((), ())), preferred_element_type=F32)


def _sigmoid(x):
    return 1.0 / (1.0 + jnp.exp(-x))


def _in_proj_kernel(x_ref, g_ref, wa_ref, wb_ref, wg_ref, wvt_ref, cw_ref, proj_ref, gate_ref,
                    vt_ref, u_ref, hist_ref, cbuf_ref, *, bounds, per_seq, q_scale):
    i = pl.program_id(0)
    j = pl.program_id(1)
    tm, tn = proj_ref.shape
    b0, b1, b2, b3, b4, b5 = bounds

    @pl.when(j == 0)
    def _():
        x = x_ref[...]
        ms = jnp.mean(x * x, axis=-1, keepdims=True)
        u = (x * lax.rsqrt(ms + 1e-6) * g_ref[...]).astype(BF16)
        u_ref[...] = u
        gate_ref[...] = _dot(u, wg_ref[...])

    n_col_chunks = tn // IN_PROJ_SUB
    first_of_seq = (i % per_seq) == 0

    def chunk_cols(c):
        return slice(c * IN_PROJ_SUB, (c + 1) * IN_PROJ_SUB)

    def raw_cols(w_ref, c):
        cbuf_ref[8:8 + tm, chunk_cols(c)] = _dot(u_ref[...], w_ref[:, chunk_cols(c)])

    def raw(c):
        return cbuf_ref[8:8 + tm, chunk_cols(c)]

    def conv_silu(c):
        cw = cw_ref[:, chunk_cols(c)]
        y = None
        for s in range(CONV_WIDTH):
            tap = CONV_WIDTH - 1 - s
            term = cbuf_ref[8 - s:8 - s + tm, chunk_cols(c)] * cw[tap:tap + 1, :]
            y = term if y is None else y + term
        return y * _sigmoid(y)

    def l2norm_heads(y, scale):
        outs = []
        for h in range(IN_PROJ_SUB // DN_HEAD_DIM):
            yh = y[:, h * DN_HEAD_DIM:(h + 1) * DN_HEAD_DIM]
            outs.append(yh * (lax.rsqrt(jnp.sum(yh * yh, axis=-1, keepdims=True) + 1e-6) * scale))
        return jnp.concatenate(outs, axis=1)

    def silu_cols(c):
        z = raw(c)
        return z * _sigmoid(z)

    def project(w_ref, epilogue, conv=False):
        if conv:
            cbuf_ref[0:8, :] = jnp.where(first_of_seq, 0.0, hist_ref[j])
        raw_cols(w_ref, 0)
        for c in range(n_col_chunks):
            if c + 1 < n_col_chunks:
                raw_cols(w_ref, c + 1)
            proj_ref[:, chunk_cols(c)] = epilogue(c).astype(BF16)
        if conv:
            hist_ref[j] = cbuf_ref[tm:tm + 8, :]

    @pl.when(j < b0)
    def _():
        project(wa_ref, lambda c: l2norm_heads(conv_silu(c), DN_HEAD_DIM ** -0.5), conv=True)

    @pl.when((j >= b0) & (j < b1))
    def _():
        project(wa_ref, lambda c: l2norm_heads(conv_silu(c), 1.0), conv=True)

    @pl.when((j >= b1) & (j < b2))
    def _():
        project(wa_ref, conv_silu, conv=True)

    @pl.when((j >= b2) & (j < b3))
    def _():
        project(wa_ref, silu_cols)

    @pl.when((j >= b3) & (j < b4))
    def _():
        project(wb_ref, lambda c: raw(c) * q_scale)

    @pl.when((j >= b4) & (j < b5))
    def _():
        project(wb_ref, raw)

    @pl.when(j >= b5)
    def _():
        vt = _dot_nt(wvt_ref[...], u_ref[...]).astype(BF16)
        tv = vt_ref.shape[-1]
        for c in range(vt_ref.shape[0]):
            vt_ref[c] = vt[:, c * tv:(c + 1) * tv]


def _in_proj(x2, gain, w_all, w_b, w_vt, conv_w, *, batch, dn_dim, df_dim, tm, tn, tv, q_scale):
    T, D = x2.shape
    N = 4 * dn_dim + w_b.shape[1]
    nv = w_vt.shape[0]
    S = T // batch
    assert w_b.shape[1] == 2 * df_dim and nv == df_dim
    assert (4 * dn_dim) % GATE_COLS == 0 and w_all.shape[1] >= 4 * dn_dim + GATE_COLS
    assert S % tm == 0 and dn_dim % tn == 0 and df_dim % tn == 0 and tm % tv == 0
    assert tn % DN_HEAD_DIM == 0 and conv_w.shape == (CONV_WIDTH, 3 * dn_dim)
    assert tn % IN_PROJ_SUB == 0 and IN_PROJ_SUB % DN_HEAD_DIM == 0
    n_dn, n_df = dn_dim // tn, df_dim // tn
    bounds = (n_dn, 2 * n_dn, 3 * n_dn, 4 * n_dn, 4 * n_dn + n_df, 4 * n_dn + 2 * n_df)
    n_main = bounds[-1]
    n_conv = bounds[2]
    n_a = bounds[3]
    n_b = n_main - n_a
    per_seq = S // tm
    return pl.pallas_call(
        functools.partial(_in_proj_kernel, bounds=bounds, per_seq=per_seq, q_scale=q_scale),
        grid=(T // tm, n_main + nv // tn),
        in_specs=[
            pl.BlockSpec((tm, D), lambda i, j: (i, 0)),
            pl.BlockSpec((1, D), lambda i, j: (0, 0)),
            pl.BlockSpec((D, tn), lambda i, j: (0, jnp.where(j < n_a, j, 0))),
            pl.BlockSpec((D, tn), lambda i, j: (0, jnp.where(j < n_a, n_b - 1,
                                                             jnp.minimum(j - n_a, n_b - 1)))),
            pl.BlockSpec((D, GATE_COLS), lambda i, j: (0, 4 * dn_dim // GATE_COLS),
                         pipeline_mode=pl.Buffered(1)),
            pl.BlockSpec((tn, D), lambda i, j: (jnp.maximum(j - n_main, 0), 0),
                         pipeline_mode=pl.Buffered(1 if nv == tn else 2)),
            pl.BlockSpec((CONV_WIDTH, tn), lambda i, j: (0, jnp.minimum(j, n_conv - 1))),
        ],
        out_specs=[
            pl.BlockSpec((tm, tn), lambda i, j: (i, jnp.minimum(j, n_main - 1))),
            pl.BlockSpec((tm, GATE_COLS), lambda i, j: (i, 0)),
            pl.BlockSpec((None, tm // tv, tn, tv),
                         lambda i, j: (i // per_seq, i % per_seq, jnp.maximum(j - n_main, 0), 0)),
        ],
        out_shape=[
            jax.ShapeDtypeStruct((T, N), BF16),
            jax.ShapeDtypeStruct((T, GATE_COLS), F32),
            jax.ShapeDtypeStruct((batch, S // tv, nv, tv), BF16),
        ],
        scratch_shapes=[
            pltpu.VMEM((tm, D), BF16),
            pltpu.VMEM((n_conv, 8, tn), F32),
            pltpu.VMEM((8 + tm, tn), F32),
        ],
        compiler_params=pltpu.CompilerParams(
            dimension_semantics=("arbitrary", "arbitrary"),
            vmem_limit_bytes=V7X_VMEM_LIMIT),
        name="in_proj",
    )(x2, gain, w_all, w_b, w_all, w_vt, conv_w)


def _gdn_kernel(q_ref, k_ref, v_ref, z_ref, gate_ref, alog_ref, dtb_ref, dnn_ref,
                y_ref, state_ref, *, blk, n_heads):
    dk = DN_HEAD_DIM
    n_chunks = blk // CHUNK

    @pl.when(pl.program_id(1) == 0)
    def _():
        state_ref[...] = jnp.zeros_like(state_ref)

    gate = gate_ref[...]
    beta_all = _sigmoid(gate)
    xs = gate + dtb_ref[...]
    softplus = jnp.maximum(xs, 0.0) + jnp.log(1.0 + jnp.exp(-jnp.abs(xs)))
    g_all = -jnp.exp(alog_ref[...]) * softplus
    pos = lax.broadcasted_iota(jnp.int32, g_all.shape, 0) & (CHUNK - 1)
    gc_all = g_all
    step = 1
    while step < CHUNK:
        gc_all = gc_all + jnp.where(pos >= step, pltpu.roll(gc_all, step, 0), 0.0)
        step *= 2

    ri = lax.broadcasted_iota(jnp.int32, (CHUNK, CHUNK), 0)
    ci = lax.broadcasted_iota(jnp.int32, (CHUNK, CHUNK), 1)
    tril = ri >= ci
    strict = ri > ci
    eye = ri == ci
    gain = dnn_ref[...]

    def first_stage(c):
        rows = slice(c * CHUNK, (c + 1) * CHUNK)
        items = []
        for h in range(n_heads):
            cols = slice(h * dk, (h + 1) * dk)
            q_bf = q_ref[rows, cols]
            k_bf = k_ref[rows, cols]
            qc = q_bf.astype(F32)
            kc = k_bf.astype(F32)
            vc = v_ref[rows, cols].astype(F32)
            bc = jnp.broadcast_to(beta_all[rows, h:h + 1], (CHUNK, dk))
            gcc = jnp.broadcast_to(gc_all[rows, n_heads + h:n_heads + h + 1], (CHUNK, dk))
            g_last = gcc[CHUNK - 1:CHUNK, :]
            eg = jnp.exp(gcc)
            g_sq = gcc[:, 0:CHUNK]
            g_row = jnp.sum(jnp.where(eye, g_sq, 0.0), axis=0, keepdims=True)
            decay = jnp.exp(jnp.where(tril, g_sq - g_row, -jnp.inf))
            kb = kc * bc
            lhs = jnp.concatenate([kb.astype(BF16), q_bf], axis=0)
            aq = _dot_nt(lhs, k_bf)
            n_mat = jnp.where(strict, -(aq[:CHUNK] * decay), 0.0)
            items.append(dict(
                h=h, qe=qc * eg, g_last=g_last,
                rhs=jnp.concatenate([kb * eg, vc * bc], axis=1).astype(BF16),
                qk=jnp.where(tril, aq[CHUNK:] * decay, 0.0).astype(BF16),
                kd_t=(kc * jnp.exp(g_last - gcc)).T.astype(BF16),
                power=n_mat, inv=jnp.where(eye, 1.0, 0.0) + n_mat))
        return items

    def matrix_stages(items):
        span = 2
        while span < CHUNK:
            for s in items:
                pb = s["power"].astype(BF16)
                s["power"] = _dot(pb, pb)
            for s in items:
                s["inv"] = s["inv"] + _dot(s["inv"].astype(BF16), s["power"].astype(BF16))
            span *= 2
        for s in items:
            s["wu"] = _dot(s["inv"].astype(BF16), s["rhs"]).astype(BF16)
        for s in items:
            s["gr"] = _dot(s["kd_t"], s["wu"])
            qw = _dot(s["qk"], s["wu"])
            s["q_eff"] = (s["qe"] - qw[:, :dk]).astype(BF16)
            s["p_loc"] = qw[:, dk:]

    def state_stage(c, items):
        rows = slice(c * CHUNK, (c + 1) * CHUNK)
        for s in items:
            h = s["h"]
            state = state_ref[h]
            s_bf = state.astype(BF16)
            o = _dot(s["q_eff"], s_bf) + s["p_loc"]
            state_ref[h] = (state * jnp.exp(s["g_last"])
                            - _dot(s["gr"][:, :dk].astype(BF16), s_bf) + s["gr"][:, dk:])
            zs = z_ref[rows, h * dk:(h + 1) * dk].astype(F32)
            o = o * lax.rsqrt(jnp.mean(o * o, axis=-1, keepdims=True) + 1e-6) * gain
            y_ref[rows, h * dk:(h + 1) * dk] = (o * zs).astype(y_ref.dtype)

    groups = [list(range(g, min(g + GDN_GROUP, n_chunks))) for g in range(0, n_chunks, GDN_GROUP)]
    cur = [first_stage(c) for c in groups[0]]
    for gi, chunk_ids in enumerate(groups):
        nxt = [first_stage(c) for c in groups[gi + 1]] if gi + 1 < len(groups) else None
        matrix_stages([s for items in cur for s in items])
        for c, items in zip(chunk_ids, cur):
            state_stage(c, items)
        cur = nxt


def _gdn(proj3, gate3, alog_row, dtb_row, dn_norm, *, n_heads, blk):
    B, S, _ = proj3.shape
    dk = DN_HEAD_DIM
    dn = n_heads * dk

    def group(idx):
        return pl.BlockSpec((None, blk, dn), lambda b, t: (b, t, idx))

    def whole(arr):
        return pl.BlockSpec(arr.shape, lambda b, t: (0,) * arr.ndim)

    return pl.pallas_call(
        functools.partial(_gdn_kernel, blk=blk, n_heads=n_heads),
        grid=(B, S // blk),
        in_specs=[
            group(0), group(1), group(2), group(3),
            pl.BlockSpec((None, blk, GATE_COLS), lambda b, t: (b, t, 0)),
            whole(alog_row), whole(dtb_row), whole(dn_norm),
        ],
        out_specs=pl.BlockSpec((None, blk, dn), lambda b, t: (b, t, 0)),
        out_shape=jax.ShapeDtypeStruct((B, S, dn), BF16),
        scratch_shapes=[pltpu.VMEM((n_heads, dk, dk), F32)],
        compiler_params=pltpu.CompilerParams(
            dimension_semantics=("arbitrary", "arbitrary"),
            vmem_limit_bytes=V7X_VMEM_LIMIT),
        name="gdn",
    )(proj3, proj3, proj3, proj3, gate3, alog_row, dtb_row, dn_norm)


def _t5_bucket_starts():
    max_exact = NUM_BUCKETS // 2
    n = np.arange(0, MAX_DISTANCE + 1)
    nf = np.maximum(n, 1).astype(np.float32)
    large = max_exact + (np.log(nf / max_exact) / math.log(MAX_DISTANCE / max_exact)
                         * (NUM_BUCKETS - max_exact)).astype(np.int32)
    bucket = np.where(n < max_exact, n, np.minimum(large, NUM_BUCKETS - 1))
    assert bucket[MAX_DISTANCE] == NUM_BUCKETS - 1 and np.all(np.diff(bucket) >= 0)
    starts = [(0, int(bucket[0]))]
    for d in range(1, MAX_DISTANCE + 1):
        if bucket[d] != bucket[d - 1]:
            starts.append((d, int(bucket[d])))
    return starts


def _diff_attn_kernel(rb_ref, q_ref, k_ref, vt_ref, lq1_ref, lk1_ref, lq2_ref, lk2_ref, dfn_ref,
                      c0_ref, c1_ref, c2_ref, y_ref, o0_ref, o1_ref, o2_ref,
                      bias_ref, s_ref, p_ref, m_ref, l_ref, acc_ref,
                      *, tq, tk, n_heads, lam_init):
    h = pl.program_id(1)
    qi = pl.program_id(2)
    d = DF_HEAD_DIM

    for c_ref, o_ref in ((c0_ref, o0_ref), (c1_ref, o1_ref), (c2_ref, o2_ref)):
        o_ref[...] = c_ref[...].astype(o_ref.dtype)

    sub = ATTN_SUB
    nsq = tq // sub
    nsk = tk // sub
    log2e = math.log2(math.e)

    @pl.when(qi == 0)
    def _():
        keys = lax.broadcasted_iota(jnp.int32, (sub, sub), 0)
        qrys = lax.broadcasted_iota(jnp.int32, (sub, sub), 1)
        starts = _t5_bucket_starts()
        for idx in range(2):
            dist = qrys - keys + idx * sub
            tile = jnp.full((sub, sub), rb_ref[starts[0][1] * n_heads + h], F32)
            for first, bucket in starts[1:]:
                tile = jnp.where(dist >= first, rb_ref[bucket * n_heads + h], tile)
            tile = tile * log2e
            if idx == 0:
                tile = jnp.where(dist >= 0, tile, -jnp.inf)
            bias_ref[idx] = tile
        bias_ref[2] = jnp.full((sub, sub), rb_ref[(NUM_BUCKETS - 1) * n_heads + h] * log2e, F32)
        bias_ref[3] = jnp.full((sub, sub), -jnp.inf, F32)

    m_ref[...] = jnp.full(m_ref.shape, -jnp.inf, F32)
    l_ref[...] = jnp.zeros_like(l_ref)
    acc_ref[...] = jnp.zeros_like(acc_ref)

    chains = [(qb, m) for qb in range(nsq) for m in range(2)]

    far_shift = rb_ref[(NUM_BUCKETS - 1) * n_heads + h] * log2e

    def produce(j, c):
        qb, m = chains[c]
        k0 = pl.multiple_of(j * tk, tk)
        s_ref[c] = _dot_nt(k_ref[pl.ds(k0, tk), m * d:(m + 1) * d],
                           q_ref[qb * sub:(qb + 1) * sub, m * d:(m + 1) * d])

    def softmax(j, c, far):
        qb, m = chains[c]
        if far:
            s = s_ref[c]
            shift = far_shift
        else:
            parts = []
            for kb in range(nsk):
                off = (qi * nsq + qb) - (j * nsk + kb)
                idx = jnp.where(off < 0, 3, jnp.minimum(off, 2))
                parts.append(s_ref[c, kb * sub:(kb + 1) * sub, :] + bias_ref[idx])
            s = jnp.concatenate(parts, axis=0)
            shift = 0.0
        cols = slice(qb * sub, (qb + 1) * sub)
        m_prev = m_ref[m, :, cols]
        m_new = jnp.maximum(m_prev, jnp.max(s, axis=0, keepdims=True) + shift)
        alpha = jnp.exp2(m_prev - m_new)
        pr = jnp.exp2(s - (m_new - shift))
        l_ref[m, :, cols] = alpha * l_ref[m, :, cols] + jnp.sum(pr, axis=0, keepdims=True)
        acc_ref[m, :, cols] = alpha * acc_ref[m, :, cols]
        m_ref[m, :, cols] = m_new
        return pr.astype(BF16)

    def add_values(j, c, p):
        qb, m = chains[c]
        cols = slice(qb * sub, (qb + 1) * sub)
        acc_ref[m, :, cols] += _dot(vt_ref[j], p)

    n_chains = len(chains)
    for c in range(n_chains):
        produce(0, c)
    p_ref[...] = softmax(0, 0, False)

    def trip(j, far):
        add_values(j, 0, p_ref[...])
        produce(j + 1, 0)
        for c in range(1, n_chains):
            add_values(j, c, softmax(j, c, far))
            produce(j + 1, c)
        p_ref[...] = softmax(j + 1, 0, far)

    last = (qi + 1) * (tq // tk) - 1
    n_far = jnp.maximum((qi * nsq - 1) // nsk - 1, 0)
    lax.fori_loop(0, n_far, lambda j, carry: (trip(j, True), carry)[1], 0)
    lax.fori_loop(n_far, last, lambda j, carry: (trip(j, False), carry)[1], 0)
    live = [c for c in range(n_chains) if chains[c][0] >= nsq - nsk]
    for c in live:
        add_values(last, c, p_ref[...] if c == 0 else softmax(last, c, False))

    lam = (jnp.exp(jnp.sum(lq1_ref[...] * lk1_ref[...], axis=-1, keepdims=True))
           - jnp.exp(jnp.sum(lq2_ref[...] * lk2_ref[...], axis=-1, keepdims=True))
           + lam_init)
    o = acc_ref[0] * (1.0 / l_ref[0]) - acc_ref[1] * (lam / l_ref[1])
    o = o * lax.rsqrt(jnp.mean(o * o, axis=0, keepdims=True) + 1e-5) * dfn_ref[...]
    y_ref[...] = (o * (1.0 - lam_init)).T.astype(y_ref.dtype)


def _diff_attn(proj3, v_t, rel_bias, lq1, lk1, lq2, lk2, df_norm_col, to_cast, *, n_heads, col0,
               tq, tk, lam_init):
    B, S, _ = proj3.shape
    d2 = 2 * DF_HEAD_DIM
    assert tq % tk == 0 and tk % ATTN_SUB == 0 and ATTN_SUB >= MAX_DISTANCE and S % tq == 0
    nq = S // tq
    n_chains = 2 * (tq // ATTN_SUB)
    cb = col0 // d2
    vec = lambda n: pl.BlockSpec((1, n), lambda b, h, i: (0, 0))
    n_steps = B * n_heads * nq

    def slab(w):
        rows = w.shape[0] // n_steps
        assert w.shape[0] % n_steps == 0 and rows % 16 == 0
        return pl.BlockSpec((rows, w.shape[1]), lambda b, h, i: ((b * n_heads + h) * nq + i, 0))

    return pl.pallas_call(
        functools.partial(_diff_attn_kernel, tq=tq, tk=tk, n_heads=n_heads, lam_init=lam_init),
        grid=(B, n_heads, nq),
        in_specs=[
            pl.BlockSpec(memory_space=pltpu.SMEM),
            pl.BlockSpec((None, tq, d2), lambda b, h, i: (b, i, cb + h)),
            pl.BlockSpec((None, S, d2), lambda b, h, i: (b, 0, cb + n_heads + h)),
            pl.BlockSpec((None, S // tk, d2, tk), lambda b, h, i: (b, 0, h, 0)),
            vec(DF_HEAD_DIM), vec(DF_HEAD_DIM), vec(DF_HEAD_DIM), vec(DF_HEAD_DIM),
            pl.BlockSpec((d2, 1), lambda b, h, i: (0, 0)),
        ] + [slab(w) for w in to_cast],
        out_specs=[pl.BlockSpec((None, tq, d2), lambda b, h, i: (b, i, h))]
        + [slab(w) for w in to_cast],
        out_shape=[jax.ShapeDtypeStruct((B, S, n_heads * d2), BF16)]
        + [jax.ShapeDtypeStruct(w.shape, BF16) for w in to_cast],
        scratch_shapes=[
            pltpu.VMEM((4, ATTN_SUB, ATTN_SUB), F32),
            pltpu.VMEM((n_chains, tk, ATTN_SUB), F32),
            pltpu.VMEM((tk, ATTN_SUB), BF16),
            pltpu.VMEM((2, 1, tq), F32),
            pltpu.VMEM((2, 1, tq), F32),
            pltpu.VMEM((2, d2, tq), F32),
        ],
        compiler_params=pltpu.CompilerParams(
            dimension_semantics=("arbitrary", "arbitrary", "arbitrary"),
            vmem_limit_bytes=V7X_VMEM_LIMIT),
        name="diff_attn",
    )(rel_bias, proj3, proj3, v_t, lq1, lk1, lq2, lk2, df_norm_col, *to_cast)


def _out_proj_kernel(x_ref, ya_ref, yb_ref, wa_ref, wb_ref, h_ref):
    h_ref[...] = x_ref[...] + _dot(ya_ref[...], wa_ref[...]) + _dot(yb_ref[...], wb_ref[...])


def _out_proj(x2, y_dn, y_df, w_o, *, tm, tn):
    T, D = x2.shape
    ka = y_dn.shape[1]
    kb = y_df.shape[1]
    assert ka == kb
    return pl.pallas_call(
        _out_proj_kernel,
        grid=(T // tm, D // tn),
        in_specs=[
            pl.BlockSpec((tm, tn), lambda i, j: (i, j)),
            pl.BlockSpec((tm, ka), lambda i, j: (i, 0)),
            pl.BlockSpec((tm, kb), lambda i, j: (i, 0)),
            pl.BlockSpec((ka, tn), lambda i, j: (0, j)),
            pl.BlockSpec((kb, tn), lambda i, j: (1, j)),
        ],
        out_specs=pl.BlockSpec((tm, tn), lambda i, j: (i, j)),
        out_shape=jax.ShapeDtypeStruct((T, D), F32),
        compiler_params=pltpu.CompilerParams(
            dimension_semantics=("arbitrary", "arbitrary"),
            vmem_limit_bytes=V7X_VMEM_LIMIT),
        name="out_proj",
    )(x2, y_dn, y_df, w_o, w_o)


def _mlp_kernel(h_ref, g_ref, wu_ref, wd_ref, gf_ref, o_ref, u_ref, acc_ref):
    f = pl.program_id(1)

    @pl.when(f == 0)
    def _():
        x = h_ref[...]
        ms = jnp.mean(x * x, axis=-1, keepdims=True)
        u_ref[...] = (x * lax.rsqrt(ms + 1e-6) * g_ref[...]).astype(BF16)
        acc_ref[...] = jnp.zeros_like(acc_ref)

    hid = jnp.maximum(_dot(u_ref[...], wu_ref[...]), 0.0)
    acc_ref[...] += _dot((hid * hid).astype(BF16), wd_ref[...])

    @pl.when(f == pl.num_programs(1) - 1)
    def _():
        y = h_ref[...] + acc_ref[...]
        ms = jnp.mean(y * y, axis=-1, keepdims=True)
        o_ref[...] = y * lax.rsqrt(ms + 1e-6) * gf_ref[...]


def _mlp(h1, gain, w_up, w_down, final_gain, *, tm, tf):
    T, D = h1.shape
    Fdim = w_up.shape[1]
    return pl.pallas_call(
        _mlp_kernel,
        grid=(T // tm, Fdim // tf),
        in_specs=[
            pl.BlockSpec((tm, D), lambda i, f: (i, 0)),
            pl.BlockSpec((1, D), lambda i, f: (0, 0)),
            pl.BlockSpec((D, tf), lambda i, f: (0, f)),
            pl.BlockSpec((tf, D), lambda i, f: (f, 0)),
            pl.BlockSpec((1, D), lambda i, f: (0, 0)),
        ],
        out_specs=pl.BlockSpec((tm, D), lambda i, f: (i, 0)),
        out_shape=jax.ShapeDtypeStruct((T, D), F32),
        scratch_shapes=[pltpu.VMEM((tm, D), BF16), pltpu.VMEM((tm, D), F32)],
        compiler_params=pltpu.CompilerParams(
            dimension_semantics=("arbitrary", "arbitrary"),
            vmem_limit_bytes=V7X_VMEM_LIMIT),
        name="mlp",
    )(h1, gain, w_up, w_down, final_gain)


def _tile(n, pref):
    if n <= pref:
        return n
    t = pref - pref % 128
    while t > 128 and n % t:
        t -= 128
    assert n % t == 0
    return t


def kernel(x, attn_norm, w_in, conv_w, a_log, dt_bias, dn_norm, lambda_q1, lambda_k1,
           lambda_q2, lambda_k2, df_norm, rel_bias, w_o, mlp_norm, w_up, w_down, final_norm):
    B, S, D = x.shape
    depth = attn_norm.shape[0]
    n_dn = a_log.shape[1]
    n_df = rel_bias.shape[1]
    dn_dim = n_dn * DN_HEAD_DIM
    df_dim = n_df * 2 * DF_HEAD_DIM
    T = B * S
    gate0 = 4 * dn_dim
    assert w_in.shape[2] == gate0 + 2 * n_dn + 3 * df_dim and 2 * n_dn <= GATE_COLS

    assert depth == 1
    l = 0
    h = x.reshape(T, D)

    wl = w_in[l]
    dfq0 = gate0 + 2 * n_dn
    dfv0 = dfq0 + 2 * df_dim
    w_all = wl.astype(BF16)
    w_b = w_all[:, dfq0:dfv0]
    w_vt = w_all[:, dfv0:].T

    proj, gates, v_t = _in_proj(h, attn_norm[l][None, :], w_all, w_b, w_vt, conv_w[l],
                                batch=B, dn_dim=dn_dim, df_dim=df_dim, tm=_tile(S, IN_PROJ_ROWS),
                                tn=_tile(math.gcd(dn_dim, df_dim), IN_PROJ_COLS),
                                tv=_tile(S, ATTN_BLOCK),
                                q_scale=DF_HEAD_DIM ** -0.5 * math.log2(math.e))
    proj3 = proj.reshape(B, S, -1)
    gate3 = gates.reshape(B, S, GATE_COLS)

    gate_pad = ((0, 0), (n_dn, GATE_COLS - 2 * n_dn))
    alog_row = jnp.pad(a_log[l][None, :], gate_pad)
    dtb_row = jnp.pad(dt_bias[l][None, :], gate_pad)
    y_dn = _gdn(proj3, gate3, alog_row, dtb_row, dn_norm[l][None, :],
                n_heads=n_dn, blk=_tile(S, GDN_BLOCK))
    lam_init = 0.8 - 0.6 * math.exp(-0.3 * l)
    y_df, w_o_bf, w_up_bf, w_down_bf = _diff_attn(
                      proj3, v_t, rel_bias.reshape(-1),
                      lambda_q1[l][None, :], lambda_k1[l][None, :],
                      lambda_q2[l][None, :], lambda_k2[l][None, :], df_norm[l][:, None],
                      (w_o[l], w_up[l], w_down[l]),
                      n_heads=n_df, col0=gate0, tq=_tile(S, ATTN_QBLOCK), tk=_tile(S, ATTN_BLOCK),
                      lam_init=lam_init)

    h1 = _out_proj(h, y_dn.reshape(T, dn_dim), y_df.reshape(T, df_dim), w_o_bf,
                   tm=_tile(T, 512), tn=_tile(D, 2048))
    out = _mlp(h1, mlp_norm[l][None, :], w_up_bf, w_down_bf, final_norm[None, :],
               tm=_tile(T, 512), tf=_tile(w_up_bf.shape[1], 1024))
    return out.reshape(B, S, D)
```

```python
import functools
import math

import numpy as np
import jax
import jax.numpy as jnp
from jax import lax
from jax.experimental import pallas as pl
from jax.experimental.pallas import tpu as pltpu

F32 = jnp.float32
BF16 = jnp.bfloat16

DN_HEAD_DIM = 128
DF_HEAD_DIM = 128
CONV_WIDTH = 4
CHUNK = 64
NUM_BUCKETS = 32
MAX_DISTANCE = 128
GATE_COLS = 128
IN_PROJ_ROWS = 1024
IN_PROJ_COLS = 1024
IN_PROJ_SUB = 256
GDN_BLOCK = 512
GDN_GROUP = 4
ATTN_SUB = 256
ATTN_BLOCK = 512
ATTN_QBLOCK = 1024

V7X_VMEM_LIMIT = 58 * 1024 * 1024


def _dot(a, b):
    return jnp.dot(a, b, preferred_element_type=F32)


def _dot_nt(a, b):
    return lax.dot_general(a, b, (((1,), (1,)), ((), ())), preferred_element_type=F32)


def _sigmoid(x):
    return 1.0 / (1.0 + jnp.exp(-x))


def _in_proj_kernel(x_ref, g_ref, wa_ref, wb_ref, wg_ref, cw_ref, proj_ref, gate_ref, vt_ref,
                    u_ref, hist_ref, cbuf_ref, *, bounds, per_seq, q_scale):
    i = pl.program_id(0)
    j = pl.program_id(1)
    tm, tn = proj_ref.shape
    b0, b1, b2, b3, b4, b5 = bounds

    @pl.when(j == 0)
    def _():
        x = x_ref[...]
        ms = jnp.mean(x * x, axis=-1, keepdims=True)
        u = (x * lax.rsqrt(ms + 1e-6) * g_ref[...]).astype(BF16)
        u_ref[...] = u
        gate_ref[...] = _dot(u, wg_ref[...])

    n_col_chunks = tn // IN_PROJ_SUB
    first_of_seq = (i % per_seq) == 0

    def chunk_cols(c):
        return slice(c * IN_PROJ_SUB, (c + 1) * IN_PROJ_SUB)

    def raw_cols(w_ref, c):
        cbuf_ref[8:8 + tm, chunk_cols(c)] = _dot(u_ref[...], w_ref[:, chunk_cols(c)])

    def raw(c):
        return cbuf_ref[8:8 + tm, chunk_cols(c)]

    def conv_silu(c):
        cw = cw_ref[:, chunk_cols(c)]
        y = None
        for s in range(CONV_WIDTH):
            tap = CONV_WIDTH - 1 - s
            term = cbuf_ref[8 - s:8 - s + tm, chunk_cols(c)] * cw[tap:tap + 1, :]
            y = term if y is None else y + term
        return y * _sigmoid(y)

    def l2norm_heads(y, scale):
        outs = []
        for h in range(IN_PROJ_SUB // DN_HEAD_DIM):
            yh = y[:, h * DN_HEAD_DIM:(h + 1) * DN_HEAD_DIM]
            outs.append(yh * (lax.rsqrt(jnp.sum(yh * yh, axis=-1, keepdims=True) + 1e-6) * scale))
        return jnp.concatenate(outs, axis=1)

    def silu_cols(c):
        z = raw(c)
        return z * _sigmoid(z)

    def project(w_ref, epilogue, conv=False):
        if conv:
            cbuf_ref[0:8, :] = jnp.where(first_of_seq, 0.0, hist_ref[j])
        raw_cols(w_ref, 0)
        for c in range(n_col_chunks):
            if c + 1 < n_col_chunks:
                raw_cols(w_ref, c + 1)
            proj_ref[:, chunk_cols(c)] = epilogue(c).astype(BF16)
        if conv:
            hist_ref[j] = cbuf_ref[tm:tm + 8, :]

    @pl.when(j < b0)
    def _():
        project(wa_ref, lambda c: l2norm_heads(conv_silu(c), DN_HEAD_DIM ** -0.5), conv=True)

    @pl.when((j >= b0) & (j < b1))
    def _():
        project(wa_ref, lambda c: l2norm_heads(conv_silu(c), 1.0), conv=True)

    @pl.when((j >= b1) & (j < b2))
    def _():
        project(wa_ref, conv_silu, conv=True)

    @pl.when((j >= b2) & (j < b3))
    def _():
        project(wa_ref, silu_cols)

    @pl.when((j >= b3) & (j < b4))
    def _():
        project(wb_ref, lambda c: raw(c) * q_scale)

    @pl.when((j >= b4) & (j < b5))
    def _():
        project(wb_ref, raw)

    @pl.when(j >= b5)
    def _():
        tv = vt_ref.shape[-1]
        raw_cols(wb_ref, 0)
        for c in range(n_col_chunks):
            if c + 1 < n_col_chunks:
                raw_cols(wb_ref, c + 1)
            vt = raw(c).T.astype(BF16)
            for tb in range(vt_ref.shape[0]):
                vt_ref[tb, chunk_cols(c), :] = vt[:, tb * tv:(tb + 1) * tv]


def _in_proj(x2, gain, w_all, w_b, conv_w, *, batch, dn_dim, df_dim, tm, tn, tv, q_scale):
    T, D = x2.shape
    N = 4 * dn_dim + 2 * df_dim
    nv = df_dim
    S = T // batch
    assert w_b.shape[1] == 3 * df_dim
    assert (4 * dn_dim) % GATE_COLS == 0 and w_all.shape[1] >= 4 * dn_dim + GATE_COLS
    assert S % tm == 0 and dn_dim % tn == 0 and df_dim % tn == 0 and tm % tv == 0
    assert tn % DN_HEAD_DIM == 0 and conv_w.shape == (CONV_WIDTH, 3 * dn_dim)
    assert tn % IN_PROJ_SUB == 0 and IN_PROJ_SUB % DN_HEAD_DIM == 0
    n_dn, n_df = dn_dim // tn, df_dim // tn
    bounds = (n_dn, 2 * n_dn, 3 * n_dn, 4 * n_dn, 4 * n_dn + n_df, 4 * n_dn + 2 * n_df)
    n_main = bounds[-1]
    n_conv = bounds[2]
    n_a = bounds[3]
    n_b = n_main - n_a + nv // tn
    per_seq = S // tm
    return pl.pallas_call(
        functools.partial(_in_proj_kernel, bounds=bounds, per_seq=per_seq, q_scale=q_scale),
        grid=(T // tm, n_main + nv // tn),
        in_specs=[
            pl.BlockSpec((tm, D), lambda i, j: (i, 0)),
            pl.BlockSpec((1, D), lambda i, j: (0, 0)),
            pl.BlockSpec((D, tn), lambda i, j: (0, jnp.where(j < n_a, j, 0))),
            pl.BlockSpec((D, tn), lambda i, j: (0, jnp.where(j < n_a, n_b - 1,
                                                             jnp.minimum(j - n_a, n_b - 1)))),
            pl.BlockSpec((D, GATE_COLS), lambda i, j: (0, 4 * dn_dim // GATE_COLS),
                         pipeline_mode=pl.Buffered(1)),
            pl.BlockSpec((CONV_WIDTH, tn), lambda i, j: (0, jnp.minimum(j, n_conv - 1))),
        ],
        out_specs=[
            pl.BlockSpec((tm, tn), lambda i, j: (i, jnp.minimum(j, n_main - 1))),
            pl.BlockSpec((tm, GATE_COLS), lambda i, j: (i, 0)),
            pl.BlockSpec((None, tm // tv, tn, tv),
                         lambda i, j: (i // per_seq, i % per_seq, jnp.maximum(j - n_main, 0), 0)),
        ],
        out_shape=[
            jax.ShapeDtypeStruct((T, N), BF16),
            jax.ShapeDtypeStruct((T, GATE_COLS), F32),
            jax.ShapeDtypeStruct((batch, S // tv, nv, tv), BF16),
        ],
        scratch_shapes=[
            pltpu.VMEM((tm, D), BF16),
            pltpu.VMEM((n_conv, 8, tn), F32),
            pltpu.VMEM((8 + tm, tn), F32),
        ],
        compiler_params=pltpu.CompilerParams(
            dimension_semantics=("arbitrary", "arbitrary"),
            vmem_limit_bytes=V7X_VMEM_LIMIT),
        name="in_proj",
    )(x2, gain, w_all, w_b, w_all, conv_w)


def _gdn_kernel(q_ref, k_ref, v_ref, z_ref, gate_ref, alog_ref, dtb_ref, dnn_ref,
                y_ref, state_ref, *, blk, n_heads):
    dk = DN_HEAD_DIM
    n_chunks = blk // CHUNK

    @pl.when(pl.program_id(1) == 0)
    def _():
        state_ref[...] = jnp.zeros_like(state_ref)

    gate = gate_ref[...]
    beta_all = _sigmoid(gate)
    xs = gate + dtb_ref[...]
    softplus = jnp.maximum(xs, 0.0) + jnp.log(1.0 + jnp.exp(-jnp.abs(xs)))
    g_all = -jnp.exp(alog_ref[...]) * softplus
    pos = lax.broadcasted_iota(jnp.int32, g_all.shape, 0) & (CHUNK - 1)
    gc_all = g_all
    step = 1
    while step < CHUNK:
        gc_all = gc_all + jnp.where(pos >= step, pltpu.roll(gc_all, step, 0), 0.0)
        step *= 2

    ri = lax.broadcasted_iota(jnp.int32, (CHUNK, CHUNK), 0)
    ci = lax.broadcasted_iota(jnp.int32, (CHUNK, CHUNK), 1)
    tril = ri >= ci
    strict = ri > ci
    eye = ri == ci
    gain = dnn_ref[...]

    def first_stage(c):
        rows = slice(c * CHUNK, (c + 1) * CHUNK)
        items = []
        for h in range(n_heads):
            cols = slice(h * dk, (h + 1) * dk)
            q_bf = q_ref[rows, cols]
            k_bf = k_ref[rows, cols]
            qc = q_bf.astype(F32)
            kc = k_bf.astype(F32)
            vc = v_ref[rows, cols].astype(F32)
            bc = jnp.broadcast_to(beta_all[rows, h:h + 1], (CHUNK, dk))
            gcc = jnp.broadcast_to(gc_all[rows, n_heads + h:n_heads + h + 1], (CHUNK, dk))
            g_last = gcc[CHUNK - 1:CHUNK, :]
            eg = jnp.exp(gcc)
            g_sq = gcc[:, 0:CHUNK]
            g_row = jnp.sum(jnp.where(eye, g_sq, 0.0), axis=0, keepdims=True)
            decay = jnp.exp(jnp.where(tril, g_sq - g_row, -jnp.inf))
            kb = kc * bc
            lhs = jnp.concatenate([kb.astype(BF16), q_bf], axis=0)
            aq = _dot_nt(lhs, k_bf)
            n_mat = jnp.where(strict, -(aq[:CHUNK] * decay), 0.0)
            items.append(dict(
                h=h, qe=qc * eg, g_last=g_last,
                rhs=jnp.concatenate([kb * eg, vc * bc], axis=1).astype(BF16),
                qk=jnp.where(tril, aq[CHUNK:] * decay, 0.0).astype(BF16),
                kd_t=(kc * jnp.exp(g_last - gcc)).T.astype(BF16),
                power=n_mat, inv=jnp.where(eye, 1.0, 0.0) + n_mat))
        return items

    def matrix_stages(items):
        span = 2
        while span < CHUNK:
            for s in items:
                pb = s["power"].astype(BF16)
                s["power"] = _dot(pb, pb)
            for s in items:
                s["inv"] = s["inv"] + _dot(s["inv"].astype(BF16), s["power"].astype(BF16))
            span *= 2
        for s in items:
            s["wu"] = _dot(s["inv"].astype(BF16), s["rhs"]).astype(BF16)
        for s in items:
            s["gr"] = _dot(s["kd_t"], s["wu"])
            qw = _dot(s["qk"], s["wu"])
            s["q_eff"] = (s["qe"] - qw[:, :dk]).astype(BF16)
            s["p_loc"] = qw[:, dk:]

    def state_stage(c, items):
        rows = slice(c * CHUNK, (c + 1) * CHUNK)
        for s in items:
            h = s["h"]
            state = state_ref[h]
            s_bf = state.astype(BF16)
            o = _dot(s["q_eff"], s_bf) + s["p_loc"]
            state_ref[h] = (state * jnp.exp(s["g_last"])
                            - _dot(s["gr"][:, :dk].astype(BF16), s_bf) + s["gr"][:, dk:])
            zs = z_ref[rows, h * dk:(h + 1) * dk].astype(F32)
            o = o * lax.rsqrt(jnp.mean(o * o, axis=-1, keepdims=True) + 1e-6) * gain
            y_ref[rows, h * dk:(h + 1) * dk] = (o * zs).astype(y_ref.dtype)

    groups = [list(range(g, min(g + GDN_GROUP, n_chunks))) for g in range(0, n_chunks, GDN_GROUP)]
    cur = [first_stage(c) for c in groups[0]]
    for gi, chunk_ids in enumerate(groups):
        nxt = [first_stage(c) for c in groups[gi + 1]] if gi + 1 < len(groups) else None
        matrix_stages([s for items in cur for s in items])
        for c, items in zip(chunk_ids, cur):
            state_stage(c, items)
        cur = nxt


def _gdn(proj3, gate3, alog_row, dtb_row, dn_norm, *, n_heads, blk):
    B, S, _ = proj3.shape
    dk = DN_HEAD_DIM
    dn = n_heads * dk

    def group(idx):
        return pl.BlockSpec((None, blk, dn), lambda b, t: (b, t, idx))

    def whole(arr):
        return pl.BlockSpec(arr.shape, lambda b, t: (0,) * arr.ndim)

    return pl.pallas_call(
        functools.partial(_gdn_kernel, blk=blk, n_heads=n_heads),
        grid=(B, S // blk),
        in_specs=[
            group(0), group(1), group(2), group(3),
            pl.BlockSpec((None, blk, GATE_COLS), lambda b, t: (b, t, 0)),
            whole(alog_row), whole(dtb_row), whole(dn_norm),
        ],
        out_specs=pl.BlockSpec((None, blk, dn), lambda b, t: (b, t, 0)),
        out_shape=jax.ShapeDtypeStruct((B, S, dn), BF16),
        scratch_shapes=[pltpu.VMEM((n_heads, dk, dk), F32)],
        compiler_params=pltpu.CompilerParams(
            dimension_semantics=("arbitrary", "arbitrary"),
            vmem_limit_bytes=V7X_VMEM_LIMIT),
        name="gdn",
    )(proj3, proj3, proj3, proj3, gate3, alog_row, dtb_row, dn_norm)


def _t5_bucket_starts():
    max_exact = NUM_BUCKETS // 2
    n = np.arange(0, MAX_DISTANCE + 1)
    nf = np.maximum(n, 1).astype(np.float32)
    large = max_exact + (np.log(nf / max_exact) / math.log(MAX_DISTANCE / max_exact)
                         * (NUM_BUCKETS - max_exact)).astype(np.int32)
    bucket = np.where(n < max_exact, n, np.minimum(large, NUM_BUCKETS - 1))
    assert bucket[MAX_DISTANCE] == NUM_BUCKETS - 1 and np.all(np.diff(bucket) >= 0)
    starts = [(0, int(bucket[0]))]
    for d in range(1, MAX_DISTANCE + 1):
        if bucket[d] != bucket[d - 1]:
            starts.append((d, int(bucket[d])))
    return starts


def _diff_attn_kernel(rb_ref, q_ref, k_ref, vt_ref, lq1_ref, lk1_ref, lq2_ref, lk2_ref, dfn_ref,
                      c0_ref, c1_ref, c2_ref, y_ref, o0_ref, o1_ref, o2_ref,
                      bias_ref, s_ref, p_ref, m_ref, l_ref, acc_ref,
                      *, tq, tk, n_heads, lam_init):
    h = pl.program_id(1)
    qi = pl.program_id(2)
    d = DF_HEAD_DIM

    for c_ref, o_ref in ((c0_ref, o0_ref), (c1_ref, o1_ref), (c2_ref, o2_ref)):
        o_ref[...] = c_ref[...].astype(o_ref.dtype)

    sub = ATTN_SUB
    nsq = tq // sub
    nsk = tk // sub
    log2e = math.log2(math.e)

    @pl.when(qi == 0)
    def _():
        keys = lax.broadcasted_iota(jnp.int32, (sub, sub), 0)
        qrys = lax.broadcasted_iota(jnp.int32, (sub, sub), 1)
        starts = _t5_bucket_starts()
        for idx in range(2):
            dist = qrys - keys + idx * sub
            tile = jnp.full((sub, sub), rb_ref[starts[0][1] * n_heads + h], F32)
            for first, bucket in starts[1:]:
                tile = jnp.where(dist >= first, rb_ref[bucket * n_heads + h], tile)
            tile = tile * log2e
            if idx == 0:
                tile = jnp.where(dist >= 0, tile, -jnp.inf)
            bias_ref[idx] = tile
        bias_ref[2] = jnp.full((sub, sub), rb_ref[(NUM_BUCKETS - 1) * n_heads + h] * log2e, F32)
        bias_ref[3] = jnp.full((sub, sub), -jnp.inf, F32)

    m_ref[...] = jnp.full(m_ref.shape, -jnp.inf, F32)
    l_ref[...] = jnp.zeros_like(l_ref)
    acc_ref[...] = jnp.zeros_like(acc_ref)

    chains = [(qb, m) for qb in range(nsq) for m in range(2)]

    far_shift = rb_ref[(NUM_BUCKETS - 1) * n_heads + h] * log2e

    def produce(j, c):
        qb, m = chains[c]
        k0 = pl.multiple_of(j * tk, tk)
        s_ref[c] = _dot_nt(k_ref[pl.ds(k0, tk), m * d:(m + 1) * d],
                           q_ref[qb * sub:(qb + 1) * sub, m * d:(m + 1) * d])

    def softmax(j, c, far):
        qb, m = chains[c]
        if far:
            s = s_ref[c]
            shift = far_shift
        else:
            parts = []
            for kb in range(nsk):
                off = (qi * nsq + qb) - (j * nsk + kb)
                idx = jnp.where(off < 0, 3, jnp.minimum(off, 2))
                parts.append(s_ref[c, kb * sub:(kb + 1) * sub, :] + bias_ref[idx])
            s = jnp.concatenate(parts, axis=0)
            shift = 0.0
        cols = slice(qb * sub, (qb + 1) * sub)
        m_prev = m_ref[m, :, cols]
        m_new = jnp.maximum(m_prev, jnp.max(s, axis=0, keepdims=True) + shift)
        alpha = jnp.exp2(m_prev - m_new)
        pr = jnp.exp2(s - (m_new - shift))
        l_ref[m, :, cols] = alpha * l_ref[m, :, cols] + jnp.sum(pr, axis=0, keepdims=True)
        acc_ref[m, :, cols] = alpha * acc_ref[m, :, cols]
        m_ref[m, :, cols] = m_new
        return pr.astype(BF16)

    def add_values(j, c, p):
        qb, m = chains[c]
        cols = slice(qb * sub, (qb + 1) * sub)
        acc_ref[m, :, cols] += _dot(vt_ref[j], p)

    n_chains = len(chains)
    for c in range(n_chains):
        produce(0, c)
    p_ref[...] = softmax(0, 0, False)

    def trip(j, far):
        add_values(j, 0, p_ref[...])
        produce(j + 1, 0)
        for c in range(1, n_chains):
            add_values(j, c, softmax(j, c, far))
            produce(j + 1, c)
        p_ref[...] = softmax(j + 1, 0, far)

    last = (qi + 1) * (tq // tk) - 1
    n_far = jnp.maximum((qi * nsq - 1) // nsk - 1, 0)
    lax.fori_loop(0, n_far, lambda j, carry: (trip(j, True), carry)[1], 0)
    lax.fori_loop(n_far, last, lambda j, carry: (trip(j, False), carry)[1], 0)
    live = [c for c in range(n_chains) if chains[c][0] >= nsq - nsk]
    for c in live:
        add_values(last, c, p_ref[...] if c == 0 else softmax(last, c, False))

    lam = (jnp.exp(jnp.sum(lq1_ref[...] * lk1_ref[...], axis=-1, keepdims=True))
           - jnp.exp(jnp.sum(lq2_ref[...] * lk2_ref[...], axis=-1, keepdims=True))
           + lam_init)
    o = acc_ref[0] * (1.0 / l_ref[0]) - acc_ref[1] * (lam / l_ref[1])
    o = o * lax.rsqrt(jnp.mean(o * o, axis=0, keepdims=True) + 1e-5) * dfn_ref[...]
    y_ref[...] = (o * (1.0 - lam_init)).T.astype(y_ref.dtype)


def _diff_attn(proj3, v_t, rel_bias, lq1, lk1, lq2, lk2, df_norm_col, to_cast, *, n_heads, col0,
               tq, tk, lam_init):
    B, S, _ = proj3.shape
    d2 = 2 * DF_HEAD_DIM
    assert tq % tk == 0 and tk % ATTN_SUB == 0 and ATTN_SUB >= MAX_DISTANCE and S % tq == 0
    nq = S // tq
    n_chains = 2 * (tq // ATTN_SUB)
    cb = col0 // d2
    vec = lambda n: pl.BlockSpec((1, n), lambda b, h, i: (0, 0))
    n_steps = B * n_heads * nq

    def slab(w):
        rows = w.shape[0] // n_steps
        assert w.shape[0] % n_steps == 0 and rows % 16 == 0
        return pl.BlockSpec((rows, w.shape[1]), lambda b, h, i: ((b * n_heads + h) * nq + i, 0))

    return pl.pallas_call(
        functools.partial(_diff_attn_kernel, tq=tq, tk=tk, n_heads=n_heads, lam_init=lam_init),
        grid=(B, n_heads, nq),
        in_specs=[
            pl.BlockSpec(memory_space=pltpu.SMEM),
            pl.BlockSpec((None, tq, d2), lambda b, h, i: (b, i, cb + h)),
            pl.BlockSpec((None, S, d2), lambda b, h, i: (b, 0, cb + n_heads + h)),
            pl.BlockSpec((None, S // tk, d2, tk), lambda b, h, i: (b, 0, h, 0)),
            vec(DF_HEAD_DIM), vec(DF_HEAD_DIM), vec(DF_HEAD_DIM), vec(DF_HEAD_DIM),
            pl.BlockSpec((d2, 1), lambda b, h, i: (0, 0)),
        ] + [slab(w) for w in to_cast],
        out_specs=[pl.BlockSpec((None, tq, d2), lambda b, h, i: (b, i, h))]
        + [slab(w) for w in to_cast],
        out_shape=[jax.ShapeDtypeStruct((B, S, n_heads * d2), BF16)]
        + [jax.ShapeDtypeStruct(w.shape, BF16) for w in to_cast],
        scratch_shapes=[
            pltpu.VMEM((4, ATTN_SUB, ATTN_SUB), F32),
            pltpu.VMEM((n_chains, tk, ATTN_SUB), F32),
            pltpu.VMEM((tk, ATTN_SUB), BF16),
            pltpu.VMEM((2, 1, tq), F32),
            pltpu.VMEM((2, 1, tq), F32),
            pltpu.VMEM((2, d2, tq), F32),
        ],
        compiler_params=pltpu.CompilerParams(
            dimension_semantics=("arbitrary", "arbitrary", "arbitrary"),
            vmem_limit_bytes=V7X_VMEM_LIMIT),
        name="diff_attn",
    )(rel_bias, proj3, proj3, v_t, lq1, lk1, lq2, lk2, df_norm_col, *to_cast)


def _out_proj_kernel(x_ref, ya_ref, yb_ref, wa_ref, wb_ref, h_ref):
    h_ref[...] = x_ref[...] + _dot(ya_ref[...], wa_ref[...]) + _dot(yb_ref[...], wb_ref[...])


def _out_proj(x2, y_dn, y_df, w_o, *, tm, tn):
    T, D = x2.shape
    ka = y_dn.shape[1]
    kb = y_df.shape[1]
    assert ka == kb
    return pl.pallas_call(
        _out_proj_kernel,
        grid=(T // tm, D // tn),
        in_specs=[
            pl.BlockSpec((tm, tn), lambda i, j: (i, j)),
            pl.BlockSpec((tm, ka), lambda i, j: (i, 0)),
            pl.BlockSpec((tm, kb), lambda i, j: (i, 0)),
            pl.BlockSpec((ka, tn), lambda i, j: (0, j)),
            pl.BlockSpec((kb, tn), lambda i, j: (1, j)),
        ],
        out_specs=pl.BlockSpec((tm, tn), lambda i, j: (i, j)),
        out_shape=jax.ShapeDtypeStruct((T, D), F32),
        compiler_params=pltpu.CompilerParams(
            dimension_semantics=("arbitrary", "arbitrary"),
            vmem_limit_bytes=V7X_VMEM_LIMIT),
        name="out_proj",
    )(x2, y_dn, y_df, w_o, w_o)


def _mlp_kernel(h_ref, g_ref, wu_ref, wd_ref, gf_ref, o_ref, u_ref, acc_ref):
    f = pl.program_id(1)

    @pl.when(f == 0)
    def _():
        x = h_ref[...]
        ms = jnp.mean(x * x, axis=-1, keepdims=True)
        u_ref[...] = (x * lax.rsqrt(ms + 1e-6) * g_ref[...]).astype(BF16)
        acc_ref[...] = jnp.zeros_like(acc_ref)

    hid = jnp.maximum(_dot(u_ref[...], wu_ref[...]), 0.0)
    acc_ref[...] += _dot((hid * hid).astype(BF16), wd_ref[...])

    @pl.when(f == pl.num_programs(1) - 1)
    def _():
        y = h_ref[...] + acc_ref[...]
        ms = jnp.mean(y * y, axis=-1, keepdims=True)
        o_ref[...] = y * lax.rsqrt(ms + 1e-6) * gf_ref[...]


def _mlp(h1, gain, w_up, w_down, final_gain, *, tm, tf):
    T, D = h1.shape
    Fdim = w_up.shape[1]
    return pl.pallas_call(
        _mlp_kernel,
        grid=(T // tm, Fdim // tf),
        in_specs=[
            pl.BlockSpec((tm, D), lambda i, f: (i, 0)),
            pl.BlockSpec((1, D), lambda i, f: (0, 0)),
            pl.BlockSpec((D, tf), lambda i, f: (0, f)),
            pl.BlockSpec((tf, D), lambda i, f: (f, 0)),
            pl.BlockSpec((1, D), lambda i, f: (0, 0)),
        ],
        out_specs=pl.BlockSpec((tm, D), lambda i, f: (i, 0)),
        out_shape=jax.ShapeDtypeStruct((T, D), F32),
        scratch_shapes=[pltpu.VMEM((tm, D), BF16), pltpu.VMEM((tm, D), F32)],
        compiler_params=pltpu.CompilerParams(
            dimension_semantics=("arbitrary", "arbitrary"),
            vmem_limit_bytes=V7X_VMEM_LIMIT),
        name="mlp",
    )(h1, gain, w_up, w_down, final_gain)


def _tile(n, pref):
    if n <= pref:
        return n
    t = pref - pref % 128
    while t > 128 and n % t:
        t -= 128
    assert n % t == 0
    return t


def kernel(x, attn_norm, w_in, conv_w, a_log, dt_bias, dn_norm, lambda_q1, lambda_k1,
           lambda_q2, lambda_k2, df_norm, rel_bias, w_o, mlp_norm, w_up, w_down, final_norm):
    B, S, D = x.shape
    depth = attn_norm.shape[0]
    n_dn = a_log.shape[1]
    n_df = rel_bias.shape[1]
    dn_dim = n_dn * DN_HEAD_DIM
    df_dim = n_df * 2 * DF_HEAD_DIM
    T = B * S
    gate0 = 4 * dn_dim
    assert w_in.shape[2] == gate0 + 2 * n_dn + 3 * df_dim and 2 * n_dn <= GATE_COLS

    assert depth == 1
    l = 0
    h = x.reshape(T, D)

    wl = w_in[l]
    dfq0 = gate0 + 2 * n_dn
    w_all = wl.astype(BF16)
    w_b = w_all[:, dfq0:]

    proj, gates, v_t = _in_proj(h, attn_norm[l][None, :], w_all, w_b, conv_w[l],
                                batch=B, dn_dim=dn_dim, df_dim=df_dim, tm=_tile(S, IN_PROJ_ROWS),
                                tn=_tile(math.gcd(dn_dim, df_dim), IN_PROJ_COLS),
                                tv=_tile(S, ATTN_BLOCK),
                                q_scale=DF_HEAD_DIM ** -0.5 * math.log2(math.e))
    proj3 = proj.reshape(B, S, -1)
    gate3 = gates.reshape(B, S, GATE_COLS)

    gate_pad = ((0, 0), (n_dn, GATE_COLS - 2 * n_dn))
    alog_row = jnp.pad(a_log[l][None, :], gate_pad)
    dtb_row = jnp.pad(dt_bias[l][None, :], gate_pad)
    y_dn = _gdn(proj3, gate3, alog_row, dtb_row, dn_norm[l][None, :],
                n_heads=n_dn, blk=_tile(S, GDN_BLOCK))
    lam_init = 0.8 - 0.6 * math.exp(-0.3 * l)
    y_df, w_o_bf, w_up_bf, w_down_bf = _diff_attn(
                      proj3, v_t, rel_bias.reshape(-1),
                      lambda_q1[l][None, :], lambda_k1[l][None, :],
                      lambda_q2[l][None, :], lambda_k2[l][None, :], df_norm[l][:, None],
                      (w_o[l], w_up[l], w_down[l]),
                      n_heads=n_df, col0=gate0, tq=_tile(S, ATTN_QBLOCK), tk=_tile(S, ATTN_BLOCK),
                      lam_init=lam_init)

    h1 = _out_proj(h, y_dn.reshape(T, dn_dim), y_df.reshape(T, df_dim), w_o_bf,
                   tm=_tile(T, 512), tn=_tile(D, 2048))
    out = _mlp(h1, mlp_norm[l][None, :], w_up_bf, w_down_bf, final_norm[None, :],
               tm=_tile(T, 512), tf=_tile(w_up_bf.shape[1], 1024))
    return out.reshape(B, S, D)
```

```python
import functools
import math

import numpy as np
import jax
import jax.numpy as jnp
from jax import lax
from jax.experimental import pallas as pl
from jax.experimental.pallas import tpu as pltpu

F32 = jnp.float32
BF16 = jnp.bfloat16

DN_HEAD_DIM = 128
DF_HEAD_DIM = 128
CONV_WIDTH = 4
CHUNK = 64
NUM_BUCKETS = 32
MAX_DISTANCE = 128
GATE_COLS = 128
IN_PROJ_ROWS = 1024
IN_PROJ_COLS = 1024
IN_PROJ_SUB = 256
GDN_BLOCK = 512
GDN_GROUP = 4
ATTN_SUB = 256
ATTN_BLOCK = 512
ATTN_QBLOCK = 1024

V7X_VMEM_LIMIT = 58 * 1024 * 1024


def _dot(a, b):
    return jnp.dot(a, b, preferred_element_type=F32)


def _dot_nt(a, b):
    return lax.dot_general(a, b, (((1,), (1,)), ((), ())), preferred_element_type=F32)


def _sigmoid(x):
    return 1.0 / (1.0 + jnp.exp(-x))


def _in_proj_kernel(x_ref, g_ref, wa_ref, wb_ref, wg_ref, cw_ref, proj_ref, gate_ref, vt_ref,
                    u_ref, hist_ref, cbuf_ref, *, bounds, per_seq, q_scale):
    i = pl.program_id(0)
    j = pl.program_id(1)
    tm, tn = proj_ref.shape
    b0, b1, b2, b3, b4, b5 = bounds

    @pl.when(j == 0)
    def _():
        x = x_ref[...]
        ms = jnp.mean(x * x, axis=-1, keepdims=True)
        u = (x * lax.rsqrt(ms + 1e-6) * g_ref[...]).astype(BF16)
        u_ref[...] = u
        gate_ref[...] = _dot(u, wg_ref[...])

    n_col_chunks = tn // IN_PROJ_SUB
    first_of_seq = (i % per_seq) == 0

    def chunk_cols(c):
        return slice(c * IN_PROJ_SUB, (c + 1) * IN_PROJ_SUB)

    def raw_cols(w_ref, c):
        cbuf_ref[8:8 + tm, chunk_cols(c)] = _dot(u_ref[...], w_ref[:, chunk_cols(c)])

    def raw(c):
        return cbuf_ref[8:8 + tm, chunk_cols(c)]

    def conv_silu(c):
        cw = cw_ref[:, chunk_cols(c)]
        y = None
        for s in range(CONV_WIDTH):
            tap = CONV_WIDTH - 1 - s
            term = cbuf_ref[8 - s:8 - s + tm, chunk_cols(c)] * cw[tap:tap + 1, :]
            y = term if y is None else y + term
        return y * _sigmoid(y)

    def l2norm_heads(y, scale):
        outs = []
        for h in range(IN_PROJ_SUB // DN_HEAD_DIM):
            yh = y[:, h * DN_HEAD_DIM:(h + 1) * DN_HEAD_DIM]
            outs.append(yh * (lax.rsqrt(jnp.sum(yh * yh, axis=-1, keepdims=True) + 1e-6) * scale))
        return jnp.concatenate(outs, axis=1)

    def silu_cols(c):
        z = raw(c)
        return z * _sigmoid(z)

    def project(w_ref, epilogue, conv=False):
        if conv:
            cbuf_ref[0:8, :] = jnp.where(first_of_seq, 0.0, hist_ref[j])
        raw_cols(w_ref, 0)
        for c in range(n_col_chunks):
            if c + 1 < n_col_chunks:
                raw_cols(w_ref, c + 1)
            proj_ref[:, chunk_cols(c)] = epilogue(c).astype(BF16)
        if conv:
            hist_ref[j] = cbuf_ref[tm:tm + 8, :]

    @pl.when(j < b0)
    def _():
        project(wa_ref, lambda c: l2norm_heads(conv_silu(c), DN_HEAD_DIM ** -0.5), conv=True)

    @pl.when((j >= b0) & (j < b1))
    def _():
        project(wa_ref, lambda c: l2norm_heads(conv_silu(c), 1.0), conv=True)

    @pl.when((j >= b1) & (j < b2))
    def _():
        project(wa_ref, conv_silu, conv=True)

    @pl.when((j >= b2) & (j < b3))
    def _():
        project(wa_ref, silu_cols)

    @pl.when((j >= b3) & (j < b4))
    def _():
        project(wb_ref, lambda c: raw(c) * q_scale)

    @pl.when((j >= b4) & (j < b5))
    def _():
        project(wb_ref, raw)

    @pl.when(j >= b5)
    def _():
        tv = vt_ref.shape[-1]
        raw_cols(wb_ref, 0)
        for c in range(n_col_chunks):
            if c + 1 < n_col_chunks:
                raw_cols(wb_ref, c + 1)
            vt = raw(c).T.astype(BF16)
            for tb in range(vt_ref.shape[0]):
                vt_ref[tb, chunk_cols(c), :] = vt[:, tb * tv:(tb + 1) * tv]


def _in_proj(x2, gain, w_all, w_b, conv_w, *, batch, dn_dim, df_dim, tm, tn, tv, q_scale):
    T, D = x2.shape
    N = 4 * dn_dim + 2 * df_dim
    nv = df_dim
    S = T // batch
    assert w_b.shape[1] == 3 * df_dim
    assert (4 * dn_dim) % GATE_COLS == 0 and w_all.shape[1] >= 4 * dn_dim + GATE_COLS
    assert S % tm == 0 and dn_dim % tn == 0 and df_dim % tn == 0 and tm % tv == 0
    assert tn % DN_HEAD_DIM == 0 and conv_w.shape == (CONV_WIDTH, 3 * dn_dim)
    assert tn % IN_PROJ_SUB == 0 and IN_PROJ_SUB % DN_HEAD_DIM == 0
    n_dn, n_df = dn_dim // tn, df_dim // tn
    bounds = (n_dn, 2 * n_dn, 3 * n_dn, 4 * n_dn, 4 * n_dn + n_df, 4 * n_dn + 2 * n_df)
    n_main = bounds[-1]
    n_conv = bounds[2]
    n_a = bounds[3]
    n_b = n_main - n_a + nv // tn
    per_seq = S // tm
    return pl.pallas_call(
        functools.partial(_in_proj_kernel, bounds=bounds, per_seq=per_seq, q_scale=q_scale),
        grid=(T // tm, n_main + nv // tn),
        in_specs=[
            pl.BlockSpec((tm, D), lambda i, j: (i, 0)),
            pl.BlockSpec((1, D), lambda i, j: (0, 0)),
            pl.BlockSpec((D, tn), lambda i, j: (0, jnp.where(j < n_a, j, 0))),
            pl.BlockSpec((D, tn), lambda i, j: (0, jnp.where(j < n_a, n_b - 1,
                                                             jnp.minimum(j - n_a, n_b - 1)))),
            pl.BlockSpec((D, GATE_COLS), lambda i, j: (0, 4 * dn_dim // GATE_COLS),
                         pipeline_mode=pl.Buffered(1)),
            pl.BlockSpec((CONV_WIDTH, tn), lambda i, j: (0, jnp.minimum(j, n_conv - 1))),
        ],
        out_specs=[
            pl.BlockSpec((tm, tn), lambda i, j: (i, jnp.minimum(j, n_main - 1))),
            pl.BlockSpec((tm, GATE_COLS), lambda i, j: (i, 0)),
            pl.BlockSpec((None, tm // tv, tn, tv),
                         lambda i, j: (i // per_seq, i % per_seq, jnp.maximum(j - n_main, 0), 0)),
        ],
        out_shape=[
            jax.ShapeDtypeStruct((T, N), BF16),
            jax.ShapeDtypeStruct((T, GATE_COLS), F32),
            jax.ShapeDtypeStruct((batch, S // tv, nv, tv), BF16),
        ],
        scratch_shapes=[
            pltpu.VMEM((tm, D), BF16),
            pltpu.VMEM((n_conv, 8, tn), F32),
            pltpu.VMEM((8 + tm, tn), F32),
        ],
        compiler_params=pltpu.CompilerParams(
            dimension_semantics=("arbitrary", "arbitrary"),
            vmem_limit_bytes=V7X_VMEM_LIMIT),
        name="in_proj",
    )(x2, gain, w_all, w_b, w_all, conv_w)


def _gdn_kernel(q_ref, k_ref, v_ref, z_ref, gate_ref, alog_ref, dtb_ref, dnn_ref,
                y_ref, state_ref, *, blk, n_heads):
    dk = DN_HEAD_DIM
    n_chunks = blk // CHUNK

    @pl.when(pl.program_id(1) == 0)
    def _():
        state_ref[...] = jnp.zeros_like(state_ref)

    gate = gate_ref[...]
    beta_all = _sigmoid(gate)
    xs = gate + dtb_ref[...]
    softplus = jnp.maximum(xs, 0.0) + jnp.log(1.0 + jnp.exp(-jnp.abs(xs)))
    g_all = -jnp.exp(alog_ref[...]) * softplus
    pos = lax.broadcasted_iota(jnp.int32, g_all.shape, 0) & (CHUNK - 1)
    gc_all = g_all
    step = 1
    while step < CHUNK:
        gc_all = gc_all + jnp.where(pos >= step, pltpu.roll(gc_all, step, 0), 0.0)
        step *= 2

    ri = lax.broadcasted_iota(jnp.int32, (CHUNK, CHUNK), 0)
    ci = lax.broadcasted_iota(jnp.int32, (CHUNK, CHUNK), 1)
    tril = ri >= ci
    strict = ri > ci
    eye = ri == ci
    gain = dnn_ref[...]

    def first_stage(c):
        rows = slice(c * CHUNK, (c + 1) * CHUNK)
        items = []
        for h in range(n_heads):
            cols = slice(h * dk, (h + 1) * dk)
            q_bf = q_ref[rows, cols]
            k_bf = k_ref[rows, cols]
            qc = q_bf.astype(F32)
            kc = k_bf.astype(F32)
            vc = v_ref[rows, cols].astype(F32)
            bc = jnp.broadcast_to(beta_all[rows, h:h + 1], (CHUNK, dk))
            gcc = jnp.broadcast_to(gc_all[rows, n_heads + h:n_heads + h + 1], (CHUNK, dk))
            g_last = gcc[CHUNK - 1:CHUNK, :]
            eg = jnp.exp(gcc)
            g_sq = gcc[:, 0:CHUNK]
            g_row = jnp.sum(jnp.where(eye, g_sq, 0.0), axis=0, keepdims=True)
            decay = jnp.exp(jnp.where(tril, g_sq - g_row, -jnp.inf))
            kb = kc * bc
            lhs = jnp.concatenate([kb.astype(BF16), q_bf], axis=0)
            aq = _dot_nt(lhs, k_bf)
            n_mat = jnp.where(strict, -(aq[:CHUNK] * decay), 0.0)
            items.append(dict(
                h=h, qe=qc * eg, g_last=g_last,
                rhs=jnp.concatenate([kb * eg, vc * bc], axis=1).astype(BF16),
                qk=jnp.where(tril, aq[CHUNK:] * decay, 0.0).astype(BF16),
                kd_t=(kc * jnp.exp(g_last - gcc)).T.astype(BF16),
                power=n_mat, inv=jnp.where(eye, 1.0, 0.0) + n_mat))
        return items

    def matrix_stages(items):
        span = 2
        while span < CHUNK:
            for s in items:
                pb = s["power"].astype(BF16)
                s["power"] = _dot(pb, pb)
            for s in items:
                s["inv"] = s["inv"] + _dot(s["inv"].astype(BF16), s["power"].astype(BF16))
            span *= 2
        for s in items:
            s["wu"] = _dot(s["inv"].astype(BF16), s["rhs"]).astype(BF16)
        for s in items:
            s["gr"] = _dot(s["kd_t"], s["wu"])
            qw = _dot(s["qk"], s["wu"])
            s["q_eff"] = (s["qe"] - qw[:, :dk]).astype(BF16)
            s["p_loc"] = qw[:, dk:]

    def state_stage(c, items):
        rows = slice(c * CHUNK, (c + 1) * CHUNK)
        for s in items:
            h = s["h"]
            state = state_ref[h]
            s_bf = state.astype(BF16)
            o = _dot(s["q_eff"], s_bf) + s["p_loc"]
            state_ref[h] = (state * jnp.exp(s["g_last"])
                            - _dot(s["gr"][:, :dk].astype(BF16), s_bf) + s["gr"][:, dk:])
            zs = z_ref[rows, h * dk:(h + 1) * dk].astype(F32)
            o = o * lax.rsqrt(jnp.mean(o * o, axis=-1, keepdims=True) + 1e-6) * gain
            y_ref[rows, h * dk:(h + 1) * dk] = (o * zs).astype(y_ref.dtype)

    groups = [list(range(g, min(g + GDN_GROUP, n_chunks))) for g in range(0, n_chunks, GDN_GROUP)]
    cur = [first_stage(c) for c in groups[0]]
    for gi, chunk_ids in enumerate(groups):
        nxt = [first_stage(c) for c in groups[gi + 1]] if gi + 1 < len(groups) else None
        matrix_stages([s for items in cur for s in items])
        for c, items in zip(chunk_ids, cur):
            state_stage(c, items)
        cur = nxt


def _gdn(proj3, gate3, alog_row, dtb_row, dn_norm, *, n_heads, blk):
    B, S, _ = proj3.shape
    dk = DN_HEAD_DIM
    dn = n_heads * dk

    def group(idx):
        return pl.BlockSpec((None, blk, dn), lambda b, t: (b, t, idx))

    def whole(arr):
        return pl.BlockSpec(arr.shape, lambda b, t: (0,) * arr.ndim)

    return pl.pallas_call(
        functools.partial(_gdn_kernel, blk=blk, n_heads=n_heads),
        grid=(B, S // blk),
        in_specs=[
            group(0), group(1), group(2), group(3),
            pl.BlockSpec((None, blk, GATE_COLS), lambda b, t: (b, t, 0)),
            whole(alog_row), whole(dtb_row), whole(dn_norm),
        ],
        out_specs=pl.BlockSpec((None, blk, dn), lambda b, t: (b, t, 0)),
        out_shape=jax.ShapeDtypeStruct((B, S, dn), BF16),
        scratch_shapes=[pltpu.VMEM((n_heads, dk, dk), F32)],
        compiler_params=pltpu.CompilerParams(
            dimension_semantics=("arbitrary", "arbitrary"),
            vmem_limit_bytes=V7X_VMEM_LIMIT),
        name="gdn",
    )(proj3, proj3, proj3, proj3, gate3, alog_row, dtb_row, dn_norm)


def _t5_bucket_starts():
    max_exact = NUM_BUCKETS // 2
    n = np.arange(0, MAX_DISTANCE + 1)
    nf = np.maximum(n, 1).astype(np.float32)
    large = max_exact + (np.log(nf / max_exact) / math.log(MAX_DISTANCE / max_exact)
                         * (NUM_BUCKETS - max_exact)).astype(np.int32)
    bucket = np.where(n < max_exact, n, np.minimum(large, NUM_BUCKETS - 1))
    assert bucket[MAX_DISTANCE] == NUM_BUCKETS - 1 and np.all(np.diff(bucket) >= 0)
    starts = [(0, int(bucket[0]))]
    for d in range(1, MAX_DISTANCE + 1):
        if bucket[d] != bucket[d - 1]:
            starts.append((d, int(bucket[d])))
    return starts


def _diff_attn_kernel(rb_ref, q_ref, k_ref, vt_ref, lq1_ref, lk1_ref, lq2_ref, lk2_ref, dfn_ref,
                      c0_ref, c1_ref, c2_ref, y_ref, o0_ref, o1_ref, o2_ref,
                      bias_ref, s_ref, p_ref, m_ref, l_ref, acc_ref,
                      *, tq, tk, n_heads, lam_init):
    h = pl.program_id(1)
    qi = pl.program_id(2)
    d = DF_HEAD_DIM

    for c_ref, o_ref in ((c0_ref, o0_ref), (c1_ref, o1_ref), (c2_ref, o2_ref)):
        o_ref[...] = c_ref[...].astype(o_ref.dtype)

    sub = ATTN_SUB
    nsq = tq // sub
    nsk = tk // sub
    log2e = math.log2(math.e)

    @pl.when(qi == 0)
    def _():
        keys = lax.broadcasted_iota(jnp.int32, (sub, sub), 0)
        qrys = lax.broadcasted_iota(jnp.int32, (sub, sub), 1)
        starts = _t5_bucket_starts()
        for idx in range(2):
            dist = qrys - keys + idx * sub
            tile = jnp.full((sub, sub), rb_ref[starts[0][1] * n_heads + h], F32)
            for first, bucket in starts[1:]:
                tile = jnp.where(dist >= first, rb_ref[bucket * n_heads + h], tile)
            tile = tile * log2e
            if idx == 0:
                tile = jnp.where(dist >= 0, tile, -jnp.inf)
            bias_ref[idx] = tile
        bias_ref[2] = jnp.full((sub, sub), rb_ref[(NUM_BUCKETS - 1) * n_heads + h] * log2e, F32)
        bias_ref[3] = jnp.full((sub, sub), -jnp.inf, F32)

    m_ref[...] = jnp.full(m_ref.shape, -jnp.inf, F32)
    l_ref[...] = jnp.zeros_like(l_ref)
    acc_ref[...] = jnp.zeros_like(acc_ref)

    chains = [(qb, m) for qb in range(nsq) for m in range(2)]

    far_shift = rb_ref[(NUM_BUCKETS - 1) * n_heads + h] * log2e

    def produce(j, c):
        qb, m = chains[c]
        k0 = pl.multiple_of(j * tk, tk)
        s_ref[c] = _dot_nt(k_ref[pl.ds(k0, tk), m * d:(m + 1) * d],
                           q_ref[qb * sub:(qb + 1) * sub, m * d:(m + 1) * d])

    def softmax(j, c, far):
        qb, m = chains[c]
        if far:
            s = s_ref[c]
            shift = far_shift
        else:
            parts = []
            for kb in range(nsk):
                off = (qi * nsq + qb) - (j * nsk + kb)
                idx = jnp.where(off < 0, 3, jnp.minimum(off, 2))
                parts.append(s_ref[c, kb * sub:(kb + 1) * sub, :] + bias_ref[idx])
            s = jnp.concatenate(parts, axis=0)
            shift = 0.0
        cols = slice(qb * sub, (qb + 1) * sub)
        m_prev = m_ref[m, :, cols]
        m_new = jnp.maximum(m_prev, jnp.max(s, axis=0, keepdims=True) + shift)
        alpha = jnp.exp2(m_prev - m_new)
        pr = jnp.exp2(s - (m_new - shift))
        l_ref[m, :, cols] = alpha * l_ref[m, :, cols] + jnp.sum(pr, axis=0, keepdims=True)
        acc_ref[m, :, cols] = alpha * acc_ref[m, :, cols]
        m_ref[m, :, cols] = m_new
        return pr.astype(BF16)

    def add_values(j, c, p):
        qb, m = chains[c]
        cols = slice(qb * sub, (qb + 1) * sub)
        acc_ref[m, :, cols] += _dot(vt_ref[j], p)

    n_chains = len(chains)
    for c in range(n_chains):
        produce(0, c)
    p_ref[...] = softmax(0, 0, False)

    def trip(j, far):
        add_values(j, 0, p_ref[...])
        produce(j + 1, 0)
        for c in range(1, n_chains):
            add_values(j, c, softmax(j, c, far))
            produce(j + 1, c)
        p_ref[...] = softmax(j + 1, 0, far)

    last = (qi + 1) * (tq // tk) - 1
    n_far = jnp.maximum((qi * nsq - 1) // nsk - 1, 0)
    lax.fori_loop(0, n_far, lambda j, carry: (trip(j, True), carry)[1], 0)
    lax.fori_loop(n_far, last, lambda j, carry: (trip(j, False), carry)[1], 0)
    live = [c for c in range(n_chains) if chains[c][0] >= nsq - nsk]
    for c in live:
        add_values(last, c, p_ref[...] if c == 0 else softmax(last, c, False))

    lam = (jnp.exp(jnp.sum(lq1_ref[...] * lk1_ref[...], axis=-1, keepdims=True))
           - jnp.exp(jnp.sum(lq2_ref[...] * lk2_ref[...], axis=-1, keepdims=True))
           + lam_init)
    o = acc_ref[0] * (1.0 / l_ref[0]) - acc_ref[1] * (lam / l_ref[1])
    o = o * lax.rsqrt(jnp.mean(o * o, axis=0, keepdims=True) + 1e-5) * dfn_ref[...]
    y_ref[...] = (o * (1.0 - lam_init)).T.astype(y_ref.dtype)


def _diff_attn(proj3, v_t, rel_bias, lq1, lk1, lq2, lk2, df_norm_col, to_cast, *, n_heads, col0,
               tq, tk, lam_init):
    B, S, _ = proj3.shape
    d2 = 2 * DF_HEAD_DIM
    assert tq % tk == 0 and tk % ATTN_SUB == 0 and ATTN_SUB >= MAX_DISTANCE and S % tq == 0
    nq = S // tq
    n_chains = 2 * (tq // ATTN_SUB)
    cb = col0 // d2
    vec = lambda n: pl.BlockSpec((1, n), lambda b, h, i: (0, 0))
    n_steps = B * n_heads * nq

    def slab(w):
        rows = w.shape[0] // n_steps
        assert w.shape[0] % n_steps == 0 and rows % 16 == 0
        return pl.BlockSpec((rows, w.shape[1]), lambda b, h, i: ((b * n_heads + h) * nq + i, 0))

    return pl.pallas_call(
        functools.partial(_diff_attn_kernel, tq=tq, tk=tk, n_heads=n_heads, lam_init=lam_init),
        grid=(B, n_heads, nq),
        in_specs=[
            pl.BlockSpec(memory_space=pltpu.SMEM),
            pl.BlockSpec((None, tq, d2), lambda b, h, i: (b, i, cb + h)),
            pl.BlockSpec((None, S, d2), lambda b, h, i: (b, 0, cb + n_heads + h)),
            pl.BlockSpec((None, S // tk, d2, tk), lambda b, h, i: (b, 0, h, 0)),
            vec(DF_HEAD_DIM), vec(DF_HEAD_DIM), vec(DF_HEAD_DIM), vec(DF_HEAD_DIM),
            pl.BlockSpec((d2, 1), lambda b, h, i: (0, 0)),
        ] + [slab(w) for w in to_cast],
        out_specs=[pl.BlockSpec((None, tq, d2), lambda b, h, i: (b, i, h))]
        + [slab(w) for w in to_cast],
        out_shape=[jax.ShapeDtypeStruct((B, S, n_heads * d2), BF16)]
        + [jax.ShapeDtypeStruct(w.shape, BF16) for w in to_cast],
        scratch_shapes=[
            pltpu.VMEM((4, ATTN_SUB, ATTN_SUB), F32),
            pltpu.VMEM((n_chains, tk, ATTN_SUB), F32),
            pltpu.VMEM((tk, ATTN_SUB), BF16),
            pltpu.VMEM((2, 1, tq), F32),
            pltpu.VMEM((2, 1, tq), F32),
            pltpu.VMEM((2, d2, tq), F32),
        ],
        compiler_params=pltpu.CompilerParams(
            dimension_semantics=("arbitrary", "arbitrary", "arbitrary"),
            vmem_limit_bytes=V7X_VMEM_LIMIT),
        name="diff_attn",
    )(rel_bias, proj3, proj3, v_t, lq1, lk1, lq2, lk2, df_norm_col, *to_cast)


def _out_proj_kernel(x_ref, ya_ref, yb_ref, wa_ref, wb_ref, h_ref):
    h_ref[...] = x_ref[...] + _dot(ya_ref[...], wa_ref[...]) + _dot(yb_ref[...], wb_ref[...])


def _out_proj(x2, y_dn, y_df, w_o, *, tm, tn):
    T, D = x2.shape
    ka = y_dn.shape[1]
    kb = y_df.shape[1]
    assert ka == kb
    return pl.pallas_call(
        _out_proj_kernel,
        grid=(T // tm, D // tn),
        in_specs=[
            pl.BlockSpec((tm, tn), lambda i, j: (i, j)),
            pl.BlockSpec((tm, ka), lambda i, j: (i, 0)),
            pl.BlockSpec((tm, kb), lambda i, j: (i, 0)),
            pl.BlockSpec((ka, tn), lambda i, j: (0, j)),
            pl.BlockSpec((kb, tn), lambda i, j: (1, j)),
        ],
        out_specs=pl.BlockSpec((tm, tn), lambda i, j: (i, j)),
        out_shape=jax.ShapeDtypeStruct((T, D), F32),
        compiler_params=pltpu.CompilerParams(
            dimension_semantics=("arbitrary", "arbitrary"),
            vmem_limit_bytes=V7X_VMEM_LIMIT),
        name="out_proj",
    )(x2, y_dn, y_df, w_o, w_o)


def _mlp_kernel(h_ref, hn_ref, g_ref, wu_ref, wd_ref, gf_ref, o_ref, u_ref, acc_ref, hid_ref,
                *, n_f):
    i = pl.program_id(0)
    f = pl.program_id(1)

    def normalise(x_ref, slot):
        x = x_ref[...]
        ms = jnp.mean(x * x, axis=-1, keepdims=True)
        u_ref[slot] = (x * lax.rsqrt(ms + 1e-6) * g_ref[...]).astype(BF16)

    def up(slot):
        hid = jnp.maximum(_dot(u_ref[i % 2], wu_ref[...]), 0.0)
        hid_ref[slot] = (hid * hid).astype(BF16)

    def down(slot):
        acc_ref[...] += _dot(hid_ref[slot], wd_ref[...])

    @pl.when((f == 0) & (i == 0))
    def _():
        normalise(h_ref, 0)

    @pl.when(f == 0)
    def _():
        acc_ref[...] = jnp.zeros_like(acc_ref)
        up(0)

    for parity in range(2):
        @pl.when((f > 0) & (f < n_f) & (f % 2 == parity))
        def _():
            up(parity)
            down(1 - parity)

    @pl.when(f == n_f)
    def _():
        normalise(hn_ref, (i + 1) % 2)
        down((n_f - 1) % 2)
        y = h_ref[...] + acc_ref[...]
        ms = jnp.mean(y * y, axis=-1, keepdims=True)
        o_ref[...] = y * lax.rsqrt(ms + 1e-6) * gf_ref[...]


def _mlp(h1, gain, w_up, w_down, final_gain, *, tm, tf):
    T, D = h1.shape
    Fdim = w_up.shape[1]
    n_f = Fdim // tf
    n_i = T // tm
    return pl.pallas_call(
        functools.partial(_mlp_kernel, n_f=n_f),
        grid=(n_i, n_f + 1),
        in_specs=[
            pl.BlockSpec((tm, D), lambda i, f: (i, 0)),
            pl.BlockSpec((tm, D), lambda i, f: (jnp.minimum(i + 1, n_i - 1), 0)),
            pl.BlockSpec((1, D), lambda i, f: (0, 0)),
            pl.BlockSpec((D, tf), lambda i, f: (0, jnp.minimum(f, n_f - 1))),
            pl.BlockSpec((tf, D), lambda i, f: (jnp.maximum(f - 1, 0), 0)),
            pl.BlockSpec((1, D), lambda i, f: (0, 0)),
        ],
        out_specs=pl.BlockSpec((tm, D), lambda i, f: (i, 0)),
        out_shape=jax.ShapeDtypeStruct((T, D), F32),
        scratch_shapes=[
            pltpu.VMEM((2, tm, D), BF16),
            pltpu.VMEM((tm, D), F32),
            pltpu.VMEM((2, tm, tf), BF16),
        ],
        compiler_params=pltpu.CompilerParams(
            dimension_semantics=("arbitrary", "arbitrary"),
            vmem_limit_bytes=V7X_VMEM_LIMIT),
        name="mlp",
    )(h1, h1, gain, w_up, w_down, final_gain)


def _tile(n, pref):
    if n <= pref:
        return n
    t = pref - pref % 128
    while t > 128 and n % t:
        t -= 128
    assert n % t == 0
    return t


def kernel(x, attn_norm, w_in, conv_w, a_log, dt_bias, dn_norm, lambda_q1, lambda_k1,
           lambda_q2, lambda_k2, df_norm, rel_bias, w_o, mlp_norm, w_up, w_down, final_norm):
    B, S, D = x.shape
    depth = attn_norm.shape[0]
    n_dn = a_log.shape[1]
    n_df = rel_bias.shape[1]
    dn_dim = n_dn * DN_HEAD_DIM
    df_dim = n_df * 2 * DF_HEAD_DIM
    T = B * S
    gate0 = 4 * dn_dim
    assert w_in.shape[2] == gate0 + 2 * n_dn + 3 * df_dim and 2 * n_dn <= GATE_COLS

    assert depth == 1
    l = 0
    h = x.reshape(T, D)

    wl = w_in[l]
    dfq0 = gate0 + 2 * n_dn
    w_all = wl.astype(BF16)
    w_b = w_all[:, dfq0:]

    proj, gates, v_t = _in_proj(h, attn_norm[l][None, :], w_all, w_b, conv_w[l],
                                batch=B, dn_dim=dn_dim, df_dim=df_dim, tm=_tile(S, IN_PROJ_ROWS),
                                tn=_tile(math.gcd(dn_dim, df_dim), IN_PROJ_COLS),
                                tv=_tile(S, ATTN_BLOCK),
                                q_scale=DF_HEAD_DIM ** -0.5 * math.log2(math.e))
    proj3 = proj.reshape(B, S, -1)
    gate3 = gates.reshape(B, S, GATE_COLS)

    gate_pad = ((0, 0), (n_dn, GATE_COLS - 2 * n_dn))
    alog_row = jnp.pad(a_log[l][None, :], gate_pad)
    dtb_row = jnp.pad(dt_bias[l][None, :], gate_pad)
    y_dn = _gdn(proj3, gate3, alog_row, dtb_row, dn_norm[l][None, :],
                n_heads=n_dn, blk=_tile(S, GDN_BLOCK))
    lam_init = 0.8 - 0.6 * math.exp(-0.3 * l)
    y_df, w_o_bf, w_up_bf, w_down_bf = _diff_attn(
                      proj3, v_t, rel_bias.reshape(-1),
                      lambda_q1[l][None, :], lambda_k1[l][None, :],
                      lambda_q2[l][None, :], lambda_k2[l][None, :], df_norm[l][:, None],
                      (w_o[l], w_up[l], w_down[l]),
                      n_heads=n_df, col0=gate0, tq=_tile(S, ATTN_QBLOCK), tk=_tile(S, ATTN_BLOCK),
                      lam_init=lam_init)

    h1 = _out_proj(h, y_dn.reshape(T, dn_dim), y_df.reshape(T, df_dim), w_o_bf,
                   tm=_tile(T, 512), tn=_tile(D, 2048))
    out = _mlp(h1, mlp_norm[l][None, :], w_up_bf, w_down_bf, final_norm[None, :],
               tm=_tile(T, 512), tf=_tile(w_up_bf.shape[1], 1024))
    return out.reshape(B, S, D)
```

```python
import functools
import math

import numpy as np
import jax
import jax.numpy as jnp
from jax import lax
from jax.experimental import pallas as pl
from jax.experimental.pallas import tpu as pltpu

F32 = jnp.float32
BF16 = jnp.bfloat16

DN_HEAD_DIM = 128
DF_HEAD_DIM = 128
CONV_WIDTH = 4
CHUNK = 64
NUM_BUCKETS = 32
MAX_DISTANCE = 128
GATE_COLS = 128
IN_PROJ_ROWS = 1024
IN_PROJ_COLS = 1024
IN_PROJ_SUB = 256
GDN_BLOCK = 512
GDN_GROUP = 4
ATTN_SUB = 256
ATTN_BLOCK = 512
ATTN_QBLOCK = 1024

V7X_VMEM_LIMIT = 58 * 1024 * 1024


def _dot(a, b):
    return jnp.dot(a, b, preferred_element_type=F32)


def _dot_nt(a, b):
    return lax.dot_general(a, b, (((1,), (1,)), ((), ())), preferred_element_type=F32)


def _sigmoid(x):
    return 1.0 / (1.0 + jnp.exp(-x))


def _in_proj_kernel(x_ref, g_ref, wa_ref, wb_ref, wg_ref, cw_ref, proj_ref, gate_ref, vt_ref,
                    u_ref, hist_ref, cbuf_ref, *, bounds, per_seq, q_scale):
    i = pl.program_id(0)
    j = pl.program_id(1)
    tm, tn = proj_ref.shape
    b0, b1, b2, b3, b4, b5 = bounds

    @pl.when(j == 0)
    def _():
        x = x_ref[...]
        ms = jnp.mean(x * x, axis=-1, keepdims=True)
        u = (x * lax.rsqrt(ms + 1e-6) * g_ref[...]).astype(BF16)
        u_ref[...] = u
        gate_ref[...] = _dot(u, wg_ref[...])

    n_col_chunks = tn // IN_PROJ_SUB
    first_of_seq = (i % per_seq) == 0

    def chunk_cols(c):
        return slice(c * IN_PROJ_SUB, (c + 1) * IN_PROJ_SUB)

    def raw_cols(w_ref, c):
        cbuf_ref[8:8 + tm, chunk_cols(c)] = _dot(u_ref[...], w_ref[:, chunk_cols(c)])

    def raw(c):
        return cbuf_ref[8:8 + tm, chunk_cols(c)]

    def conv_silu(c):
        cw = cw_ref[:, chunk_cols(c)]
        y = None
        for s in range(CONV_WIDTH):
            tap = CONV_WIDTH - 1 - s
            term = cbuf_ref[8 - s:8 - s + tm, chunk_cols(c)] * cw[tap:tap + 1, :]
            y = term if y is None else y + term
        return y * _sigmoid(y)

    def l2norm_heads(y, scale):
        outs = []
        for h in range(IN_PROJ_SUB // DN_HEAD_DIM):
            yh = y[:, h * DN_HEAD_DIM:(h + 1) * DN_HEAD_DIM]
            outs.append(yh * (lax.rsqrt(jnp.sum(yh * yh, axis=-1, keepdims=True) + 1e-6) * scale))
        return jnp.concatenate(outs, axis=1)

    def silu_cols(c):
        z = raw(c)
        return z * _sigmoid(z)

    def project(w_ref, epilogue, conv=False):
        if conv:
            cbuf_ref[0:8, :] = jnp.where(first_of_seq, 0.0, hist_ref[j])
        raw_cols(w_ref, 0)
        for c in range(n_col_chunks):
            if c + 1 < n_col_chunks:
                raw_cols(w_ref, c + 1)
            proj_ref[:, chunk_cols(c)] = epilogue(c).astype(BF16)
        if conv:
            hist_ref[j] = cbuf_ref[tm:tm + 8, :]

    @pl.when(j < b0)
    def _():
        project(wa_ref, lambda c: l2norm_heads(conv_silu(c), DN_HEAD_DIM ** -0.5), conv=True)

    @pl.when((j >= b0) & (j < b1))
    def _():
        project(wa_ref, lambda c: l2norm_heads(conv_silu(c), 1.0), conv=True)

    @pl.when((j >= b1) & (j < b2))
    def _():
        project(wa_ref, conv_silu, conv=True)

    @pl.when((j >= b2) & (j < b3))
    def _():
        project(wa_ref, silu_cols)

    @pl.when((j >= b3) & (j < b4))
    def _():
        project(wb_ref, lambda c: raw(c) * q_scale)

    @pl.when((j >= b4) & (j < b5))
    def _():
        project(wb_ref, raw)

    @pl.when(j >= b5)
    def _():
        tv = vt_ref.shape[-1]
        raw_cols(wb_ref, 0)
        for c in range(n_col_chunks):
            if c + 1 < n_col_chunks:
                raw_cols(wb_ref, c + 1)
            vt = raw(c).T.astype(BF16)
            for tb in range(vt_ref.shape[0]):
                vt_ref[tb, chunk_cols(c), :] = vt[:, tb * tv:(tb + 1) * tv]


def _in_proj(x2, gain, w_all, w_b, conv_w, *, batch, dn_dim, df_dim, tm, tn, tv, q_scale):
    T, D = x2.shape
    N = 4 * dn_dim + 2 * df_dim
    nv = df_dim
    S = T // batch
    assert w_b.shape[1] == 3 * df_dim
    assert (4 * dn_dim) % GATE_COLS == 0 and w_all.shape[1] >= 4 * dn_dim + GATE_COLS
    assert S % tm == 0 and dn_dim % tn == 0 and df_dim % tn == 0 and tm % tv == 0
    assert tn % DN_HEAD_DIM == 0 and conv_w.shape == (CONV_WIDTH, 3 * dn_dim)
    assert tn % IN_PROJ_SUB == 0 and IN_PROJ_SUB % DN_HEAD_DIM == 0
    n_dn, n_df = dn_dim // tn, df_dim // tn
    bounds = (n_dn, 2 * n_dn, 3 * n_dn, 4 * n_dn, 4 * n_dn + n_df, 4 * n_dn + 2 * n_df)
    n_main = bounds[-1]
    n_conv = bounds[2]
    n_a = bounds[3]
    n_b = n_main - n_a + nv // tn
    per_seq = S // tm
    return pl.pallas_call(
        functools.partial(_in_proj_kernel, bounds=bounds, per_seq=per_seq, q_scale=q_scale),
        grid=(T // tm, n_main + nv // tn),
        in_specs=[
            pl.BlockSpec((tm, D), lambda i, j: (i, 0)),
            pl.BlockSpec((1, D), lambda i, j: (0, 0)),
            pl.BlockSpec((D, tn), lambda i, j: (0, jnp.where(j < n_a, j, 0))),
            pl.BlockSpec((D, tn), lambda i, j: (0, jnp.where(j < n_a, n_b - 1,
                                                             jnp.minimum(j - n_a, n_b - 1)))),
            pl.BlockSpec((D, GATE_COLS), lambda i, j: (0, 4 * dn_dim // GATE_COLS),
                         pipeline_mode=pl.Buffered(1)),
            pl.BlockSpec((CONV_WIDTH, tn), lambda i, j: (0, jnp.minimum(j, n_conv - 1))),
        ],
        out_specs=[
            pl.BlockSpec((tm, tn), lambda i, j: (i, jnp.minimum(j, n_main - 1))),
            pl.BlockSpec((tm, GATE_COLS), lambda i, j: (i, 0)),
            pl.BlockSpec((None, tm // tv, tn, tv),
                         lambda i, j: (i // per_seq, i % per_seq, jnp.maximum(j - n_main, 0), 0)),
        ],
        out_shape=[
            jax.ShapeDtypeStruct((T, N), BF16),
            jax.ShapeDtypeStruct((T, GATE_COLS), F32),
            jax.ShapeDtypeStruct((batch, S // tv, nv, tv), BF16),
        ],
        scratch_shapes=[
            pltpu.VMEM((tm, D), BF16),
            pltpu.VMEM((n_conv, 8, tn), F32),
            pltpu.VMEM((8 + tm, tn), F32),
        ],
        compiler_params=pltpu.CompilerParams(
            dimension_semantics=("arbitrary", "arbitrary"),
            vmem_limit_bytes=V7X_VMEM_LIMIT),
        name="in_proj",
    )(x2, gain, w_all, w_b, w_all, conv_w)


def _gdn_kernel(q_ref, k_ref, v_ref, z_ref, gate_ref, alog_ref, dtb_ref, dnn_ref,
                y_ref, state_ref, *, blk, n_heads):
    dk = DN_HEAD_DIM
    n_chunks = blk // CHUNK

    @pl.when(pl.program_id(1) == 0)
    def _():
        state_ref[...] = jnp.zeros_like(state_ref)

    gate = gate_ref[...]
    beta_all = _sigmoid(gate)
    xs = gate + dtb_ref[...]
    softplus = jnp.maximum(xs, 0.0) + jnp.log(1.0 + jnp.exp(-jnp.abs(xs)))
    g_all = -jnp.exp(alog_ref[...]) * softplus
    pos = lax.broadcasted_iota(jnp.int32, g_all.shape, 0) & (CHUNK - 1)
    gc_all = g_all
    step = 1
    while step < CHUNK:
        gc_all = gc_all + jnp.where(pos >= step, pltpu.roll(gc_all, step, 0), 0.0)
        step *= 2

    ri = lax.broadcasted_iota(jnp.int32, (CHUNK, CHUNK), 0)
    ci = lax.broadcasted_iota(jnp.int32, (CHUNK, CHUNK), 1)
    tril = ri >= ci
    strict = ri > ci
    eye = ri == ci
    gain = dnn_ref[...]

    def first_stage(c):
        rows = slice(c * CHUNK, (c + 1) * CHUNK)
        items = []
        for h in range(n_heads):
            cols = slice(h * dk, (h + 1) * dk)
            q_bf = q_ref[rows, cols]
            k_bf = k_ref[rows, cols]
            qc = q_bf.astype(F32)
            kc = k_bf.astype(F32)
            vc = v_ref[rows, cols].astype(F32)
            bc = jnp.broadcast_to(beta_all[rows, h:h + 1], (CHUNK, dk))
            gcc = jnp.broadcast_to(gc_all[rows, n_heads + h:n_heads + h + 1], (CHUNK, dk))
            g_last = gcc[CHUNK - 1:CHUNK, :]
            eg = jnp.exp(gcc)
            g_sq = gcc[:, 0:CHUNK]
            g_row = jnp.sum(jnp.where(eye, g_sq, 0.0), axis=0, keepdims=True)
            decay = jnp.exp(jnp.where(tril, g_sq - g_row, -jnp.inf))
            kb = kc * bc
            lhs = jnp.concatenate([kb.astype(BF16), q_bf], axis=0)
            aq = _dot_nt(lhs, k_bf)
            n_mat = jnp.where(strict, -(aq[:CHUNK] * decay), 0.0)
            items.append(dict(
                h=h, qe=qc * eg, g_last=g_last,
                rhs=jnp.concatenate([kb * eg, vc * bc], axis=1).astype(BF16),
                qk=jnp.where(tril, aq[CHUNK:] * decay, 0.0).astype(BF16),
                kd_t=(kc * jnp.exp(g_last - gcc)).T.astype(BF16),
                power=n_mat, inv=jnp.where(eye, 1.0, 0.0) + n_mat))
        return items

    def matrix_stages(items):
        span = 2
        while span < CHUNK:
            for s in items:
                pb = s["power"].astype(BF16)
                s["power"] = _dot(pb, pb)
            for s in items:
                s["inv"] = s["inv"] + _dot(s["inv"].astype(BF16), s["power"].astype(BF16))
            span *= 2
        for s in items:
            s["wu"] = _dot(s["inv"].astype(BF16), s["rhs"]).astype(BF16)
        for s in items:
            s["gr"] = _dot(s["kd_t"], s["wu"])
            qw = _dot(s["qk"], s["wu"])
            s["q_eff"] = (s["qe"] - qw[:, :dk]).astype(BF16)
            s["p_loc"] = qw[:, dk:]

    def state_stage(c, items):
        rows = slice(c * CHUNK, (c + 1) * CHUNK)
        for s in items:
            h = s["h"]
            state = state_ref[h]
            s_bf = state.astype(BF16)
            o = _dot(s["q_eff"], s_bf) + s["p_loc"]
            state_ref[h] = (state * jnp.exp(s["g_last"])
                            - _dot(s["gr"][:, :dk].astype(BF16), s_bf) + s["gr"][:, dk:])
            zs = z_ref[rows, h * dk:(h + 1) * dk].astype(F32)
            o = o * lax.rsqrt(jnp.mean(o * o, axis=-1, keepdims=True) + 1e-6) * gain
            y_ref[rows, h * dk:(h + 1) * dk] = (o * zs).astype(y_ref.dtype)

    groups = [list(range(g, min(g + GDN_GROUP, n_chunks))) for g in range(0, n_chunks, GDN_GROUP)]
    cur = [first_stage(c) for c in groups[0]]
    for gi, chunk_ids in enumerate(groups):
        nxt = [first_stage(c) for c in groups[gi + 1]] if gi + 1 < len(groups) else None
        matrix_stages([s for items in cur for s in items])
        for c, items in zip(chunk_ids, cur):
            state_stage(c, items)
        cur = nxt


def _gdn(proj3, gate3, alog_row, dtb_row, dn_norm, *, n_heads, blk):
    B, S, _ = proj3.shape
    dk = DN_HEAD_DIM
    dn = n_heads * dk

    def group(idx):
        return pl.BlockSpec((None, blk, dn), lambda b, t: (b, t, idx))

    def whole(arr):
        return pl.BlockSpec(arr.shape, lambda b, t: (0,) * arr.ndim)

    return pl.pallas_call(
        functools.partial(_gdn_kernel, blk=blk, n_heads=n_heads),
        grid=(B, S // blk),
        in_specs=[
            group(0), group(1), group(2), group(3),
            pl.BlockSpec((None, blk, GATE_COLS), lambda b, t: (b, t, 0)),
            whole(alog_row), whole(dtb_row), whole(dn_norm),
        ],
        out_specs=pl.BlockSpec((None, blk, dn), lambda b, t: (b, t, 0)),
        out_shape=jax.ShapeDtypeStruct((B, S, dn), BF16),
        scratch_shapes=[pltpu.VMEM((n_heads, dk, dk), F32)],
        compiler_params=pltpu.CompilerParams(
            dimension_semantics=("arbitrary", "arbitrary"),
            vmem_limit_bytes=V7X_VMEM_LIMIT),
        name="gdn",
    )(proj3, proj3, proj3, proj3, gate3, alog_row, dtb_row, dn_norm)


def _t5_bucket_starts():
    max_exact = NUM_BUCKETS // 2
    n = np.arange(0, MAX_DISTANCE + 1)
    nf = np.maximum(n, 1).astype(np.float32)
    large = max_exact + (np.log(nf / max_exact) / math.log(MAX_DISTANCE / max_exact)
                         * (NUM_BUCKETS - max_exact)).astype(np.int32)
    bucket = np.where(n < max_exact, n, np.minimum(large, NUM_BUCKETS - 1))
    assert bucket[MAX_DISTANCE] == NUM_BUCKETS - 1 and np.all(np.diff(bucket) >= 0)
    starts = [(0, int(bucket[0]))]
    for d in range(1, MAX_DISTANCE + 1):
        if bucket[d] != bucket[d - 1]:
            starts.append((d, int(bucket[d])))
    return starts


def _diff_attn_kernel(rb_ref, q_ref, k_ref, vt_ref, lq1_ref, lk1_ref, lq2_ref, lk2_ref, dfn_ref,
                      c0_ref, c1_ref, c2_ref, y_ref, o0_ref, o1_ref, o2_ref,
                      bias_ref, s_ref, p_ref, m_ref, l_ref, acc_ref,
                      *, tq, tk, n_heads, lam_init):
    h = pl.program_id(1)
    qi = pl.program_id(2)
    d = DF_HEAD_DIM

    for c_ref, o_ref in ((c0_ref, o0_ref), (c1_ref, o1_ref), (c2_ref, o2_ref)):
        o_ref[...] = c_ref[...].astype(o_ref.dtype)

    sub = ATTN_SUB
    nsq = tq // sub
    nsk = tk // sub
    log2e = math.log2(math.e)

    @pl.when(qi == 0)
    def _():
        keys = lax.broadcasted_iota(jnp.int32, (sub, sub), 0)
        qrys = lax.broadcasted_iota(jnp.int32, (sub, sub), 1)
        starts = _t5_bucket_starts()
        for idx in range(2):
            dist = qrys - keys + idx * sub
            tile = jnp.full((sub, sub), rb_ref[starts[0][1] * n_heads + h], F32)
            for first, bucket in starts[1:]:
                tile = jnp.where(dist >= first, rb_ref[bucket * n_heads + h], tile)
            tile = tile * log2e
            if idx == 0:
                tile = jnp.where(dist >= 0, tile, -jnp.inf)
            bias_ref[idx] = tile
        bias_ref[2] = jnp.full((sub, sub), rb_ref[(NUM_BUCKETS - 1) * n_heads + h] * log2e, F32)
        bias_ref[3] = jnp.full((sub, sub), -jnp.inf, F32)

    m_ref[...] = jnp.full(m_ref.shape, -jnp.inf, F32)
    l_ref[...] = jnp.zeros_like(l_ref)
    acc_ref[...] = jnp.zeros_like(acc_ref)

    chains = [(qb, m) for qb in range(nsq) for m in range(2)]

    far_shift = rb_ref[(NUM_BUCKETS - 1) * n_heads + h] * log2e

    def produce(j, c):
        qb, m = chains[c]
        k0 = pl.multiple_of(j * tk, tk)
        s_ref[c] = _dot_nt(k_ref[pl.ds(k0, tk), m * d:(m + 1) * d],
                           q_ref[qb * sub:(qb + 1) * sub, m * d:(m + 1) * d])

    def softmax(j, c, far):
        qb, m = chains[c]
        if far:
            s = s_ref[c]
            shift = far_shift
        else:
            parts = []
            for kb in range(nsk):
                off = (qi * nsq + qb) - (j * nsk + kb)
                idx = jnp.where(off < 0, 3, jnp.minimum(off, 2))
                parts.append(s_ref[c, kb * sub:(kb + 1) * sub, :] + bias_ref[idx])
            s = jnp.concatenate(parts, axis=0)
            shift = 0.0
        cols = slice(qb * sub, (qb + 1) * sub)
        m_prev = m_ref[m, :, cols]
        m_new = jnp.maximum(m_prev, jnp.max(s, axis=0, keepdims=True) + shift)
        alpha = jnp.exp2(m_prev - m_new)
        pr = jnp.exp2(s - (m_new - shift))
        l_ref[m, :, cols] = alpha * l_ref[m, :, cols] + jnp.sum(pr, axis=0, keepdims=True)
        acc_ref[m, :, cols] = alpha * acc_ref[m, :, cols]
        m_ref[m, :, cols] = m_new
        return pr.astype(BF16)

    def add_values(j, c, p):
        qb, m = chains[c]
        cols = slice(qb * sub, (qb + 1) * sub)
        acc_ref[m, :, cols] += _dot(vt_ref[j], p)

    n_chains = len(chains)
    for c in range(n_chains):
        produce(0, c)
    p_ref[...] = softmax(0, 0, False)

    def trip(j, far):
        add_values(j, 0, p_ref[...])
        produce(j + 1, 0)
        for c in range(1, n_chains):
            add_values(j, c, softmax(j, c, far))
            produce(j + 1, c)
        p_ref[...] = softmax(j + 1, 0, far)

    last = (qi + 1) * (tq // tk) - 1
    n_far = jnp.maximum((qi * nsq - 1) // nsk - 1, 0)
    lax.fori_loop(0, n_far, lambda j, carry: (trip(j, True), carry)[1], 0)
    lax.fori_loop(n_far, last, lambda j, carry: (trip(j, False), carry)[1], 0)
    live = [c for c in range(n_chains) if chains[c][0] >= nsq - nsk]
    for c in live:
        add_values(last, c, p_ref[...] if c == 0 else softmax(last, c, False))

    lam = (jnp.exp(jnp.sum(lq1_ref[...] * lk1_ref[...], axis=-1, keepdims=True))
           - jnp.exp(jnp.sum(lq2_ref[...] * lk2_ref[...], axis=-1, keepdims=True))
           + lam_init)
    o = acc_ref[0] * (1.0 / l_ref[0]) - acc_ref[1] * (lam / l_ref[1])
    o = o * lax.rsqrt(jnp.mean(o * o, axis=0, keepdims=True) + 1e-5) * dfn_ref[...]
    y_ref[...] = (o * (1.0 - lam_init)).T.astype(y_ref.dtype)


def _diff_attn(proj3, v_t, rel_bias, lq1, lk1, lq2, lk2, df_norm_col, to_cast, *, n_heads, col0,
               tq, tk, lam_init):
    B, S, _ = proj3.shape
    d2 = 2 * DF_HEAD_DIM
    assert tq % tk == 0 and tk % ATTN_SUB == 0 and ATTN_SUB >= MAX_DISTANCE and S % tq == 0
    nq = S // tq
    n_chains = 2 * (tq // ATTN_SUB)
    cb = col0 // d2
    vec = lambda n: pl.BlockSpec((1, n), lambda b, h, i: (0, 0))
    n_steps = B * n_heads * nq

    def slab(w):
        rows = w.shape[0] // n_steps
        assert w.shape[0] % n_steps == 0 and rows % 16 == 0
        return pl.BlockSpec((rows, w.shape[1]), lambda b, h, i: ((b * n_heads + h) * nq + i, 0))

    return pl.pallas_call(
        functools.partial(_diff_attn_kernel, tq=tq, tk=tk, n_heads=n_heads, lam_init=lam_init),
        grid=(B, n_heads, nq),
        in_specs=[
            pl.BlockSpec(memory_space=pltpu.SMEM),
            pl.BlockSpec((None, tq, d2), lambda b, h, i: (b, i, cb + h)),
            pl.BlockSpec((None, S, d2), lambda b, h, i: (b, 0, cb + n_heads + h)),
            pl.BlockSpec((None, S // tk, d2, tk), lambda b, h, i: (b, 0, h, 0)),
            vec(DF_HEAD_DIM), vec(DF_HEAD_DIM), vec(DF_HEAD_DIM), vec(DF_HEAD_DIM),
            pl.BlockSpec((d2, 1), lambda b, h, i: (0, 0)),
        ] + [slab(w) for w in to_cast],
        out_specs=[pl.BlockSpec((None, tq, d2), lambda b, h, i: (b, i, h))]
        + [slab(w) for w in to_cast],
        out_shape=[jax.ShapeDtypeStruct((B, S, n_heads * d2), BF16)]
        + [jax.ShapeDtypeStruct(w.shape, BF16) for w in to_cast],
        scratch_shapes=[
            pltpu.VMEM((4, ATTN_SUB, ATTN_SUB), F32),
            pltpu.VMEM((n_chains, tk, ATTN_SUB), F32),
            pltpu.VMEM((tk, ATTN_SUB), BF16),
            pltpu.VMEM((2, 1, tq), F32),
            pltpu.VMEM((2, 1, tq), F32),
            pltpu.VMEM((2, d2, tq), F32),
        ],
        compiler_params=pltpu.CompilerParams(
            dimension_semantics=("arbitrary", "arbitrary", "arbitrary"),
            vmem_limit_bytes=V7X_VMEM_LIMIT),
        name="diff_attn",
    )(rel_bias, proj3, proj3, v_t, lq1, lk1, lq2, lk2, df_norm_col, *to_cast)


def _out_proj_kernel(x_ref, ya_ref, yb_ref, wa_ref, wb_ref, h_ref):
    h_ref[...] = x_ref[...] + _dot(ya_ref[...], wa_ref[...]) + _dot(yb_ref[...], wb_ref[...])


def _out_proj(x2, y_dn, y_df, w_o, *, tm, tn):
    T, D = x2.shape
    ka = y_dn.shape[1]
    kb = y_df.shape[1]
    assert ka == kb
    return pl.pallas_call(
        _out_proj_kernel,
        grid=(T // tm, D // tn),
        in_specs=[
            pl.BlockSpec((tm, tn), lambda i, j: (i, j)),
            pl.BlockSpec((tm, ka), lambda i, j: (i, 0)),
            pl.BlockSpec((tm, kb), lambda i, j: (i, 0)),
            pl.BlockSpec((ka, tn), lambda i, j: (0, j)),
            pl.BlockSpec((kb, tn), lambda i, j: (1, j)),
        ],
        out_specs=pl.BlockSpec((tm, tn), lambda i, j: (i, j)),
        out_shape=jax.ShapeDtypeStruct((T, D), F32),
        compiler_params=pltpu.CompilerParams(
            dimension_semantics=("arbitrary", "arbitrary"),
            vmem_limit_bytes=V7X_VMEM_LIMIT),
        name="out_proj",
    )(x2, y_dn, y_df, w_o, w_o)


def _mlp_kernel(h_ref, g_ref, wu_ref, wd_ref, gf_ref, o_ref, u_ref, acc_ref):
    f = pl.program_id(1)

    @pl.when(f == 0)
    def _():
        x = h_ref[...]
        ms = jnp.mean(x * x, axis=-1, keepdims=True)
        u_ref[...] = (x * lax.rsqrt(ms + 1e-6) * g_ref[...]).astype(BF16)
        acc_ref[...] = jnp.zeros_like(acc_ref)

    hid = jnp.maximum(_dot(u_ref[...], wu_ref[...]), 0.0)
    acc_ref[...] += _dot((hid * hid).astype(BF16), wd_ref[...])

    @pl.when(f == pl.num_programs(1) - 1)
    def _():
        y = h_ref[...] + acc_ref[...]
        ms = jnp.mean(y * y, axis=-1, keepdims=True)
        o_ref[...] = y * lax.rsqrt(ms + 1e-6) * gf_ref[...]


def _mlp(h1, gain, w_up, w_down, final_gain, *, tm, tf):
    T, D = h1.shape
    Fdim = w_up.shape[1]
    return pl.pallas_call(
        _mlp_kernel,
        grid=(T // tm, Fdim // tf),
        in_specs=[
            pl.BlockSpec((tm, D), lambda i, f: (i, 0)),
            pl.BlockSpec((1, D), lambda i, f: (0, 0)),
            pl.BlockSpec((D, tf), lambda i, f: (0, f)),
            pl.BlockSpec((tf, D), lambda i, f: (f, 0)),
            pl.BlockSpec((1, D), lambda i, f: (0, 0)),
        ],
        out_specs=pl.BlockSpec((tm, D), lambda i, f: (i, 0), pipeline_mode=pl.Buffered(1)),
        out_shape=jax.ShapeDtypeStruct((T, D), F32),
        scratch_shapes=[pltpu.VMEM((tm, D), BF16), pltpu.VMEM((tm, D), F32)],
        compiler_params=pltpu.CompilerParams(
            dimension_semantics=("arbitrary", "arbitrary"),
            vmem_limit_bytes=V7X_VMEM_LIMIT),
        name="mlp",
    )(h1, gain, w_up, w_down, final_gain)


def _tile(n, pref):
    if n <= pref:
        return n
    t = pref - pref % 128
    while t > 128 and n % t:
        t -= 128
    assert n % t == 0
    return t


def kernel(x, attn_norm, w_in, conv_w, a_log, dt_bias, dn_norm, lambda_q1, lambda_k1,
           lambda_q2, lambda_k2, df_norm, rel_bias, w_o, mlp_norm, w_up, w_down, final_norm):
    B, S, D = x.shape
    depth = attn_norm.shape[0]
    n_dn = a_log.shape[1]
    n_df = rel_bias.shape[1]
    dn_dim = n_dn * DN_HEAD_DIM
    df_dim = n_df * 2 * DF_HEAD_DIM
    T = B * S
    gate0 = 4 * dn_dim
    assert w_in.shape[2] == gate0 + 2 * n_dn + 3 * df_dim and 2 * n_dn <= GATE_COLS

    assert depth == 1
    l = 0
    h = x.reshape(T, D)

    wl = w_in[l]
    dfq0 = gate0 + 2 * n_dn
    w_all = wl.astype(BF16)
    w_b = w_all[:, dfq0:]

    proj, gates, v_t = _in_proj(h, attn_norm[l][None, :], w_all, w_b, conv_w[l],
                                batch=B, dn_dim=dn_dim, df_dim=df_dim, tm=_tile(S, IN_PROJ_ROWS),
                                tn=_tile(math.gcd(dn_dim, df_dim), IN_PROJ_COLS),
                                tv=_tile(S, ATTN_BLOCK),
                                q_scale=DF_HEAD_DIM ** -0.5 * math.log2(math.e))
    proj3 = proj.reshape(B, S, -1)
    gate3 = gates.reshape(B, S, GATE_COLS)

    gate_pad = ((0, 0), (n_dn, GATE_COLS - 2 * n_dn))
    alog_row = jnp.pad(a_log[l][None, :], gate_pad)
    dtb_row = jnp.pad(dt_bias[l][None, :], gate_pad)
    y_dn = _gdn(proj3, gate3, alog_row, dtb_row, dn_norm[l][None, :],
                n_heads=n_dn, blk=_tile(S, GDN_BLOCK))
    lam_init = 0.8 - 0.6 * math.exp(-0.3 * l)
    y_df, w_o_bf, w_up_bf, w_down_bf = _diff_attn(
                      proj3, v_t, rel_bias.reshape(-1),
                      lambda_q1[l][None, :], lambda_k1[l][None, :],
                      lambda_q2[l][None, :], lambda_k2[l][None, :], df_norm[l][:, None],
                      (w_o[l], w_up[l], w_down[l]),
                      n_heads=n_df, col0=gate0, tq=_tile(S, ATTN_QBLOCK), tk=_tile(S, ATTN_BLOCK),
                      lam_init=lam_init)

    h1 = _out_proj(h, y_dn.reshape(T, dn_dim), y_df.reshape(T, df_dim), w_o_bf,
                   tm=_tile(T, 512), tn=_tile(D, 2048))
    out = _mlp(h1, mlp_norm[l][None, :], w_up_bf, w_down_bf, final_norm[None, :],
               tm=_tile(T, 1024), tf=_tile(w_up_bf.shape[1], 512))
    return out.reshape(B, S, D)
```

```python
import functools
import math

import numpy as np
import jax
import jax.numpy as jnp
from jax import lax
from jax.experimental import pallas as pl
from jax.experimental.pallas import tpu as pltpu

F32 = jnp.float32
BF16 = jnp.bfloat16

DN_HEAD_DIM = 128
DF_HEAD_DIM = 128
CONV_WIDTH = 4
CHUNK = 64
NUM_BUCKETS = 32
MAX_DISTANCE = 128
GATE_COLS = 128
IN_PROJ_ROWS = 1024
IN_PROJ_COLS = 1024
IN_PROJ_SUB = 256
GDN_BLOCK = 512
GDN_GROUP = 4
ATTN_SUB = 256
ATTN_BLOCK = 512
ATTN_QBLOCK = 1024

V7X_VMEM_LIMIT = 58 * 1024 * 1024


def _dot(a, b):
    return jnp.dot(a, b, preferred_element_type=F32)


def _dot_nt(a, b):
    return lax.dot_general(a, b, (((1,), (1,)), ((), ())), preferred_element_type=F32)


def _sigmoid(x):
    return 1.0 / (1.0 + jnp.exp(-x))


def _in_proj_kernel(x_ref, g_ref, wa_ref, wb_ref, wg_ref, cw_ref, proj_ref, gate_ref, vt_ref,
                    u_ref, hist_ref, cbuf_ref, *, bounds, per_seq, q_scale):
    i = pl.program_id(0)
    j = pl.program_id(1)
    tm, tn = proj_ref.shape
    b0, b1, b2, b3, b4, b5 = bounds

    @pl.when(j == 0)
    def _():
        x = x_ref[...]
        ms = jnp.mean(x * x, axis=-1, keepdims=True)
        u = (x * lax.rsqrt(ms + 1e-6) * g_ref[...]).astype(BF16)
        u_ref[...] = u
        gate_ref[...] = _dot(u, wg_ref[...])

    n_col_chunks = tn // IN_PROJ_SUB
    first_of_seq = (i % per_seq) == 0

    def chunk_cols(c):
        return slice(c * IN_PROJ_SUB, (c + 1) * IN_PROJ_SUB)

    def raw_cols(w_ref, c):
        cbuf_ref[8:8 + tm, chunk_cols(c)] = _dot(u_ref[...], w_ref[:, chunk_cols(c)])

    def raw(c):
        return cbuf_ref[8:8 + tm, chunk_cols(c)]

    def conv_silu(c):
        cw = cw_ref[:, chunk_cols(c)]
        y = None
        for s in range(CONV_WIDTH):
            tap = CONV_WIDTH - 1 - s
            term = cbuf_ref[8 - s:8 - s + tm, chunk_cols(c)] * cw[tap:tap + 1, :]
            y = term if y is None else y + term
        return y * _sigmoid(y)

    def l2norm_heads(y, scale):
        outs = []
        for h in range(IN_PROJ_SUB // DN_HEAD_DIM):
            yh = y[:, h * DN_HEAD_DIM:(h + 1) * DN_HEAD_DIM]
            outs.append(yh * (lax.rsqrt(jnp.sum(yh * yh, axis=-1, keepdims=True) + 1e-6) * scale))
        return jnp.concatenate(outs, axis=1)

    def silu_cols(c):
        z = raw(c)
        return z * _sigmoid(z)

    def project(w_ref, epilogue, conv=False):
        if conv:
            cbuf_ref[0:8, :] = jnp.where(first_of_seq, 0.0, hist_ref[j])
        raw_cols(w_ref, 0)
        for c in range(n_col_chunks):
            if c + 1 < n_col_chunks:
                raw_cols(w_ref, c + 1)
            proj_ref[:, chunk_cols(c)] = epilogue(c).astype(BF16)
        if conv:
            hist_ref[j] = cbuf_ref[tm:tm + 8, :]

    @pl.when(j < b0)
    def _():
        project(wa_ref, lambda c: l2norm_heads(conv_silu(c), DN_HEAD_DIM ** -0.5), conv=True)

    @pl.when((j >= b0) & (j < b1))
    def _():
        project(wa_ref, lambda c: l2norm_heads(conv_silu(c), 1.0), conv=True)

    @pl.when((j >= b1) & (j < b2))
    def _():
        project(wa_ref, conv_silu, conv=True)

    @pl.when((j >= b2) & (j < b3))
    def _():
        project(wa_ref, silu_cols)

    @pl.when((j >= b3) & (j < b4))
    def _():
        project(wb_ref, lambda c: raw(c) * q_scale)

    @pl.when((j >= b4) & (j < b5))
    def _():
        project(wb_ref, raw)

    @pl.when(j >= b5)
    def _():
        tv = vt_ref.shape[-1]
        raw_cols(wb_ref, 0)
        for c in range(n_col_chunks):
            if c + 1 < n_col_chunks:
                raw_cols(wb_ref, c + 1)
            vt = raw(c).T.astype(BF16)
            for tb in range(vt_ref.shape[0]):
                vt_ref[tb, chunk_cols(c), :] = vt[:, tb * tv:(tb + 1) * tv]


def _in_proj(x2, gain, w_all, w_b, conv_w, *, batch, dn_dim, df_dim, tm, tn, tv, q_scale):
    T, D = x2.shape
    N = 4 * dn_dim + 2 * df_dim
    nv = df_dim
    S = T // batch
    assert w_b.shape[1] == 3 * df_dim
    assert (4 * dn_dim) % GATE_COLS == 0 and w_all.shape[1] >= 4 * dn_dim + GATE_COLS
    assert S % tm == 0 and dn_dim % tn == 0 and df_dim % tn == 0 and tm % tv == 0
    assert tn % DN_HEAD_DIM == 0 and conv_w.shape == (CONV_WIDTH, 3 * dn_dim)
    assert tn % IN_PROJ_SUB == 0 and IN_PROJ_SUB % DN_HEAD_DIM == 0
    n_dn, n_df = dn_dim // tn, df_dim // tn
    bounds = (n_dn, 2 * n_dn, 3 * n_dn, 4 * n_dn, 4 * n_dn + n_df, 4 * n_dn + 2 * n_df)
    n_main = bounds[-1]
    n_conv = bounds[2]
    n_a = bounds[3]
    n_b = n_main - n_a + nv // tn
    per_seq = S // tm
    return pl.pallas_call(
        functools.partial(_in_proj_kernel, bounds=bounds, per_seq=per_seq, q_scale=q_scale),
        grid=(T // tm, n_main + nv // tn),
        in_specs=[
            pl.BlockSpec((tm, D), lambda i, j: (i, 0)),
            pl.BlockSpec((1, D), lambda i, j: (0, 0)),
            pl.BlockSpec((D, tn), lambda i, j: (0, jnp.where(j < n_a, j, 0))),
            pl.BlockSpec((D, tn), lambda i, j: (0, jnp.where(j < n_a, n_b - 1,
                                                             jnp.minimum(j - n_a, n_b - 1)))),
            pl.BlockSpec((D, GATE_COLS), lambda i, j: (0, 4 * dn_dim // GATE_COLS),
                         pipeline_mode=pl.Buffered(1)),
            pl.BlockSpec((CONV_WIDTH, tn), lambda i, j: (0, jnp.minimum(j, n_conv - 1))),
        ],
        out_specs=[
            pl.BlockSpec((tm, tn), lambda i, j: (i, jnp.minimum(j, n_main - 1))),
            pl.BlockSpec((tm, GATE_COLS), lambda i, j: (i, 0)),
            pl.BlockSpec((None, tm // tv, tn, tv),
                         lambda i, j: (i // per_seq, i % per_seq, jnp.maximum(j - n_main, 0), 0)),
        ],
        out_shape=[
            jax.ShapeDtypeStruct((T, N), BF16),
            jax.ShapeDtypeStruct((T, GATE_COLS), F32),
            jax.ShapeDtypeStruct((batch, S // tv, nv, tv), BF16),
        ],
        scratch_shapes=[
            pltpu.VMEM((tm, D), BF16),
            pltpu.VMEM((n_conv, 8, tn), F32),
            pltpu.VMEM((8 + tm, tn), F32),
        ],
        compiler_params=pltpu.CompilerParams(
            dimension_semantics=("arbitrary", "arbitrary"),
            vmem_limit_bytes=V7X_VMEM_LIMIT),
        name="in_proj",
    )(x2, gain, w_all, w_b, w_all, conv_w)


def _gdn_kernel(q_ref, k_ref, v_ref, z_ref, gate_ref, alog_ref, dtb_ref, dnn_ref,
                y_ref, state_ref, *, blk, n_heads):
    dk = DN_HEAD_DIM
    n_chunks = blk // CHUNK

    @pl.when(pl.program_id(1) == 0)
    def _():
        state_ref[...] = jnp.zeros_like(state_ref)

    gate = gate_ref[...]
    beta_all = _sigmoid(gate)
    xs = gate + dtb_ref[...]
    softplus = jnp.maximum(xs, 0.0) + jnp.log(1.0 + jnp.exp(-jnp.abs(xs)))
    g_all = -jnp.exp(alog_ref[...]) * softplus
    pos = lax.broadcasted_iota(jnp.int32, g_all.shape, 0) & (CHUNK - 1)
    gc_all = g_all
    step = 1
    while step < CHUNK:
        gc_all = gc_all + jnp.where(pos >= step, pltpu.roll(gc_all, step, 0), 0.0)
        step *= 2

    ri = lax.broadcasted_iota(jnp.int32, (CHUNK, CHUNK), 0)
    ci = lax.broadcasted_iota(jnp.int32, (CHUNK, CHUNK), 1)
    tril = ri >= ci
    eye = ri == ci
    pair_masks = [(ri > ci) & ((ri >> 1) == (ci >> 1))]
    size = 2
    while size < CHUNK:
        pair_masks.append(((ri // (2 * size)) == (ci // (2 * size))) & ((ri // size) != (ci // size))
                          & (ri > ci))
        size *= 2
    gain = dnn_ref[...]

    def first_stage(c):
        rows = slice(c * CHUNK, (c + 1) * CHUNK)
        items = []
        for h in range(n_heads):
            cols = slice(h * dk, (h + 1) * dk)
            q_bf = q_ref[rows, cols]
            k_bf = k_ref[rows, cols]
            qc = q_bf.astype(F32)
            kc = k_bf.astype(F32)
            vc = v_ref[rows, cols].astype(F32)
            bc = jnp.broadcast_to(beta_all[rows, h:h + 1], (CHUNK, dk))
            gcc = jnp.broadcast_to(gc_all[rows, n_heads + h:n_heads + h + 1], (CHUNK, dk))
            g_last = gcc[CHUNK - 1:CHUNK, :]
            eg = jnp.exp(gcc)
            g_sq = gcc[:, 0:CHUNK]
            g_row = jnp.sum(jnp.where(eye, g_sq, 0.0), axis=0, keepdims=True)
            decay = jnp.exp(jnp.where(tril, g_sq - g_row, -jnp.inf))
            kb = kc * bc
            lhs = jnp.concatenate([kb.astype(BF16), q_bf], axis=0)
            aq = _dot_nt(lhs, k_bf)
            l_mat = aq[:CHUNK] * decay
            items.append(dict(
                h=h, qe=qc * eg, g_last=g_last,
                rhs=jnp.concatenate([kb * eg, vc * bc], axis=1).astype(BF16),
                qk=jnp.where(tril, aq[CHUNK:] * decay, 0.0).astype(BF16),
                kd_t=(kc * jnp.exp(g_last - gcc)).T.astype(BF16),
                inv=jnp.where(eye, 1.0, 0.0) - jnp.where(pair_masks[0], l_mat, 0.0),
                off=[jnp.where(m, l_mat, 0.0).astype(BF16) for m in pair_masks[1:]]))
        return items

    def matrix_stages(items):
        for level in range(len(pair_masks) - 1):
            for s in items:
                s["x"] = _dot(s["off"][level], s["inv"].astype(BF16)).astype(BF16)
            for s in items:
                s["inv"] = s["inv"] - _dot(s["inv"].astype(BF16), s["x"])
        for s in items:
            s["wu"] = _dot(s["inv"].astype(BF16), s["rhs"]).astype(BF16)
        for s in items:
            s["gr"] = _dot(s["kd_t"], s["wu"])
            qw = _dot(s["qk"], s["wu"])
            s["q_eff"] = (s["qe"] - qw[:, :dk]).astype(BF16)
            s["p_loc"] = qw[:, dk:]

    def state_stage(c, items):
        rows = slice(c * CHUNK, (c + 1) * CHUNK)
        for s in items:
            h = s["h"]
            state = state_ref[h]
            s_bf = state.astype(BF16)
            o = _dot(s["q_eff"], s_bf) + s["p_loc"]
            state_ref[h] = (state * jnp.exp(s["g_last"])
                            - _dot(s["gr"][:, :dk].astype(BF16), s_bf) + s["gr"][:, dk:])
            zs = z_ref[rows, h * dk:(h + 1) * dk].astype(F32)
            o = o * lax.rsqrt(jnp.mean(o * o, axis=-1, keepdims=True) + 1e-6) * gain
            y_ref[rows, h * dk:(h + 1) * dk] = (o * zs).astype(y_ref.dtype)

    groups = [list(range(g, min(g + GDN_GROUP, n_chunks))) for g in range(0, n_chunks, GDN_GROUP)]
    cur = [first_stage(c) for c in groups[0]]
    for gi, chunk_ids in enumerate(groups):
        nxt = [first_stage(c) for c in groups[gi + 1]] if gi + 1 < len(groups) else None
        matrix_stages([s for items in cur for s in items])
        for c, items in zip(chunk_ids, cur):
            state_stage(c, items)
        cur = nxt


def _gdn(proj3, gate3, alog_row, dtb_row, dn_norm, *, n_heads, blk):
    B, S, _ = proj3.shape
    dk = DN_HEAD_DIM
    dn = n_heads * dk

    def group(idx):
        return pl.BlockSpec((None, blk, dn), lambda b, t: (b, t, idx))

    def whole(arr):
        return pl.BlockSpec(arr.shape, lambda b, t: (0,) * arr.ndim)

    return pl.pallas_call(
        functools.partial(_gdn_kernel, blk=blk, n_heads=n_heads),
        grid=(B, S // blk),
        in_specs=[
            group(0), group(1), group(2), group(3),
            pl.BlockSpec((None, blk, GATE_COLS), lambda b, t: (b, t, 0)),
            whole(alog_row), whole(dtb_row), whole(dn_norm),
        ],
        out_specs=pl.BlockSpec((None, blk, dn), lambda b, t: (b, t, 0)),
        out_shape=jax.ShapeDtypeStruct((B, S, dn), BF16),
        scratch_shapes=[pltpu.VMEM((n_heads, dk, dk), F32)],
        compiler_params=pltpu.CompilerParams(
            dimension_semantics=("arbitrary", "arbitrary"),
            vmem_limit_bytes=V7X_VMEM_LIMIT),
        name="gdn",
    )(proj3, proj3, proj3, proj3, gate3, alog_row, dtb_row, dn_norm)


def _t5_bucket_starts():
    max_exact = NUM_BUCKETS // 2
    n = np.arange(0, MAX_DISTANCE + 1)
    nf = np.maximum(n, 1).astype(np.float32)
    large = max_exact + (np.log(nf / max_exact) / math.log(MAX_DISTANCE / max_exact)
                         * (NUM_BUCKETS - max_exact)).astype(np.int32)
    bucket = np.where(n < max_exact, n, np.minimum(large, NUM_BUCKETS - 1))
    assert bucket[MAX_DISTANCE] == NUM_BUCKETS - 1 and np.all(np.diff(bucket) >= 0)
    starts = [(0, int(bucket[0]))]
    for d in range(1, MAX_DISTANCE + 1):
        if bucket[d] != bucket[d - 1]:
            starts.append((d, int(bucket[d])))
    return starts


def _diff_attn_kernel(rb_ref, q_ref, k_ref, vt_ref, lq1_ref, lk1_ref, lq2_ref, lk2_ref, dfn_ref,
                      c0_ref, c1_ref, c2_ref, y_ref, o0_ref, o1_ref, o2_ref,
                      bias_ref, s_ref, p_ref, m_ref, l_ref, acc_ref,
                      *, tq, tk, n_heads, lam_init):
    h = pl.program_id(1)
    qi = pl.program_id(2)
    d = DF_HEAD_DIM

    for c_ref, o_ref in ((c0_ref, o0_ref), (c1_ref, o1_ref), (c2_ref, o2_ref)):
        o_ref[...] = c_ref[...].astype(o_ref.dtype)

    sub = ATTN_SUB
    nsq = tq // sub
    nsk = tk // sub
    log2e = math.log2(math.e)

    @pl.when(qi == 0)
    def _():
        keys = lax.broadcasted_iota(jnp.int32, (sub, sub), 0)
        qrys = lax.broadcasted_iota(jnp.int32, (sub, sub), 1)
        starts = _t5_bucket_starts()
        for idx in range(2):
            dist = qrys - keys + idx * sub
            tile = jnp.full((sub, sub), rb_ref[starts[0][1] * n_heads + h], F32)
            for first, bucket in starts[1:]:
                tile = jnp.where(dist >= first, rb_ref[bucket * n_heads + h], tile)
            tile = tile * log2e
            if idx == 0:
                tile = jnp.where(dist >= 0, tile, -jnp.inf)
            bias_ref[idx] = tile
        bias_ref[2] = jnp.full((sub, sub), rb_ref[(NUM_BUCKETS - 1) * n_heads + h] * log2e, F32)
        bias_ref[3] = jnp.full((sub, sub), -jnp.inf, F32)

    m_ref[...] = jnp.full(m_ref.shape, -jnp.inf, F32)
    l_ref[...] = jnp.zeros_like(l_ref)
    acc_ref[...] = jnp.zeros_like(acc_ref)

    chains = [(qb, m) for qb in range(nsq) for m in range(2)]

    far_shift = rb_ref[(NUM_BUCKETS - 1) * n_heads + h] * log2e

    def produce(j, c):
        qb, m = chains[c]
        k0 = pl.multiple_of(j * tk, tk)
        s_ref[c] = _dot_nt(k_ref[pl.ds(k0, tk), m * d:(m + 1) * d],
                           q_ref[qb * sub:(qb + 1) * sub, m * d:(m + 1) * d])

    def softmax(j, c, far):
        qb, m = chains[c]
        if far:
            s = s_ref[c]
            shift = far_shift
        else:
            parts = []
            for kb in range(nsk):
                off = (qi * nsq + qb) - (j * nsk + kb)
                idx = jnp.where(off < 0, 3, jnp.minimum(off, 2))
                parts.append(s_ref[c, kb * sub:(kb + 1) * sub, :] + bias_ref[idx])
            s = jnp.concatenate(parts, axis=0)
            shift = 0.0
        cols = slice(qb * sub, (qb + 1) * sub)
        m_prev = m_ref[m, :, cols]
        m_new = jnp.maximum(m_prev, jnp.max(s, axis=0, keepdims=True) + shift)
        alpha = jnp.exp2(m_prev - m_new)
        pr = jnp.exp2(s - (m_new - shift))
        l_ref[m, :, cols] = alpha * l_ref[m, :, cols] + jnp.sum(pr, axis=0, keepdims=True)
        acc_ref[m, :, cols] = alpha * acc_ref[m, :, cols]
        m_ref[m, :, cols] = m_new
        return pr.astype(BF16)

    def add_values(j, c, p):
        qb, m = chains[c]
        cols = slice(qb * sub, (qb + 1) * sub)
        acc_ref[m, :, cols] += _dot(vt_ref[j], p)

    n_chains = len(chains)
    for c in range(n_chains):
        produce(0, c)
    p_ref[...] = softmax(0, 0, False)

    def trip(j, far):
        add_values(j, 0, p_ref[...])
        produce(j + 1, 0)
        for c in range(1, n_chains):
            add_values(j, c, softmax(j, c, far))
            produce(j + 1, c)
        p_ref[...] = softmax(j + 1, 0, far)

    last = (qi + 1) * (tq // tk) - 1
    n_far = jnp.maximum((qi * nsq - 1) // nsk - 1, 0)
    lax.fori_loop(0, n_far, lambda j, carry: (trip(j, True), carry)[1], 0)
    lax.fori_loop(n_far, last, lambda j, carry: (trip(j, False), carry)[1], 0)
    live = [c for c in range(n_chains) if chains[c][0] >= nsq - nsk]
    for c in live:
        add_values(last, c, p_ref[...] if c == 0 else softmax(last, c, False))

    lam = (jnp.exp(jnp.sum(lq1_ref[...] * lk1_ref[...], axis=-1, keepdims=True))
           - jnp.exp(jnp.sum(lq2_ref[...] * lk2_ref[...], axis=-1, keepdims=True))
           + lam_init)
    o = acc_ref[0] * (1.0 / l_ref[0]) - acc_ref[1] * (lam / l_ref[1])
    o = o * lax.rsqrt(jnp.mean(o * o, axis=0, keepdims=True) + 1e-5) * dfn_ref[...]
    y_ref[...] = (o * (1.0 - lam_init)).T.astype(y_ref.dtype)


def _diff_attn(proj3, v_t, rel_bias, lq1, lk1, lq2, lk2, df_norm_col, to_cast, *, n_heads, col0,
               tq, tk, lam_init):
    B, S, _ = proj3.shape
    d2 = 2 * DF_HEAD_DIM
    assert tq % tk == 0 and tk % ATTN_SUB == 0 and ATTN_SUB >= MAX_DISTANCE and S % tq == 0
    nq = S // tq
    n_chains = 2 * (tq // ATTN_SUB)
    cb = col0 // d2
    vec = lambda n: pl.BlockSpec((1, n), lambda b, h, i: (0, 0))
    n_steps = B * n_heads * nq

    def slab(w):
        rows = w.shape[0] // n_steps
        assert w.shape[0] % n_steps == 0 and rows % 16 == 0
        return pl.BlockSpec((rows, w.shape[1]), lambda b, h, i: ((b * n_heads + h) * nq + i, 0))

    return pl.pallas_call(
        functools.partial(_diff_attn_kernel, tq=tq, tk=tk, n_heads=n_heads, lam_init=lam_init),
        grid=(B, n_heads, nq),
        in_specs=[
            pl.BlockSpec(memory_space=pltpu.SMEM),
            pl.BlockSpec((None, tq, d2), lambda b, h, i: (b, i, cb + h)),
            pl.BlockSpec((None, S, d2), lambda b, h, i: (b, 0, cb + n_heads + h)),
            pl.BlockSpec((None, S // tk, d2, tk), lambda b, h, i: (b, 0, h, 0)),
            vec(DF_HEAD_DIM), vec(DF_HEAD_DIM), vec(DF_HEAD_DIM), vec(DF_HEAD_DIM),
            pl.BlockSpec((d2, 1), lambda b, h, i: (0, 0)),
        ] + [slab(w) for w in to_cast],
        out_specs=[pl.BlockSpec((None, tq, d2), lambda b, h, i: (b, i, h))]
        + [slab(w) for w in to_cast],
        out_shape=[jax.ShapeDtypeStruct((B, S, n_heads * d2), BF16)]
        + [jax.ShapeDtypeStruct(w.shape, BF16) for w in to_cast],
        scratch_shapes=[
            pltpu.VMEM((4, ATTN_SUB, ATTN_SUB), F32),
            pltpu.VMEM((n_chains, tk, ATTN_SUB), F32),
            pltpu.VMEM((tk, ATTN_SUB), BF16),
            pltpu.VMEM((2, 1, tq), F32),
            pltpu.VMEM((2, 1, tq), F32),
            pltpu.VMEM((2, d2, tq), F32),
        ],
        compiler_params=pltpu.CompilerParams(
            dimension_semantics=("arbitrary", "arbitrary", "arbitrary"),
            vmem_limit_bytes=V7X_VMEM_LIMIT),
        name="diff_attn",
    )(rel_bias, proj3, proj3, v_t, lq1, lk1, lq2, lk2, df_norm_col, *to_cast)


def _out_proj_kernel(x_ref, ya_ref, yb_ref, wa_ref, wb_ref, h_ref):
    h_ref[...] = x_ref[...] + _dot(ya_ref[...], wa_ref[...]) + _dot(yb_ref[...], wb_ref[...])


def _out_proj(x2, y_dn, y_df, w_o, *, tm, tn):
    T, D = x2.shape
    ka = y_dn.shape[1]
    kb = y_df.shape[1]
    assert ka == kb
    return pl.pallas_call(
        _out_proj_kernel,
        grid=(T // tm, D // tn),
        in_specs=[
            pl.BlockSpec((tm, tn), lambda i, j: (i, j)),
            pl.BlockSpec((tm, ka), lambda i, j: (i, 0)),
            pl.BlockSpec((tm, kb), lambda i, j: (i, 0)),
            pl.BlockSpec((ka, tn), lambda i, j: (0, j)),
            pl.BlockSpec((kb, tn), lambda i, j: (1, j)),
        ],
        out_specs=pl.BlockSpec((tm, tn), lambda i, j: (i, j)),
        out_shape=jax.ShapeDtypeStruct((T, D), F32),
        compiler_params=pltpu.CompilerParams(
            dimension_semantics=("arbitrary", "arbitrary"),
            vmem_limit_bytes=V7X_VMEM_LIMIT),
        name="out_proj",
    )(x2, y_dn, y_df, w_o, w_o)


def _mlp_kernel(h_ref, g_ref, wu_ref, wd_ref, gf_ref, o_ref, u_ref, acc_ref):
    f = pl.program_id(1)

    @pl.when(f == 0)
    def _():
        x = h_ref[...]
        ms = jnp.mean(x * x, axis=-1, keepdims=True)
        u_ref[...] = (x * lax.rsqrt(ms + 1e-6) * g_ref[...]).astype(BF16)
        acc_ref[...] = jnp.zeros_like(acc_ref)

    hid = jnp.maximum(_dot(u_ref[...], wu_ref[...]), 0.0)
    acc_ref[...] += _dot((hid * hid).astype(BF16), wd_ref[...])

    @pl.when(f == pl.num_programs(1) - 1)
    def _():
        y = h_ref[...] + acc_ref[...]
        ms = jnp.mean(y * y, axis=-1, keepdims=True)
        o_ref[...] = y * lax.rsqrt(ms + 1e-6) * gf_ref[...]


def _mlp(h1, gain, w_up, w_down, final_gain, *, tm, tf):
    T, D = h1.shape
    Fdim = w_up.shape[1]
    return pl.pallas_call(
        _mlp_kernel,
        grid=(T // tm, Fdim // tf),
        in_specs=[
            pl.BlockSpec((tm, D), lambda i, f: (i, 0)),
            pl.BlockSpec((1, D), lambda i, f: (0, 0)),
            pl.BlockSpec((D, tf), lambda i, f: (0, f)),
            pl.BlockSpec((tf, D), lambda i, f: (f, 0)),
            pl.BlockSpec((1, D), lambda i, f: (0, 0)),
        ],
        out_specs=pl.BlockSpec((tm, D), lambda i, f: (i, 0)),
        out_shape=jax.ShapeDtypeStruct((T, D), F32),
        scratch_shapes=[pltpu.VMEM((tm, D), BF16), pltpu.VMEM((tm, D), F32)],
        compiler_params=pltpu.CompilerParams(
            dimension_semantics=("arbitrary", "arbitrary"),
            vmem_limit_bytes=V7X_VMEM_LIMIT),
        name="mlp",
    )(h1, gain, w_up, w_down, final_gain)


def _tile(n, pref):
    if n <= pref:
        return n
    t = pref - pref % 128
    while t > 128 and n % t:
        t -= 128
    assert n % t == 0
    return t


def kernel(x, attn_norm, w_in, conv_w, a_log, dt_bias, dn_norm, lambda_q1, lambda_k1,
           lambda_q2, lambda_k2, df_norm, rel_bias, w_o, mlp_norm, w_up, w_down, final_norm):
    B, S, D = x.shape
    depth = attn_norm.shape[0]
    n_dn = a_log.shape[1]
    n_df = rel_bias.shape[1]
    dn_dim = n_dn * DN_HEAD_DIM
    df_dim = n_df * 2 * DF_HEAD_DIM
    T = B * S
    gate0 = 4 * dn_dim
    assert w_in.shape[2] == gate0 + 2 * n_dn + 3 * df_dim and 2 * n_dn <= GATE_COLS

    assert depth == 1
    l = 0
    h = x.reshape(T, D)

    wl = w_in[l]
    dfq0 = gate0 + 2 * n_dn
    w_all = wl.astype(BF16)
    w_b = w_all[:, dfq0:]

    proj, gates, v_t = _in_proj(h, attn_norm[l][None, :], w_all, w_b, conv_w[l],
                                batch=B, dn_dim=dn_dim, df_dim=df_dim, tm=_tile(S, IN_PROJ_ROWS),
                                tn=_tile(math.gcd(dn_dim, df_dim), IN_PROJ_COLS),
                                tv=_tile(S, ATTN_BLOCK),
                                q_scale=DF_HEAD_DIM ** -0.5 * math.log2(math.e))
    proj3 = proj.reshape(B, S, -1)
    gate3 = gates.reshape(B, S, GATE_COLS)

    gate_pad = ((0, 0), (n_dn, GATE_COLS - 2 * n_dn))
    alog_row = jnp.pad(a_log[l][None, :], gate_pad)
    dtb_row = jnp.pad(dt_bias[l][None, :], gate_pad)
    y_dn = _gdn(proj3, gate3, alog_row, dtb_row, dn_norm[l][None, :],
                n_heads=n_dn, blk=_tile(S, GDN_BLOCK))
    lam_init = 0.8 - 0.6 * math.exp(-0.3 * l)
    y_df, w_o_bf, w_up_bf, w_down_bf = _diff_attn(
                      proj3, v_t, rel_bias.reshape(-1),
                      lambda_q1[l][None, :], lambda_k1[l][None, :],
                      lambda_q2[l][None, :], lambda_k2[l][None, :], df_norm[l][:, None],
                      (w_o[l], w_up[l], w_down[l]),
                      n_heads=n_df, col0=gate0, tq=_tile(S, ATTN_QBLOCK), tk=_tile(S, ATTN_BLOCK),
                      lam_init=lam_init)

    h1 = _out_proj(h, y_dn.reshape(T, dn_dim), y_df.reshape(T, df_dim), w_o_bf,
                   tm=_tile(T, 512), tn=_tile(D, 2048))
    out = _mlp(h1, mlp_norm[l][None, :], w_up_bf, w_down_bf, final_norm[None, :],
               tm=_tile(T, 512), tf=_tile(w_up_bf.shape[1], 1024))
    return out.reshape(B, S, D)
```

```python
import functools
import math

import numpy as np
import jax
import jax.numpy as jnp
from jax import lax
from jax.experimental import pallas as pl
from jax.experimental.pallas import tpu as pltpu

F32 = jnp.float32
BF16 = jnp.bfloat16

DN_HEAD_DIM = 128
DF_HEAD_DIM = 128
CONV_WIDTH = 4
CHUNK = 64
NUM_BUCKETS = 32
MAX_DISTANCE = 128
GATE_COLS = 128
IN_PROJ_ROWS = 1024
IN_PROJ_COLS = 1024
IN_PROJ_SUB = 256
GDN_BLOCK = 1024
GDN_GROUP = 4
ATTN_SUB = 256
ATTN_BLOCK = 512
ATTN_QBLOCK = 1024

V7X_VMEM_LIMIT = 58 * 1024 * 1024


def _dot(a, b):
    return jnp.dot(a, b, preferred_element_type=F32)


def _dot_nt(a, b):
    return lax.dot_general(a, b, (((1,), (1,)), ((), ())), preferred_element_type=F32)


def _sigmoid(x):
    return 1.0 / (1.0 + jnp.exp(-x))


def _in_proj_kernel(x_ref, g_ref, wa_ref, wb_ref, wg_ref, cw_ref, proj_ref, gate_ref, vt_ref,
                    u_ref, hist_ref, cbuf_ref, *, bounds, per_seq, q_scale):
    i = pl.program_id(0)
    j = pl.program_id(1)
    tm, tn = proj_ref.shape
    b0, b1, b2, b3, b4, b5 = bounds

    @pl.when(j == 0)
    def _():
        x = x_ref[...]
        ms = jnp.mean(x * x, axis=-1, keepdims=True)
        u = (x * lax.rsqrt(ms + 1e-6) * g_ref[...]).astype(BF16)
        u_ref[...] = u
        gate_ref[...] = _dot(u, wg_ref[...])

    n_col_chunks = tn // IN_PROJ_SUB
    first_of_seq = (i % per_seq) == 0

    def chunk_cols(c):
        return slice(c * IN_PROJ_SUB, (c + 1) * IN_PROJ_SUB)

    def raw_cols(w_ref, c):
        cbuf_ref[8:8 + tm, chunk_cols(c)] = _dot(u_ref[...], w_ref[:, chunk_cols(c)])

    def raw(c):
        return cbuf_ref[8:8 + tm, chunk_cols(c)]

    def conv_silu(c):
        cw = cw_ref[:, chunk_cols(c)]
        y = None
        for s in range(CONV_WIDTH):
            tap = CONV_WIDTH - 1 - s
            term = cbuf_ref[8 - s:8 - s + tm, chunk_cols(c)] * cw[tap:tap + 1, :]
            y = term if y is None else y + term
        return y * _sigmoid(y)

    def l2norm_heads(y, scale):
        outs = []
        for h in range(IN_PROJ_SUB // DN_HEAD_DIM):
            yh = y[:, h * DN_HEAD_DIM:(h + 1) * DN_HEAD_DIM]
            outs.append(yh * (lax.rsqrt(jnp.sum(yh * yh, axis=-1, keepdims=True) + 1e-6) * scale))
        return jnp.concatenate(outs, axis=1)

    def silu_cols(c):
        z = raw(c)
        return z * _sigmoid(z)

    def project(w_ref, epilogue, conv=False):
        if conv:
            cbuf_ref[0:8, :] = jnp.where(first_of_seq, 0.0, hist_ref[j])
        raw_cols(w_ref, 0)
        for c in range(n_col_chunks):
            if c + 1 < n_col_chunks:
                raw_cols(w_ref, c + 1)
            proj_ref[:, chunk_cols(c)] = epilogue(c).astype(BF16)
        if conv:
            hist_ref[j] = cbuf_ref[tm:tm + 8, :]

    @pl.when(j < b0)
    def _():
        project(wa_ref, lambda c: l2norm_heads(conv_silu(c), DN_HEAD_DIM ** -0.5), conv=True)

    @pl.when((j >= b0) & (j < b1))
    def _():
        project(wa_ref, lambda c: l2norm_heads(conv_silu(c), 1.0), conv=True)

    @pl.when((j >= b1) & (j < b2))
    def _():
        project(wa_ref, conv_silu, conv=True)

    @pl.when((j >= b2) & (j < b3))
    def _():
        project(wa_ref, silu_cols)

    @pl.when((j >= b3) & (j < b4))
    def _():
        project(wb_ref, lambda c: raw(c) * q_scale)

    @pl.when((j >= b4) & (j < b5))
    def _():
        project(wb_ref, raw)

    @pl.when(j >= b5)
    def _():
        tv = vt_ref.shape[-1]
        raw_cols(wb_ref, 0)
        for c in range(n_col_chunks):
            if c + 1 < n_col_chunks:
                raw_cols(wb_ref, c + 1)
            vt = raw(c).T.astype(BF16)
            for tb in range(vt_ref.shape[0]):
                vt_ref[tb, chunk_cols(c), :] = vt[:, tb * tv:(tb + 1) * tv]


def _in_proj(x2, gain, w_all, w_b, conv_w, *, batch, dn_dim, df_dim, tm, tn, tv, q_scale):
    T, D = x2.shape
    N = 4 * dn_dim + 2 * df_dim
    nv = df_dim
    S = T // batch
    assert w_b.shape[1] == 3 * df_dim
    assert (4 * dn_dim) % GATE_COLS == 0 and w_all.shape[1] >= 4 * dn_dim + GATE_COLS
    assert S % tm == 0 and dn_dim % tn == 0 and df_dim % tn == 0 and tm % tv == 0
    assert tn % DN_HEAD_DIM == 0 and conv_w.shape == (CONV_WIDTH, 3 * dn_dim)
    assert tn % IN_PROJ_SUB == 0 and IN_PROJ_SUB % DN_HEAD_DIM == 0
    n_dn, n_df = dn_dim // tn, df_dim // tn
    bounds = (n_dn, 2 * n_dn, 3 * n_dn, 4 * n_dn, 4 * n_dn + n_df, 4 * n_dn + 2 * n_df)
    n_main = bounds[-1]
    n_conv = bounds[2]
    n_a = bounds[3]
    n_b = n_main - n_a + nv // tn
    per_seq = S // tm
    return pl.pallas_call(
        functools.partial(_in_proj_kernel, bounds=bounds, per_seq=per_seq, q_scale=q_scale),
        grid=(T // tm, n_main + nv // tn),
        in_specs=[
            pl.BlockSpec((tm, D), lambda i, j: (i, 0)),
            pl.BlockSpec((1, D), lambda i, j: (0, 0)),
            pl.BlockSpec((D, tn), lambda i, j: (0, jnp.where(j < n_a, j, 0))),
            pl.BlockSpec((D, tn), lambda i, j: (0, jnp.where(j < n_a, n_b - 1,
                                                             jnp.minimum(j - n_a, n_b - 1)))),
            pl.BlockSpec((D, GATE_COLS), lambda i, j: (0, 4 * dn_dim // GATE_COLS),
                         pipeline_mode=pl.Buffered(1)),
            pl.BlockSpec((CONV_WIDTH, tn), lambda i, j: (0, jnp.minimum(j, n_conv - 1))),
        ],
        out_specs=[
            pl.BlockSpec((tm, tn), lambda i, j: (i, jnp.minimum(j, n_main - 1))),
            pl.BlockSpec((tm, GATE_COLS), lambda i, j: (i, 0)),
            pl.BlockSpec((None, tm // tv, tn, tv),
                         lambda i, j: (i // per_seq, i % per_seq, jnp.maximum(j - n_main, 0), 0)),
        ],
        out_shape=[
            jax.ShapeDtypeStruct((T, N), BF16),
            jax.ShapeDtypeStruct((T, GATE_COLS), F32),
            jax.ShapeDtypeStruct((batch, S // tv, nv, tv), BF16),
        ],
        scratch_shapes=[
            pltpu.VMEM((tm, D), BF16),
            pltpu.VMEM((n_conv, 8, tn), F32),
            pltpu.VMEM((8 + tm, tn), F32),
        ],
        compiler_params=pltpu.CompilerParams(
            dimension_semantics=("arbitrary", "arbitrary"),
            vmem_limit_bytes=V7X_VMEM_LIMIT),
        name="in_proj",
    )(x2, gain, w_all, w_b, w_all, conv_w)


def _gdn_kernel(q_ref, k_ref, v_ref, z_ref, gate_ref, alog_ref, dtb_ref, dnn_ref,
                y_ref, state_ref, *, blk, n_heads):
    dk = DN_HEAD_DIM
    n_chunks = blk // CHUNK

    @pl.when(pl.program_id(1) == 0)
    def _():
        state_ref[...] = jnp.zeros_like(state_ref)

    gate = gate_ref[...]
    beta_all = _sigmoid(gate)
    xs = gate + dtb_ref[...]
    softplus = jnp.maximum(xs, 0.0) + jnp.log(1.0 + jnp.exp(-jnp.abs(xs)))
    g_all = -jnp.exp(alog_ref[...]) * softplus
    pos = lax.broadcasted_iota(jnp.int32, g_all.shape, 0) & (CHUNK - 1)
    gc_all = g_all
    step = 1
    while step < CHUNK:
        gc_all = gc_all + jnp.where(pos >= step, pltpu.roll(gc_all, step, 0), 0.0)
        step *= 2

    ri = lax.broadcasted_iota(jnp.int32, (CHUNK, CHUNK), 0)
    ci = lax.broadcasted_iota(jnp.int32, (CHUNK, CHUNK), 1)
    tril = ri >= ci
    eye = ri == ci
    pair_masks = [(ri > ci) & ((ri >> 1) == (ci >> 1))]
    size = 2
    while size < CHUNK:
        pair_masks.append(((ri // (2 * size)) == (ci // (2 * size))) & ((ri // size) != (ci // size))
                          & (ri > ci))
        size *= 2
    gain = dnn_ref[...]

    def first_stage(c):
        rows = slice(c * CHUNK, (c + 1) * CHUNK)
        items = []
        for h in range(n_heads):
            cols = slice(h * dk, (h + 1) * dk)
            q_bf = q_ref[rows, cols]
            k_bf = k_ref[rows, cols]
            qc = q_bf.astype(F32)
            kc = k_bf.astype(F32)
            vc = v_ref[rows, cols].astype(F32)
            bc = jnp.broadcast_to(beta_all[rows, h:h + 1], (CHUNK, dk))
            gcc = jnp.broadcast_to(gc_all[rows, n_heads + h:n_heads + h + 1], (CHUNK, dk))
            g_last = gcc[CHUNK - 1:CHUNK, :]
            eg = jnp.exp(gcc)
            g_sq = gcc[:, 0:CHUNK]
            g_row = jnp.sum(jnp.where(eye, g_sq, 0.0), axis=0, keepdims=True)
            decay = jnp.exp(jnp.where(tril, g_sq - g_row, -jnp.inf))
            kb = kc * bc
            lhs = jnp.concatenate([kb.astype(BF16), q_bf], axis=0)
            aq = _dot_nt(lhs, k_bf)
            l_mat = aq[:CHUNK] * decay
            items.append(dict(
                h=h, qe=qc * eg, g_last=g_last,
                rhs=jnp.concatenate([kb * eg, vc * bc], axis=1).astype(BF16),
                qk=jnp.where(tril, aq[CHUNK:] * decay, 0.0).astype(BF16),
                kd_t=(kc * jnp.exp(g_last - gcc)).T.astype(BF16),
                inv=jnp.where(eye, 1.0, 0.0) - jnp.where(pair_masks[0], l_mat, 0.0),
                off=[jnp.where(m, l_mat, 0.0).astype(BF16) for m in pair_masks[1:]]))
        return items

    def matrix_stages(items):
        for level in range(len(pair_masks) - 1):
            for s in items:
                s["x"] = _dot(s["off"][level], s["inv"].astype(BF16)).astype(BF16)
            for s in items:
                s["inv"] = s["inv"] - _dot(s["inv"].astype(BF16), s["x"])
        for s in items:
            s["wu"] = _dot(s["inv"].astype(BF16), s["rhs"]).astype(BF16)
        for s in items:
            s["gr"] = _dot(s["kd_t"], s["wu"])
            qw = _dot(s["qk"], s["wu"])
            s["q_eff"] = (s["qe"] - qw[:, :dk]).astype(BF16)
            s["p_loc"] = qw[:, dk:]

    def state_stage(c, items):
        rows = slice(c * CHUNK, (c + 1) * CHUNK)
        for s in items:
            h = s["h"]
            state = state_ref[h]
            s_bf = state.astype(BF16)
            o = _dot(s["q_eff"], s_bf) + s["p_loc"]
            state_ref[h] = (state * jnp.exp(s["g_last"])
                            - _dot(s["gr"][:, :dk].astype(BF16), s_bf) + s["gr"][:, dk:])
            zs = z_ref[rows, h * dk:(h + 1) * dk].astype(F32)
            o = o * lax.rsqrt(jnp.mean(o * o, axis=-1, keepdims=True) + 1e-6) * gain
            y_ref[rows, h * dk:(h + 1) * dk] = (o * zs).astype(y_ref.dtype)

    groups = [list(range(g, min(g + GDN_GROUP, n_chunks))) for g in range(0, n_chunks, GDN_GROUP)]
    cur = [first_stage(c) for c in groups[0]]
    for gi, chunk_ids in enumerate(groups):
        nxt = [first_stage(c) for c in groups[gi + 1]] if gi + 1 < len(groups) else None
        matrix_stages([s for items in cur for s in items])
        for c, items in zip(chunk_ids, cur):
            state_stage(c, items)
        cur = nxt


def _gdn(proj3, gate3, alog_row, dtb_row, dn_norm, *, n_heads, blk):
    B, S, _ = proj3.shape
    dk = DN_HEAD_DIM
    dn = n_heads * dk

    def group(idx):
        return pl.BlockSpec((None, blk, dn), lambda b, t: (b, t, idx))

    def whole(arr):
        return pl.BlockSpec(arr.shape, lambda b, t: (0,) * arr.ndim)

    return pl.pallas_call(
        functools.partial(_gdn_kernel, blk=blk, n_heads=n_heads),
        grid=(B, S // blk),
        in_specs=[
            group(0), group(1), group(2), group(3),
            pl.BlockSpec((None, blk, GATE_COLS), lambda b, t: (b, t, 0)),
            whole(alog_row), whole(dtb_row), whole(dn_norm),
        ],
        out_specs=pl.BlockSpec((None, blk, dn), lambda b, t: (b, t, 0)),
        out_shape=jax.ShapeDtypeStruct((B, S, dn), BF16),
        scratch_shapes=[pltpu.VMEM((n_heads, dk, dk), F32)],
        compiler_params=pltpu.CompilerParams(
            dimension_semantics=("arbitrary", "arbitrary"),
            vmem_limit_bytes=V7X_VMEM_LIMIT),
        name="gdn",
    )(proj3, proj3, proj3, proj3, gate3, alog_row, dtb_row, dn_norm)


def _t5_bucket_starts():
    max_exact = NUM_BUCKETS // 2
    n = np.arange(0, MAX_DISTANCE + 1)
    nf = np.maximum(n, 1).astype(np.float32)
    large = max_exact + (np.log(nf / max_exact) / math.log(MAX_DISTANCE / max_exact)
                         * (NUM_BUCKETS - max_exact)).astype(np.int32)
    bucket = np.where(n < max_exact, n, np.minimum(large, NUM_BUCKETS - 1))
    assert bucket[MAX_DISTANCE] == NUM_BUCKETS - 1 and np.all(np.diff(bucket) >= 0)
    starts = [(0, int(bucket[0]))]
    for d in range(1, MAX_DISTANCE + 1):
        if bucket[d] != bucket[d - 1]:
            starts.append((d, int(bucket[d])))
    return starts


def _diff_attn_kernel(rb_ref, q_ref, k_ref, vt_ref, lq1_ref, lk1_ref, lq2_ref, lk2_ref, dfn_ref,
                      c0_ref, c1_ref, c2_ref, y_ref, o0_ref, o1_ref, o2_ref,
                      bias_ref, s_ref, p_ref, m_ref, l_ref, acc_ref,
                      *, tq, tk, n_heads, lam_init):
    h = pl.program_id(1)
    qi = pl.program_id(2)
    d = DF_HEAD_DIM

    for c_ref, o_ref in ((c0_ref, o0_ref), (c1_ref, o1_ref), (c2_ref, o2_ref)):
        o_ref[...] = c_ref[...].astype(o_ref.dtype)

    sub = ATTN_SUB
    nsq = tq // sub
    nsk = tk // sub
    log2e = math.log2(math.e)

    @pl.when(qi == 0)
    def _():
        keys = lax.broadcasted_iota(jnp.int32, (sub, sub), 0)
        qrys = lax.broadcasted_iota(jnp.int32, (sub, sub), 1)
        starts = _t5_bucket_starts()
        for idx in range(2):
            dist = qrys - keys + idx * sub
            tile = jnp.full((sub, sub), rb_ref[starts[0][1] * n_heads + h], F32)
            for first, bucket in starts[1:]:
                tile = jnp.where(dist >= first, rb_ref[bucket * n_heads + h], tile)
            tile = tile * log2e
            if idx == 0:
                tile = jnp.where(dist >= 0, tile, -jnp.inf)
            bias_ref[idx] = tile
        bias_ref[2] = jnp.full((sub, sub), rb_ref[(NUM_BUCKETS - 1) * n_heads + h] * log2e, F32)
        bias_ref[3] = jnp.full((sub, sub), -jnp.inf, F32)

    m_ref[...] = jnp.full(m_ref.shape, -jnp.inf, F32)
    l_ref[...] = jnp.zeros_like(l_ref)
    acc_ref[...] = jnp.zeros_like(acc_ref)

    chains = [(qb, m) for qb in range(nsq) for m in range(2)]

    far_shift = rb_ref[(NUM_BUCKETS - 1) * n_heads + h] * log2e

    def produce(j, c):
        qb, m = chains[c]
        k0 = pl.multiple_of(j * tk, tk)
        s_ref[c] = _dot_nt(k_ref[pl.ds(k0, tk), m * d:(m + 1) * d],
                           q_ref[qb * sub:(qb + 1) * sub, m * d:(m + 1) * d])

    def softmax(j, c, far):
        qb, m = chains[c]
        if far:
            s = s_ref[c]
            shift = far_shift
        else:
            parts = []
            for kb in range(nsk):
                off = (qi * nsq + qb) - (j * nsk + kb)
                idx = jnp.where(off < 0, 3, jnp.minimum(off, 2))
                parts.append(s_ref[c, kb * sub:(kb + 1) * sub, :] + bias_ref[idx])
            s = jnp.concatenate(parts, axis=0)
            shift = 0.0
        cols = slice(qb * sub, (qb + 1) * sub)
        m_prev = m_ref[m, :, cols]
        m_new = jnp.maximum(m_prev, jnp.max(s, axis=0, keepdims=True) + shift)
        alpha = jnp.exp2(m_prev - m_new)
        pr = jnp.exp2(s - (m_new - shift))
        l_ref[m, :, cols] = alpha * l_ref[m, :, cols] + jnp.sum(pr, axis=0, keepdims=True)
        acc_ref[m, :, cols] = alpha * acc_ref[m, :, cols]
        m_ref[m, :, cols] = m_new
        return pr.astype(BF16)

    def add_values(j, c, p):
        qb, m = chains[c]
        cols = slice(qb * sub, (qb + 1) * sub)
        acc_ref[m, :, cols] += _dot(vt_ref[j], p)

    n_chains = len(chains)
    for c in range(n_chains):
        produce(0, c)
    p_ref[...] = softmax(0, 0, False)

    def trip(j, far):
        add_values(j, 0, p_ref[...])
        produce(j + 1, 0)
        for c in range(1, n_chains):
            add_values(j, c, softmax(j, c, far))
            produce(j + 1, c)
        p_ref[...] = softmax(j + 1, 0, far)

    last = (qi + 1) * (tq // tk) - 1
    n_far = jnp.maximum((qi * nsq - 1) // nsk - 1, 0)
    lax.fori_loop(0, n_far, lambda j, carry: (trip(j, True), carry)[1], 0)
    lax.fori_loop(n_far, last, lambda j, carry: (trip(j, False), carry)[1], 0)
    live = [c for c in range(n_chains) if chains[c][0] >= nsq - nsk]
    for c in live:
        add_values(last, c, p_ref[...] if c == 0 else softmax(last, c, False))

    lam = (jnp.exp(jnp.sum(lq1_ref[...] * lk1_ref[...], axis=-1, keepdims=True))
           - jnp.exp(jnp.sum(lq2_ref[...] * lk2_ref[...], axis=-1, keepdims=True))
           + lam_init)
    o = acc_ref[0] * (1.0 / l_ref[0]) - acc_ref[1] * (lam / l_ref[1])
    o = o * lax.rsqrt(jnp.mean(o * o, axis=0, keepdims=True) + 1e-5) * dfn_ref[...]
    y_ref[...] = (o * (1.0 - lam_init)).T.astype(y_ref.dtype)


def _diff_attn(proj3, v_t, rel_bias, lq1, lk1, lq2, lk2, df_norm_col, to_cast, *, n_heads, col0,
               tq, tk, lam_init):
    B, S, _ = proj3.shape
    d2 = 2 * DF_HEAD_DIM
    assert tq % tk == 0 and tk % ATTN_SUB == 0 and ATTN_SUB >= MAX_DISTANCE and S % tq == 0
    nq = S // tq
    n_chains = 2 * (tq // ATTN_SUB)
    cb = col0 // d2
    vec = lambda n: pl.BlockSpec((1, n), lambda b, h, i: (0, 0))
    n_steps = B * n_heads * nq

    def slab(w):
        rows = w.shape[0] // n_steps
        assert w.shape[0] % n_steps == 0 and rows % 16 == 0
        return pl.BlockSpec((rows, w.shape[1]), lambda b, h, i: ((b * n_heads + h) * nq + i, 0))

    return pl.pallas_call(
        functools.partial(_diff_attn_kernel, tq=tq, tk=tk, n_heads=n_heads, lam_init=lam_init),
        grid=(B, n_heads, nq),
        in_specs=[
            pl.BlockSpec(memory_space=pltpu.SMEM),
            pl.BlockSpec((None, tq, d2), lambda b, h, i: (b, i, cb + h)),
            pl.BlockSpec((None, S, d2), lambda b, h, i: (b, 0, cb + n_heads + h)),
            pl.BlockSpec((None, S // tk, d2, tk), lambda b, h, i: (b, 0, h, 0)),
            vec(DF_HEAD_DIM), vec(DF_HEAD_DIM), vec(DF_HEAD_DIM), vec(DF_HEAD_DIM),
            pl.BlockSpec((d2, 1), lambda b, h, i: (0, 0)),
        ] + [slab(w) for w in to_cast],
        out_specs=[pl.BlockSpec((None, tq, d2), lambda b, h, i: (b, i, h))]
        + [slab(w) for w in to_cast],
        out_shape=[jax.ShapeDtypeStruct((B, S, n_heads * d2), BF16)]
        + [jax.ShapeDtypeStruct(w.shape, BF16) for w in to_cast],
        scratch_shapes=[
            pltpu.VMEM((4, ATTN_SUB, ATTN_SUB), F32),
            pltpu.VMEM((n_chains, tk, ATTN_SUB), F32),
            pltpu.VMEM((tk, ATTN_SUB), BF16),
            pltpu.VMEM((2, 1, tq), F32),
            pltpu.VMEM((2, 1, tq), F32),
            pltpu.VMEM((2, d2, tq), F32),
        ],
        compiler_params=pltpu.CompilerParams(
            dimension_semantics=("arbitrary", "arbitrary", "arbitrary"),
            vmem_limit_bytes=V7X_VMEM_LIMIT),
        name="diff_attn",
    )(rel_bias, proj3, proj3, v_t, lq1, lk1, lq2, lk2, df_norm_col, *to_cast)


def _out_proj_kernel(x_ref, ya_ref, yb_ref, wa_ref, wb_ref, h_ref):
    h_ref[...] = x_ref[...] + _dot(ya_ref[...], wa_ref[...]) + _dot(yb_ref[...], wb_ref[...])


def _out_proj(x2, y_dn, y_df, w_o, *, tm, tn):
    T, D = x2.shape
    ka = y_dn.shape[1]
    kb = y_df.shape[1]
    assert ka == kb
    return pl.pallas_call(
        _out_proj_kernel,
        grid=(T // tm, D // tn),
        in_specs=[
            pl.BlockSpec((tm, tn), lambda i, j: (i, j)),
            pl.BlockSpec((tm, ka), lambda i, j: (i, 0)),
            pl.BlockSpec((tm, kb), lambda i, j: (i, 0)),
            pl.BlockSpec((ka, tn), lambda i, j: (0, j)),
            pl.BlockSpec((kb, tn), lambda i, j: (1, j)),
        ],
        out_specs=pl.BlockSpec((tm, tn), lambda i, j: (i, j)),
        out_shape=jax.ShapeDtypeStruct((T, D), F32),
        compiler_params=pltpu.CompilerParams(
            dimension_semantics=("arbitrary", "arbitrary"),
            vmem_limit_bytes=V7X_VMEM_LIMIT),
        name="out_proj",
    )(x2, y_dn, y_df, w_o, w_o)


def _mlp_kernel(h_ref, g_ref, wu_ref, wd_ref, gf_ref, o_ref, u_ref, acc_ref):
    f = pl.program_id(1)

    @pl.when(f == 0)
    def _():
        x = h_ref[...]
        ms = jnp.mean(x * x, axis=-1, keepdims=True)
        u_ref[...] = (x * lax.rsqrt(ms + 1e-6) * g_ref[...]).astype(BF16)
        acc_ref[...] = jnp.zeros_like(acc_ref)

    hid = jnp.maximum(_dot(u_ref[...], wu_ref[...]), 0.0)
    acc_ref[...] += _dot((hid * hid).astype(BF16), wd_ref[...])

    @pl.when(f == pl.num_programs(1) - 1)
    def _():
        y = h_ref[...] + acc_ref[...]
        ms = jnp.mean(y * y, axis=-1, keepdims=True)
        o_ref[...] = y * lax.rsqrt(ms + 1e-6) * gf_ref[...]


def _mlp(h1, gain, w_up, w_down, final_gain, *, tm, tf):
    T, D = h1.shape
    Fdim = w_up.shape[1]
    return pl.pallas_call(
        _mlp_kernel,
        grid=(T // tm, Fdim // tf),
        in_specs=[
            pl.BlockSpec((tm, D), lambda i, f: (i, 0)),
            pl.BlockSpec((1, D), lambda i, f: (0, 0)),
            pl.BlockSpec((D, tf), lambda i, f: (0, f)),
            pl.BlockSpec((tf, D), lambda i, f: (f, 0)),
            pl.BlockSpec((1, D), lambda i, f: (0, 0)),
        ],
        out_specs=pl.BlockSpec((tm, D), lambda i, f: (i, 0)),
        out_shape=jax.ShapeDtypeStruct((T, D), F32),
        scratch_shapes=[pltpu.VMEM((tm, D), BF16), pltpu.VMEM((tm, D), F32)],
        compiler_params=pltpu.CompilerParams(
            dimension_semantics=("arbitrary", "arbitrary"),
            vmem_limit_bytes=V7X_VMEM_LIMIT),
        name="mlp",
    )(h1, gain, w_up, w_down, final_gain)


def _tile(n, pref):
    if n <= pref:
        return n
    t = pref - pref % 128
    while t > 128 and n % t:
        t -= 128
    assert n % t == 0
    return t


def kernel(x, attn_norm, w_in, conv_w, a_log, dt_bias, dn_norm, lambda_q1, lambda_k1,
           lambda_q2, lambda_k2, df_norm, rel_bias, w_o, mlp_norm, w_up, w_down, final_norm):
    B, S, D = x.shape
    depth = attn_norm.shape[0]
    n_dn = a_log.shape[1]
    n_df = rel_bias.shape[1]
    dn_dim = n_dn * DN_HEAD_DIM
    df_dim = n_df * 2 * DF_HEAD_DIM
    T = B * S
    gate0 = 4 * dn_dim
    assert w_in.shape[2] == gate0 + 2 * n_dn + 3 * df_dim and 2 * n_dn <= GATE_COLS

    assert depth == 1
    l = 0
    h = x.reshape(T, D)

    wl = w_in[l]
    dfq0 = gate0 + 2 * n_dn
    w_all = wl.astype(BF16)
    w_b = w_all[:, dfq0:]

    proj, gates, v_t = _in_proj(h, attn_norm[l][None, :], w_all, w_b, conv_w[l],
                                batch=B, dn_dim=dn_dim, df_dim=df_dim, tm=_tile(S, IN_PROJ_ROWS),
                                tn=_tile(math.gcd(dn_dim, df_dim), IN_PROJ_COLS),
                                tv=_tile(S, ATTN_BLOCK),
                                q_scale=DF_HEAD_DIM ** -0.5 * math.log2(math.e))
    proj3 = proj.reshape(B, S, -1)
    gate3 = gates.reshape(B, S, GATE_COLS)

    gate_pad = ((0, 0), (n_dn, GATE_COLS - 2 * n_dn))
    alog_row = jnp.pad(a_log[l][None, :], gate_pad)
    dtb_row = jnp.pad(dt_bias[l][None, :], gate_pad)
    y_dn = _gdn(proj3, gate3, alog_row, dtb_row, dn_norm[l][None, :],
                n_heads=n_dn, blk=_tile(S, GDN_BLOCK))
    lam_init = 0.8 - 0.6 * math.exp(-0.3 * l)
    y_df, w_o_bf, w_up_bf, w_down_bf = _diff_attn(
                      proj3, v_t, rel_bias.reshape(-1),
                      lambda_q1[l][None, :], lambda_k1[l][None, :],
                      lambda_q2[l][None, :], lambda_k2[l][None, :], df_norm[l][:, None],
                      (w_o[l], w_up[l], w_down[l]),
                      n_heads=n_df, col0=gate0, tq=_tile(S, ATTN_QBLOCK), tk=_tile(S, ATTN_BLOCK),
                      lam_init=lam_init)

    h1 = _out_proj(h, y_dn.reshape(T, dn_dim), y_df.reshape(T, df_dim), w_o_bf,
                   tm=_tile(T, 512), tn=_tile(D, 2048))
    out = _mlp(h1, mlp_norm[l][None, :], w_up_bf, w_down_bf, final_norm[None, :],
               tm=_tile(T, 512), tf=_tile(w_up_bf.shape[1], 1024))
    return out.reshape(B, S, D)
```

```python
import functools
import math

import numpy as np
import jax
import jax.numpy as jnp
from jax import lax
from jax.experimental import pallas as pl
from jax.experimental.pallas import tpu as pltpu

F32 = jnp.float32
BF16 = jnp.bfloat16

DN_HEAD_DIM = 128
DF_HEAD_DIM = 128
CONV_WIDTH = 4
CHUNK = 64
NUM_BUCKETS = 32
MAX_DISTANCE = 128
GATE_COLS = 128
IN_PROJ_ROWS = 1024
IN_PROJ_COLS = 1024
IN_PROJ_SUB = 256
GDN_BLOCK = 1024
GDN_GROUP = 4
ATTN_SUB = 256
ATTN_BLOCK = 512
ATTN_QBLOCK = 1024

V7X_VMEM_LIMIT = 58 * 1024 * 1024


def _dot(a, b):
    return jnp.dot(a, b, preferred_element_type=F32)


def _dot_nt(a, b):
    return lax.dot_general(a, b, (((1,), (1,)), ((), ())), preferred_element_type=F32)


def _sigmoid(x):
    return 1.0 / (1.0 + jnp.exp(-x))


def _in_proj_kernel(x_ref, g_ref, wa_ref, wb_ref, wg_ref, cw_ref, proj_ref, gate_ref, vt_ref,
                    u_ref, hist_ref, cbuf_ref, *, bounds, per_seq, q_scale):
    i = pl.program_id(0)
    j = pl.program_id(1)
    tm, tn = proj_ref.shape
    b0, b1, b2, b3, b4, b5 = bounds

    @pl.when(j == 0)
    def _():
        x = x_ref[...]
        ms = jnp.mean(x * x, axis=-1, keepdims=True)
        u = (x * lax.rsqrt(ms + 1e-6) * g_ref[...]).astype(BF16)
        u_ref[...] = u
        gate_ref[...] = _dot(u, wg_ref[...])

    n_col_chunks = tn // IN_PROJ_SUB
    first_of_seq = (i % per_seq) == 0

    def chunk_cols(c):
        return slice(c * IN_PROJ_SUB, (c + 1) * IN_PROJ_SUB)

    def raw_cols(w_ref, c):
        cbuf_ref[8:8 + tm, chunk_cols(c)] = _dot(u_ref[...], w_ref[:, chunk_cols(c)])

    def raw(c):
        return cbuf_ref[8:8 + tm, chunk_cols(c)]

    def conv_silu(c):
        cw = cw_ref[:, chunk_cols(c)]
        y = None
        for s in range(CONV_WIDTH):
            tap = CONV_WIDTH - 1 - s
            term = cbuf_ref[8 - s:8 - s + tm, chunk_cols(c)] * cw[tap:tap + 1, :]
            y = term if y is None else y + term
        return y * _sigmoid(y)

    def l2norm_heads(y, scale):
        outs = []
        for h in range(IN_PROJ_SUB // DN_HEAD_DIM):
            yh = y[:, h * DN_HEAD_DIM:(h + 1) * DN_HEAD_DIM]
            outs.append(yh * (lax.rsqrt(jnp.sum(yh * yh, axis=-1, keepdims=True) + 1e-6) * scale))
        return jnp.concatenate(outs, axis=1)

    def silu_cols(c):
        z = raw(c)
        return z * _sigmoid(z)

    def project(w_ref, epilogue, conv=False):
        if conv:
            cbuf_ref[0:8, :] = jnp.where(first_of_seq, 0.0, hist_ref[j])
        raw_cols(w_ref, 0)
        for c in range(n_col_chunks):
            if c + 1 < n_col_chunks:
                raw_cols(w_ref, c + 1)
            proj_ref[:, chunk_cols(c)] = epilogue(c).astype(BF16)
        if conv:
            hist_ref[j] = cbuf_ref[tm:tm + 8, :]

    @pl.when(j < b0)
    def _():
        project(wa_ref, lambda c: l2norm_heads(conv_silu(c), DN_HEAD_DIM ** -0.5), conv=True)

    @pl.when((j >= b0) & (j < b1))
    def _():
        project(wa_ref, lambda c: l2norm_heads(conv_silu(c), 1.0), conv=True)

    @pl.when((j >= b1) & (j < b2))
    def _():
        project(wa_ref, conv_silu, conv=True)

    @pl.when((j >= b2) & (j < b3))
    def _():
        project(wa_ref, silu_cols)

    @pl.when((j >= b3) & (j < b4))
    def _():
        project(wb_ref, lambda c: raw(c) * q_scale)

    @pl.when((j >= b4) & (j < b5))
    def _():
        project(wb_ref, raw)

    @pl.when(j >= b5)
    def _():
        tv = vt_ref.shape[-1]
        raw_cols(wb_ref, 0)
        for c in range(n_col_chunks):
            if c + 1 < n_col_chunks:
                raw_cols(wb_ref, c + 1)
            vt = raw(c).T.astype(BF16)
            for tb in range(vt_ref.shape[0]):
                vt_ref[tb, chunk_cols(c), :] = vt[:, tb * tv:(tb + 1) * tv]


def _in_proj(x2, gain, w_all, w_b, conv_w, *, batch, dn_dim, df_dim, tm, tn, tv, q_scale):
    T, D = x2.shape
    N = 4 * dn_dim + 2 * df_dim
    nv = df_dim
    S = T // batch
    assert w_b.shape[1] == 3 * df_dim
    assert (4 * dn_dim) % GATE_COLS == 0 and w_all.shape[1] >= 4 * dn_dim + GATE_COLS
    assert S % tm == 0 and dn_dim % tn == 0 and df_dim % tn == 0 and tm % tv == 0
    assert tn % DN_HEAD_DIM == 0 and conv_w.shape == (CONV_WIDTH, 3 * dn_dim)
    assert tn % IN_PROJ_SUB == 0 and IN_PROJ_SUB % DN_HEAD_DIM == 0
    n_dn, n_df = dn_dim // tn, df_dim // tn
    bounds = (n_dn, 2 * n_dn, 3 * n_dn, 4 * n_dn, 4 * n_dn + n_df, 4 * n_dn + 2 * n_df)
    n_main = bounds[-1]
    n_conv = bounds[2]
    n_a = bounds[3]
    n_b = n_main - n_a + nv // tn
    per_seq = S // tm
    return pl.pallas_call(
        functools.partial(_in_proj_kernel, bounds=bounds, per_seq=per_seq, q_scale=q_scale),
        grid=(T // tm, n_main + nv // tn),
        in_specs=[
            pl.BlockSpec((tm, D), lambda i, j: (i, 0)),
            pl.BlockSpec((1, D), lambda i, j: (0, 0)),
            pl.BlockSpec((D, tn), lambda i, j: (0, jnp.where(j < n_a, j, 0))),
            pl.BlockSpec((D, tn), lambda i, j: (0, jnp.where(j < n_a, n_b - 1,
                                                             jnp.minimum(j - n_a, n_b - 1)))),
            pl.BlockSpec((D, GATE_COLS), lambda i, j: (0, 4 * dn_dim // GATE_COLS),
                         pipeline_mode=pl.Buffered(1)),
            pl.BlockSpec((CONV_WIDTH, tn), lambda i, j: (0, jnp.minimum(j, n_conv - 1))),
        ],
        out_specs=[
            pl.BlockSpec((tm, tn), lambda i, j: (i, jnp.minimum(j, n_main - 1))),
            pl.BlockSpec((tm, GATE_COLS), lambda i, j: (i, 0)),
            pl.BlockSpec((None, tm // tv, tn, tv),
                         lambda i, j: (i // per_seq, i % per_seq, jnp.maximum(j - n_main, 0), 0)),
        ],
        out_shape=[
            jax.ShapeDtypeStruct((T, N), BF16),
            jax.ShapeDtypeStruct((T, GATE_COLS), F32),
            jax.ShapeDtypeStruct((batch, S // tv, nv, tv), BF16),
        ],
        scratch_shapes=[
            pltpu.VMEM((tm, D), BF16),
            pltpu.VMEM((n_conv, 8, tn), F32),
            pltpu.VMEM((8 + tm, tn), F32),
        ],
        compiler_params=pltpu.CompilerParams(
            dimension_semantics=("arbitrary", "arbitrary"),
            vmem_limit_bytes=V7X_VMEM_LIMIT),
        name="in_proj",
    )(x2, gain, w_all, w_b, w_all, conv_w)


def _gdn_kernel(q_ref, k_ref, v_ref, z_ref, gate_ref, alog_ref, dtb_ref, dnn_ref,
                y_ref, state_ref, *, blk, n_heads):
    dk = DN_HEAD_DIM
    n_chunks = blk // CHUNK

    @pl.when(pl.program_id(1) == 0)
    def _():
        state_ref[...] = jnp.zeros_like(state_ref)

    gate = gate_ref[...]
    beta_all = _sigmoid(gate)
    xs = gate + dtb_ref[...]
    softplus = jnp.maximum(xs, 0.0) + jnp.log(1.0 + jnp.exp(-jnp.abs(xs)))
    g_all = -jnp.exp(alog_ref[...]) * softplus
    pos = lax.broadcasted_iota(jnp.int32, g_all.shape, 0) & (CHUNK - 1)
    gc_all = g_all
    step = 1
    while step < CHUNK:
        gc_all = gc_all + jnp.where(pos >= step, pltpu.roll(gc_all, step, 0), 0.0)
        step *= 2

    ri = lax.broadcasted_iota(jnp.int32, (CHUNK, CHUNK), 0)
    ci = lax.broadcasted_iota(jnp.int32, (CHUNK, CHUNK), 1)
    tril = ri >= ci
    eye = ri == ci
    pair_masks = [(ri > ci) & ((ri >> 1) == (ci >> 1))]
    size = 2
    while size < CHUNK:
        pair_masks.append(((ri // (2 * size)) == (ci // (2 * size))) & ((ri // size) != (ci // size))
                          & (ri > ci))
        size *= 2
    gain = dnn_ref[...]

    def first_stage(c):
        rows = slice(c * CHUNK, (c + 1) * CHUNK)
        items = []
        for h in range(n_heads):
            cols = slice(h * dk, (h + 1) * dk)
            q_bf = q_ref[rows, cols]
            k_bf = k_ref[rows, cols]
            qc = q_bf.astype(F32)
            kc = k_bf.astype(F32)
            vc = v_ref[rows, cols].astype(F32)
            bc = jnp.broadcast_to(beta_all[rows, h:h + 1], (CHUNK, dk))
            gcc = jnp.broadcast_to(gc_all[rows, n_heads + h:n_heads + h + 1], (CHUNK, dk))
            g_last = gcc[CHUNK - 1:CHUNK, :]
            eg = jnp.exp(gcc)
            g_sq = gcc[:, 0:CHUNK]
            g_row = jnp.sum(jnp.where(eye, g_sq, 0.0), axis=0, keepdims=True)
            decay = jnp.exp(jnp.where(tril, g_sq - g_row, -jnp.inf))
            kb = kc * bc
            lhs = jnp.concatenate([kb.astype(BF16), q_bf], axis=0)
            aq = _dot_nt(lhs, k_bf)
            l_mat = aq[:CHUNK] * decay
            items.append(dict(
                h=h, qe=qc * eg, g_last=g_last,
                rhs=jnp.concatenate([kb * eg, vc * bc], axis=1).astype(BF16),
                qk=jnp.where(tril, aq[CHUNK:] * decay, 0.0).astype(BF16),
                kd_t=(kc * jnp.exp(g_last - gcc)).T.astype(BF16),
                inv=jnp.where(eye, 1.0, 0.0) - jnp.where(pair_masks[0], l_mat, 0.0),
                off=[jnp.where(m, l_mat, 0.0).astype(BF16) for m in pair_masks[1:]]))
        return items

    def matrix_stages(items):
        for level in range(len(pair_masks) - 1):
            for s in items:
                s["x"] = _dot(s["off"][level], s["inv"].astype(BF16)).astype(BF16)
            for s in items:
                s["inv"] = s["inv"] - _dot(s["inv"].astype(BF16), s["x"])
        for s in items:
            s["wu"] = _dot(s["inv"].astype(BF16), s["rhs"]).astype(BF16)
        for s in items:
            s["gr"] = _dot(s["kd_t"], s["wu"])
            qw = _dot(s["qk"], s["wu"])
            s["q_eff"] = (s["qe"] - qw[:, :dk]).astype(BF16)
            s["p_loc"] = qw[:, dk:]

    def state_stage(c, items):
        rows = slice(c * CHUNK, (c + 1) * CHUNK)
        for s in items:
            h = s["h"]
            state = state_ref[h]
            s_bf = state.astype(BF16)
            o = _dot(s["q_eff"], s_bf) + s["p_loc"]
            state_ref[h] = (state * jnp.exp(s["g_last"])
                            - _dot(s["gr"][:, :dk].astype(BF16), s_bf) + s["gr"][:, dk:])
            zs = z_ref[rows, h * dk:(h + 1) * dk].astype(F32)
            o = o * lax.rsqrt(jnp.mean(o * o, axis=-1, keepdims=True) + 1e-6) * gain
            y_ref[rows, h * dk:(h + 1) * dk] = (o * zs).astype(y_ref.dtype)

    groups = [list(range(g, min(g + GDN_GROUP, n_chunks))) for g in range(0, n_chunks, GDN_GROUP)]
    cur = [first_stage(c) for c in groups[0]]
    for gi, chunk_ids in enumerate(groups):
        nxt = [first_stage(c) for c in groups[gi + 1]] if gi + 1 < len(groups) else None
        matrix_stages([s for items in cur for s in items])
        for c, items in zip(chunk_ids, cur):
            state_stage(c, items)
        cur = nxt


def _gdn(proj3, gate3, alog_row, dtb_row, dn_norm, *, n_heads, blk):
    B, S, _ = proj3.shape
    dk = DN_HEAD_DIM
    dn = n_heads * dk

    def group(idx):
        return pl.BlockSpec((None, blk, dn), lambda b, t: (b, t, idx))

    def whole(arr):
        return pl.BlockSpec(arr.shape, lambda b, t: (0,) * arr.ndim)

    return pl.pallas_call(
        functools.partial(_gdn_kernel, blk=blk, n_heads=n_heads),
        grid=(B, S // blk),
        in_specs=[
            group(0), group(1), group(2), group(3),
            pl.BlockSpec((None, blk, GATE_COLS), lambda b, t: (b, t, 0)),
            whole(alog_row), whole(dtb_row), whole(dn_norm),
        ],
        out_specs=pl.BlockSpec((None, blk, dn), lambda b, t: (b, t, 0)),
        out_shape=jax.ShapeDtypeStruct((B, S, dn), BF16),
        scratch_shapes=[pltpu.VMEM((n_heads, dk, dk), F32)],
        compiler_params=pltpu.CompilerParams(
            dimension_semantics=("arbitrary", "arbitrary"),
            vmem_limit_bytes=V7X_VMEM_LIMIT),
        name="gdn",
    )(proj3, proj3, proj3, proj3, gate3, alog_row, dtb_row, dn_norm)


def _t5_bucket_starts():
    max_exact = NUM_BUCKETS // 2
    n = np.arange(0, MAX_DISTANCE + 1)
    nf = np.maximum(n, 1).astype(np.float32)
    large = max_exact + (np.log(nf / max_exact) / math.log(MAX_DISTANCE / max_exact)
                         * (NUM_BUCKETS - max_exact)).astype(np.int32)
    bucket = np.where(n < max_exact, n, np.minimum(large, NUM_BUCKETS - 1))
    assert bucket[MAX_DISTANCE] == NUM_BUCKETS - 1 and np.all(np.diff(bucket) >= 0)
    starts = [(0, int(bucket[0]))]
    for d in range(1, MAX_DISTANCE + 1):
        if bucket[d] != bucket[d - 1]:
            starts.append((d, int(bucket[d])))
    return starts


def _diff_attn_kernel(rb_ref, q_ref, k_ref, vt_ref, lq1_ref, lk1_ref, lq2_ref, lk2_ref, dfn_ref,
                      c0_ref, c1_ref, c2_ref, y_ref, o0_ref, o1_ref, o2_ref,
                      bias_ref, s_ref, p_ref, m_ref, l_ref, acc_ref,
                      *, tq, tk, n_heads, lam_init):
    h = pl.program_id(1)
    qi = pl.program_id(2)
    d = DF_HEAD_DIM

    for c_ref, o_ref in ((c0_ref, o0_ref), (c1_ref, o1_ref), (c2_ref, o2_ref)):
        o_ref[...] = c_ref[...].astype(o_ref.dtype)

    sub = ATTN_SUB
    nsq = tq // sub
    nsk = tk // sub
    log2e = math.log2(math.e)

    @pl.when(qi == 0)
    def _():
        keys = lax.broadcasted_iota(jnp.int32, (sub, sub), 0)
        qrys = lax.broadcasted_iota(jnp.int32, (sub, sub), 1)
        starts = _t5_bucket_starts()
        for idx in range(2):
            dist = qrys - keys + idx * sub
            tile = jnp.full((sub, sub), rb_ref[starts[0][1] * n_heads + h], F32)
            for first, bucket in starts[1:]:
                tile = jnp.where(dist >= first, rb_ref[bucket * n_heads + h], tile)
            tile = tile * log2e
            if idx == 0:
                tile = jnp.where(dist >= 0, tile, -jnp.inf)
            bias_ref[idx] = tile
        bias_ref[2] = jnp.full((sub, sub), rb_ref[(NUM_BUCKETS - 1) * n_heads + h] * log2e, F32)
        bias_ref[3] = jnp.full((sub, sub), -jnp.inf, F32)

    m_ref[...] = jnp.full(m_ref.shape, -jnp.inf, F32)
    l_ref[...] = jnp.zeros_like(l_ref)
    acc_ref[...] = jnp.zeros_like(acc_ref)

    chains = [(qb, m) for qb in range(nsq) for m in range(2)]

    far_shift = rb_ref[(NUM_BUCKETS - 1) * n_heads + h] * log2e

    def produce(j, c):
        qb, m = chains[c]
        k0 = pl.multiple_of(j * tk, tk)
        s_ref[c] = _dot_nt(k_ref[pl.ds(k0, tk), m * d:(m + 1) * d],
                           q_ref[qb * sub:(qb + 1) * sub, m * d:(m + 1) * d])

    def softmax(j, c, far):
        qb, m = chains[c]
        if far:
            s = s_ref[c]
            shift = far_shift
        else:
            parts = []
            for kb in range(nsk):
                off = (qi * nsq + qb) - (j * nsk + kb)
                idx = jnp.where(off < 0, 3, jnp.minimum(off, 2))
                parts.append(s_ref[c, kb * sub:(kb + 1) * sub, :] + bias_ref[idx])
            s = jnp.concatenate(parts, axis=0)
            shift = 0.0
        cols = slice(qb * sub, (qb + 1) * sub)
        m_prev = m_ref[m, :, cols]
        m_new = jnp.maximum(m_prev, jnp.max(s, axis=0, keepdims=True) + shift)
        alpha = jnp.exp2(m_prev - m_new)
        pr = jnp.exp2(s - (m_new - shift))
        l_ref[m, :, cols] = alpha * l_ref[m, :, cols] + jnp.sum(pr, axis=0, keepdims=True)
        acc_ref[m, :, cols] = alpha * acc_ref[m, :, cols]
        m_ref[m, :, cols] = m_new
        return pr.astype(BF16)

    def add_values(j, c, p):
        qb, m = chains[c]
        cols = slice(qb * sub, (qb + 1) * sub)
        acc_ref[m, :, cols] += _dot(vt_ref[j], p)

    n_chains = len(chains)
    for c in range(n_chains):
        produce(0, c)
    p_ref[...] = softmax(0, 0, False)

    def trip(j, far):
        add_values(j, 0, p_ref[...])
        produce(j + 1, 0)
        for c in range(1, n_chains):
            add_values(j, c, softmax(j, c, far))
            produce(j + 1, c)
        p_ref[...] = softmax(j + 1, 0, far)

    last = (qi + 1) * (tq // tk) - 1
    n_far = jnp.maximum((qi * nsq - 1) // nsk - 1, 0)
    n_near = last - n_far

    def sweep(first, count, far):
        def pair(t, carry):
            trip(first + 2 * t, far)
            trip(first + 2 * t + 1, far)
            return carry

        lax.fori_loop(0, count // 2, pair, 0)

        @pl.when(count % 2 == 1)
        def _():
            trip(first + count - 1, far)

    sweep(0, n_far, True)
    sweep(n_far, n_near, False)
    live = [c for c in range(n_chains) if chains[c][0] >= nsq - nsk]
    for c in live:
        add_values(last, c, p_ref[...] if c == 0 else softmax(last, c, False))

    lam = (jnp.exp(jnp.sum(lq1_ref[...] * lk1_ref[...], axis=-1, keepdims=True))
           - jnp.exp(jnp.sum(lq2_ref[...] * lk2_ref[...], axis=-1, keepdims=True))
           + lam_init)
    o = acc_ref[0] * (1.0 / l_ref[0]) - acc_ref[1] * (lam / l_ref[1])
    o = o * lax.rsqrt(jnp.mean(o * o, axis=0, keepdims=True) + 1e-5) * dfn_ref[...]
    y_ref[...] = (o * (1.0 - lam_init)).T.astype(y_ref.dtype)


def _diff_attn(proj3, v_t, rel_bias, lq1, lk1, lq2, lk2, df_norm_col, to_cast, *, n_heads, col0,
               tq, tk, lam_init):
    B, S, _ = proj3.shape
    d2 = 2 * DF_HEAD_DIM
    assert tq % tk == 0 and tk % ATTN_SUB == 0 and ATTN_SUB >= MAX_DISTANCE and S % tq == 0
    nq = S // tq
    n_chains = 2 * (tq // ATTN_SUB)
    cb = col0 // d2
    vec = lambda n: pl.BlockSpec((1, n), lambda b, h, i: (0, 0))
    n_steps = B * n_heads * nq

    def slab(w):
        rows = w.shape[0] // n_steps
        assert w.shape[0] % n_steps == 0 and rows % 16 == 0
        return pl.BlockSpec((rows, w.shape[1]), lambda b, h, i: ((b * n_heads + h) * nq + i, 0))

    return pl.pallas_call(
        functools.partial(_diff_attn_kernel, tq=tq, tk=tk, n_heads=n_heads, lam_init=lam_init),
        grid=(B, n_heads, nq),
        in_specs=[
            pl.BlockSpec(memory_space=pltpu.SMEM),
            pl.BlockSpec((None, tq, d2), lambda b, h, i: (b, i, cb + h)),
            pl.BlockSpec((None, S, d2), lambda b, h, i: (b, 0, cb + n_heads + h)),
            pl.BlockSpec((None, S // tk, d2, tk), lambda b, h, i: (b, 0, h, 0)),
            vec(DF_HEAD_DIM), vec(DF_HEAD_DIM), vec(DF_HEAD_DIM), vec(DF_HEAD_DIM),
            pl.BlockSpec((d2, 1), lambda b, h, i: (0, 0)),
        ] + [slab(w) for w in to_cast],
        out_specs=[pl.BlockSpec((None, tq, d2), lambda b, h, i: (b, i, h))]
        + [slab(w) for w in to_cast],
        out_shape=[jax.ShapeDtypeStruct((B, S, n_heads * d2), BF16)]
        + [jax.ShapeDtypeStruct(w.shape, BF16) for w in to_cast],
        scratch_shapes=[
            pltpu.VMEM((4, ATTN_SUB, ATTN_SUB), F32),
            pltpu.VMEM((n_chains, tk, ATTN_SUB), F32),
            pltpu.VMEM((tk, ATTN_SUB), BF16),
            pltpu.VMEM((2, 1, tq), F32),
            pltpu.VMEM((2, 1, tq), F32),
            pltpu.VMEM((2, d2, tq), F32),
        ],
        compiler_params=pltpu.CompilerParams(
            dimension_semantics=("arbitrary", "arbitrary", "arbitrary"),
            vmem_limit_bytes=V7X_VMEM_LIMIT),
        name="diff_attn",
    )(rel_bias, proj3, proj3, v_t, lq1, lk1, lq2, lk2, df_norm_col, *to_cast)


def _out_proj_kernel(x_ref, ya_ref, yb_ref, wa_ref, wb_ref, g_ref, h_ref, u_ref):
    h = x_ref[...] + _dot(ya_ref[...], wa_ref[...]) + _dot(yb_ref[...], wb_ref[...])
    h_ref[...] = h
    ms = jnp.mean(h * h, axis=-1, keepdims=True)
    u_ref[...] = (h * lax.rsqrt(ms + 1e-6) * g_ref[...]).astype(BF16)


def _out_proj(x2, y_dn, y_df, w_o, gain, *, tm, tn):
    T, D = x2.shape
    ka = y_dn.shape[1]
    kb = y_df.shape[1]
    assert ka == kb and tn == D
    return pl.pallas_call(
        _out_proj_kernel,
        grid=(T // tm, D // tn),
        in_specs=[
            pl.BlockSpec((tm, tn), lambda i, j: (i, j)),
            pl.BlockSpec((tm, ka), lambda i, j: (i, 0)),
            pl.BlockSpec((tm, kb), lambda i, j: (i, 0)),
            pl.BlockSpec((ka, tn), lambda i, j: (0, j)),
            pl.BlockSpec((kb, tn), lambda i, j: (1, j)),
            pl.BlockSpec((1, tn), lambda i, j: (0, j)),
        ],
        out_specs=[pl.BlockSpec((tm, tn), lambda i, j: (i, j)),
                   pl.BlockSpec((tm, tn), lambda i, j: (i, j))],
        out_shape=[jax.ShapeDtypeStruct((T, D), F32), jax.ShapeDtypeStruct((T, D), BF16)],
        compiler_params=pltpu.CompilerParams(
            dimension_semantics=("arbitrary", "arbitrary"),
            vmem_limit_bytes=V7X_VMEM_LIMIT),
        name="out_proj",
    )(x2, y_dn, y_df, w_o, w_o, gain)


def _mlp_kernel(h_ref, u_ref, wu_ref, wd_ref, gf_ref, o_ref, acc_ref):
    f = pl.program_id(1)

    @pl.when(f == 0)
    def _():
        acc_ref[...] = jnp.zeros_like(acc_ref)

    hid = jnp.maximum(_dot(u_ref[...], wu_ref[...]), 0.0)
    acc_ref[...] += _dot((hid * hid).astype(BF16), wd_ref[...])

    @pl.when(f == pl.num_programs(1) - 1)
    def _():
        y = h_ref[...] + acc_ref[...]
        ms = jnp.mean(y * y, axis=-1, keepdims=True)
        o_ref[...] = y * lax.rsqrt(ms + 1e-6) * gf_ref[...]


def _mlp(h1, u1, w_up, w_down, final_gain, *, tm, tf):
    T, D = h1.shape
    Fdim = w_up.shape[1]
    return pl.pallas_call(
        _mlp_kernel,
        grid=(T // tm, Fdim // tf),
        in_specs=[
            pl.BlockSpec((tm, D), lambda i, f: (i, 0)),
            pl.BlockSpec((tm, D), lambda i, f: (i, 0)),
            pl.BlockSpec((D, tf), lambda i, f: (0, f)),
            pl.BlockSpec((tf, D), lambda i, f: (f, 0)),
            pl.BlockSpec((1, D), lambda i, f: (0, 0)),
        ],
        out_specs=pl.BlockSpec((tm, D), lambda i, f: (i, 0)),
        out_shape=jax.ShapeDtypeStruct((T, D), F32),
        scratch_shapes=[pltpu.VMEM((tm, D), F32)],
        compiler_params=pltpu.CompilerParams(
            dimension_semantics=("arbitrary", "arbitrary"),
            vmem_limit_bytes=V7X_VMEM_LIMIT),
        name="mlp",
    )(h1, u1, w_up, w_down, final_gain)


def _tile(n, pref):
    if n <= pref:
        return n
    t = pref - pref % 128
    while t > 128 and n % t:
        t -= 128
    assert n % t == 0
    return t


def kernel(x, attn_norm, w_in, conv_w, a_log, dt_bias, dn_norm, lambda_q1, lambda_k1,
           lambda_q2, lambda_k2, df_norm, rel_bias, w_o, mlp_norm, w_up, w_down, final_norm):
    B, S, D = x.shape
    depth = attn_norm.shape[0]
    n_dn = a_log.shape[1]
    n_df = rel_bias.shape[1]
    dn_dim = n_dn * DN_HEAD_DIM
    df_dim = n_df * 2 * DF_HEAD_DIM
    T = B * S
    gate0 = 4 * dn_dim
    assert w_in.shape[2] == gate0 + 2 * n_dn + 3 * df_dim and 2 * n_dn <= GATE_COLS

    assert depth == 1
    l = 0
    h = x.reshape(T, D)

    wl = w_in[l]
    dfq0 = gate0 + 2 * n_dn
    w_all = wl.astype(BF16)
    w_b = w_all[:, dfq0:]

    proj, gates, v_t = _in_proj(h, attn_norm[l][None, :], w_all, w_b, conv_w[l],
                                batch=B, dn_dim=dn_dim, df_dim=df_dim, tm=_tile(S, IN_PROJ_ROWS),
                                tn=_tile(math.gcd(dn_dim, df_dim), IN_PROJ_COLS),
                                tv=_tile(S, ATTN_BLOCK),
                                q_scale=DF_HEAD_DIM ** -0.5 * math.log2(math.e))
    proj3 = proj.reshape(B, S, -1)
    gate3 = gates.reshape(B, S, GATE_COLS)

    gate_pad = ((0, 0), (n_dn, GATE_COLS - 2 * n_dn))
    alog_row = jnp.pad(a_log[l][None, :], gate_pad)
    dtb_row = jnp.pad(dt_bias[l][None, :], gate_pad)
    y_dn = _gdn(proj3, gate3, alog_row, dtb_row, dn_norm[l][None, :],
                n_heads=n_dn, blk=_tile(S, GDN_BLOCK))
    lam_init = 0.8 - 0.6 * math.exp(-0.3 * l)
    y_df, w_o_bf, w_up_bf, w_down_bf = _diff_attn(
                      proj3, v_t, rel_bias.reshape(-1),
                      lambda_q1[l][None, :], lambda_k1[l][None, :],
                      lambda_q2[l][None, :], lambda_k2[l][None, :], df_norm[l][:, None],
                      (w_o[l], w_up[l], w_down[l]),
                      n_heads=n_df, col0=gate0, tq=_tile(S, ATTN_QBLOCK), tk=_tile(S, ATTN_BLOCK),
                      lam_init=lam_init)

    h1, u1 = _out_proj(h, y_dn.reshape(T, dn_dim), y_df.reshape(T, df_dim), w_o_bf,
                       mlp_norm[l][None, :], tm=_tile(T, 512), tn=D)
    out = _mlp(h1, u1, w_up_bf, w_down_bf, final_norm[None, :],
               tm=_tile(T, 512), tf=_tile(w_up_bf.shape[1], 1024))
    return out.reshape(B, S, D)
```
